```python
import math
import jax
import jax.numpy as jnp
from jax import lax
import numpy as np

D_MODEL = 1024
BATCH = 16
SEQ = 256
DEPTH = 4
DEC_BATCH = 4
DEC_SEQ = 1024
PAST_LEN = 256

F32 = jnp.float32
GRID_W = 64
HEAD_DIM = 64
HEADS_PER_GROUP = D_MODEL // (4 * HEAD_DIM)
MIX_WIDTH = 4 * HEADS_PER_GROUP * HEAD_DIM
WIN_HEADS = HEADS_PER_GROUP
WIN_KV_HEADS = WIN_HEADS // 2
WINDOW = 128
BLOCK = 128
DIFF_HEADS = HEADS_PER_GROUP
DIFF_HALF = HEAD_DIM // 2
MLA_HEADS = HEADS_PER_GROUP
MLA_Q_RANK = 3 * D_MODEL // 16
MLA_KV_RANK = D_MODEL // 8
MLA_NOPE = HEAD_DIM
MLA_ROPE = HEAD_DIM // 2
MLA_VDIM = HEAD_DIM
NA_HEADS = HEADS_PER_GROUP
NA_ROWS = 8
NA_COLS = 16
D_FF = ((8 * D_MODEL // 3 + 127) // 128) * 128
N_EXPERTS = 8
TOP_K = 2
D_FF_EXPERT = 7 * D_MODEL // 2
ROPE_THETA = 10000.0
EPS = 1e-6
NEG_INF = -1e30
DENSE_KEY_LIMIT = 2048
Q_BLOCK = 128
IN_SIZES = (WIN_HEADS * HEAD_DIM, WIN_KV_HEADS * HEAD_DIM, WIN_KV_HEADS * HEAD_DIM,
            DIFF_HEADS * HEAD_DIM, DIFF_HEADS * HEAD_DIM, DIFF_HEADS * HEAD_DIM,
            MLA_Q_RANK, MLA_KV_RANK, MLA_ROPE,
            NA_HEADS * HEAD_DIM, NA_HEADS * HEAD_DIM, NA_HEADS * HEAD_DIM)
IN_WIDTH = sum(IN_SIZES)

kernel_name = 'hybrid_diffusion_parallel_heads_step'


def _rmsnorm(x, g):
    x32 = x.astype(F32)
    y = x32 * lax.rsqrt(jnp.mean(x32 * x32, axis=-1, keepdims=True) + EPS)
    return (y * g.astype(F32)).astype(x.dtype)


def _heads(t, n):
    b, l, _ = t.shape
    return t.reshape(b, l, n, -1).transpose(0, 2, 1, 3)


def _merge(o):
    b, h, l, d = o.shape
    return o.transpose(0, 2, 1, 3).reshape(b, l, h * d)


def _project(h, w_in_l):
    pts = []
    acc = 0
    for s in IN_SIZES[:-1]:
        acc += s
        pts.append(acc)
    return jnp.split(h @ w_in_l, pts, axis=-1)


def _rope1d(x, pos):
    d = x.shape[-1]
    inv = ROPE_THETA ** (-jnp.arange(0, d, 2, dtype=F32) / d)
    ang = pos.astype(F32)[:, None] * inv[None, :]
    cos = jnp.concatenate([jnp.cos(ang), jnp.cos(ang)], axis=-1)
    sin = jnp.concatenate([jnp.sin(ang), jnp.sin(ang)], axis=-1)
    rot = jnp.concatenate([-x[..., d // 2:], x[..., :d // 2]], axis=-1)
    return (x.astype(F32) * cos + rot.astype(F32) * sin).astype(x.dtype)


def _rope2d(x, pos_r, pos_c):
    h = x.shape[-1] // 2
    return jnp.concatenate([_rope1d(x[..., :h], pos_r), _rope1d(x[..., h:], pos_c)], axis=-1)


def _sdpa(q, k, v, sink=None):
    scale = q.shape[-1] ** -0.5

    def attend(qb):
        s = jnp.einsum('bhqd,bhkd->bhqk', qb, k).astype(F32) * scale
        if sink is not None:
            sk = jnp.broadcast_to(sink.astype(F32)[None, :, None, None], s.shape[:-1] + (1,))
            s = jnp.concatenate([s, sk], axis=-1)
        p = jax.nn.softmax(s, axis=-1)
        if sink is not None:
            p = p[..., :-1]
        return jnp.einsum('bhqk,bhkd->bhqd', p.astype(v.dtype), v)

    b, h, lq, _ = q.shape
    if k.shape[2] >= DENSE_KEY_LIMIT and lq > Q_BLOCK and lq % Q_BLOCK == 0:
        nb = lq // Q_BLOCK
        qb = q.reshape(b, h, nb, Q_BLOCK, -1).transpose(2, 0, 1, 3, 4)
        o = lax.map(attend, qb)
        return o.transpose(1, 2, 0, 3, 4).reshape(b, h, lq, -1)
    return attend(q)


def _window_attn(q, k, v, kc, vc, sink):
    b, h, n, d = q.shape
    g = k.shape[1]
    r = h // g
    nb = n // BLOCK
    scale = d ** -0.5
    qb = q.reshape(b, g, r, nb, BLOCK, d)
    pad = ((0, 0), (0, 0), (BLOCK, BLOCK), (0, 0))
    kp = jnp.pad(k, pad).reshape(b, g, nb + 2, BLOCK, d)
    vp = jnp.pad(v, pad).reshape(b, g, nb + 2, BLOCK, d)
    kw = jnp.concatenate([kp[:, :, :-2], kp[:, :, 1:-1], kp[:, :, 2:]], axis=3)
    vw = jnp.concatenate([vp[:, :, :-2], vp[:, :, 1:-1], vp[:, :, 2:]], axis=3)
    qi = jnp.arange(BLOCK)[:, None]
    kj = jnp.arange(3 * BLOCK)[None, :]
    kpos = jnp.arange(nb)[:, None, None] * BLOCK - BLOCK + kj[None]
    valid = (jnp.abs(kj - BLOCK - qi) <= WINDOW)[None] & (kpos >= 0) & (kpos < n)
    s_loc = jnp.einsum('bgrnqd,bgnkd->bgrnqk', qb, kw).astype(F32) * scale
    s_loc = jnp.where(valid, s_loc, NEG_INF)
    s_ctx = jnp.einsum('bgrnqd,bgkd->bgrnqk', qb, kc).astype(F32) * scale
    s_sink = jnp.broadcast_to(sink.astype(F32).reshape(g, r)[None, :, :, None, None, None],
                              s_loc.shape[:-1] + (1,))
    p = jax.nn.softmax(jnp.concatenate([s_loc, s_ctx, s_sink], axis=-1), axis=-1)
    w_loc = p[..., :3 * BLOCK].astype(v.dtype)
    w_ctx = p[..., 3 * BLOCK:3 * BLOCK + kc.shape[2]].astype(v.dtype)
    o = (jnp.einsum('bgrnqk,bgnkd->bgrnqd', w_loc, vw)
         + jnp.einsum('bgrnqk,bgkd->bgrnqd', w_ctx, vc))
    return o.reshape(b, h, n, d)


def _neighborhood_attn(q, k, v, kc, vc, rpb):
    b, h, n, d = q.shape
    rows = n // GRID_W
    kh = min(NA_ROWS, rows)
    scale = d ** -0.5
    r = jnp.arange(rows)
    rs = jnp.clip(r - kh // 2, 0, rows - kh)
    row_idx = rs[:, None] + jnp.arange(kh)[None, :]
    col = jnp.arange(GRID_W)
    cs = jnp.clip(col - NA_COLS // 2, 0, GRID_W - NA_COLS)
    col_valid = (col[None, :] >= cs[:, None]) & (col[None, :] < cs[:, None] + NA_COLS)
    rel_r = row_idx - r[:, None] + (NA_ROWS - 1)
    rel_c = jnp.clip(col[None, :] - col[:, None], -(NA_COLS - 1), NA_COLS - 1) + (NA_COLS - 1)
    bias = rpb[:, rel_r[:, None, :, None], rel_c[None, :, None, :]].astype(F32)
    qg = q.reshape(b, h, rows, GRID_W, d)
    k_rows = k.reshape(b, h, rows, GRID_W, d)[:, :, row_idx]
    v_rows = v.reshape(b, h, rows, GRID_W, d)[:, :, row_idx]
    s_loc = jnp.einsum('bhrqd,bhrikd->bhrqik', qg, k_rows).astype(F32) * scale + bias[None]
    s_loc = jnp.where(col_valid[:, None, :], s_loc, NEG_INF).reshape(b, h, rows, GRID_W, kh * GRID_W)
    s_ctx = jnp.einsum('bhrqd,bhkd->bhrqk', qg, kc).astype(F32) * scale
    p = jax.nn.softmax(jnp.concatenate([s_loc, s_ctx], axis=-1), axis=-1)
    w_loc = p[..., :kh * GRID_W].reshape(b, h, rows, GRID_W, kh, GRID_W).astype(v.dtype)
    w_ctx = p[..., kh * GRID_W:].astype(v.dtype)
    o = (jnp.einsum('bhrqik,bhrikd->bhrqd', w_loc, v_rows)
         + jnp.einsum('bhrqk,bhkd->bhrqd', w_ctx, vc))
    return o.reshape(b, h, n, d)


def _diff_out(o1, o2, lam_l, lam_init, subln_l):
    lq1, lk1, lq2, lk2 = lam_l
    lam = (jnp.exp(jnp.sum(lq1.astype(F32) * lk1.astype(F32)))
           - jnp.exp(jnp.sum(lq2.astype(F32) * lk2.astype(F32))) + lam_init)
    o = o1 - lam.astype(o1.dtype) * o2
    return _rmsnorm(o, subln_l) * (1.0 - lam_init)


def _mla_up(ckv_n, wukv_l):
    kv = _heads(ckv_n @ wukv_l, MLA_HEADS)
    return kv[..., :MLA_NOPE], kv[..., MLA_NOPE:]


def _mla_key(k_nope, kpe):
    return jnp.concatenate([k_nope, jnp.broadcast_to(kpe[:, None], k_nope.shape[:3] + (MLA_ROPE,))], axis=-1)


def _mixer_context(h, w_in_l, w_o_l, sink_l, lam_l, lam_init, subln_l, qn_l, kvn_l, wuq_l, wukv_l):
    qa, ka, va, qb, kb, vb, cq, ckv, kpe, qd, kd, vd = _project(h, w_in_l)
    rep = WIN_HEADS // WIN_KV_HEADS
    ka_h, va_h = _heads(ka, WIN_KV_HEADS), _heads(va, WIN_KV_HEADS)
    oa = _sdpa(_heads(qa, WIN_HEADS), jnp.repeat(ka_h, rep, axis=1), jnp.repeat(va_h, rep, axis=1), sink_l)
    qb_h, kb_h, vb_h = _heads(qb, DIFF_HEADS), _heads(kb, DIFF_HEADS), _heads(vb, DIFF_HEADS)
    ob = _diff_out(_sdpa(qb_h[..., :DIFF_HALF], kb_h[..., :DIFF_HALF], vb_h),
                   _sdpa(qb_h[..., DIFF_HALF:], kb_h[..., DIFF_HALF:], vb_h),
                   lam_l, lam_init, subln_l)
    q_c = _heads(_rmsnorm(cq, qn_l) @ wuq_l, MLA_HEADS)
    ckv_n = _rmsnorm(ckv, kvn_l)
    k_nope, v_c = _mla_up(ckv_n, wukv_l)
    oc = _sdpa(q_c, _mla_key(k_nope, kpe), v_c)
    kd_h, vd_h = _heads(kd, NA_HEADS), _heads(vd, NA_HEADS)
    od = _sdpa(_heads(qd, NA_HEADS), kd_h, vd_h)
    out = _merge(jnp.concatenate([oa, ob, oc, od], axis=1)) @ w_o_l
    return out, (ka_h, va_h, kb_h, vb_h, ckv_n, kpe, kd_h, vd_h)


def _mixer_latent(h, cached, w_in_l, w_o_l, sink_l, lam_l, lam_init, subln_l, qn_l, kvn_l, wuq_l, wukv_l, rpb_l):
    ka_c, va_c, kb_c, vb_c, ckv_c, kpe_c, kd_c, vd_c = cached
    n = h.shape[1]
    pos = jnp.arange(n)
    pr, pc = pos // GRID_W, pos % GRID_W
    qa, ka, va, qb, kb, vb, cq, ckv, kpe, qd, kd, vd = _project(h, w_in_l)
    oa = _window_attn(_rope2d(_heads(qa, WIN_HEADS), pr, pc), _rope2d(_heads(ka, WIN_KV_HEADS), pr, pc),
                      _heads(va, WIN_KV_HEADS), ka_c, va_c, sink_l)
    qb_h, kb_h, vb_h = _heads(qb, DIFF_HEADS), _heads(kb, DIFF_HEADS), _heads(vb, DIFF_HEADS)
    q1 = _rope2d(qb_h[..., :DIFF_HALF], pr, pc)
    q2 = _rope2d(qb_h[..., DIFF_HALF:], pr, pc)
    k1 = jnp.concatenate([_rope2d(kb_h[..., :DIFF_HALF], pr, pc), kb_c[..., :DIFF_HALF]], axis=2)
    k2 = jnp.concatenate([_rope2d(kb_h[..., DIFF_HALF:], pr, pc), kb_c[..., DIFF_HALF:]], axis=2)
    v_all = jnp.concatenate([vb_h, vb_c], axis=2)
    ob = _diff_out(_sdpa(q1, k1, v_all), _sdpa(q2, k2, v_all), lam_l, lam_init, subln_l)
    q_c = _heads(_rmsnorm(cq, qn_l) @ wuq_l, MLA_HEADS)
    q_c = jnp.concatenate([q_c[..., :MLA_NOPE], _rope2d(q_c[..., MLA_NOPE:], pr, pc)], axis=-1)
    k_nope, v_lat = _mla_up(_rmsnorm(ckv, kvn_l), wukv_l)
    k_nope_c, v_ctx = _mla_up(ckv_c, wukv_l)
    k_all = jnp.concatenate([_mla_key(k_nope, _rope2d(kpe, pr, pc)), _mla_key(k_nope_c, kpe_c)], axis=2)
    oc = _sdpa(q_c, k_all, jnp.concatenate([v_lat, v_ctx], axis=2))
    od = _neighborhood_attn(_heads(qd, NA_HEADS), _heads(kd, NA_HEADS), _heads(vd, NA_HEADS), kd_c, vd_c, rpb_l)
    return _merge(jnp.concatenate([oa, ob, oc, od], axis=1)) @ w_o_l


def _swiglu(t, wg, wu, wd):
    return (jax.nn.silu(t @ wg) * (t @ wu)) @ wd


def _moe(t, router, wg, wu, wd):
    logits = (t @ router).astype(F32)
    top_v, top_i = lax.top_k(logits, TOP_K)
    gates = jax.nn.softmax(top_v, axis=-1)
    combine = jnp.sum(jax.nn.one_hot(top_i, N_EXPERTS, dtype=F32) * gates[..., None], axis=1)
    out = jnp.zeros_like(t)
    for e in range(N_EXPERTS):
        out = out + combine[:, e:e + 1].astype(t.dtype) * _swiglu(t, wg[e], wu[e], wd[e])
    return out


def _channel_mixer(h, l, ffn_w_gate, ffn_w_up, ffn_w_down, moe_router, moe_w_gate, moe_w_up, moe_w_down):
    i = l // 2
    if l % 2 == 0:
        return _swiglu(h, ffn_w_gate[i], ffn_w_up[i], ffn_w_down[i])
    b, n, d = h.shape
    return _moe(h.reshape(b * n, d), moe_router[i], moe_w_gate[i], moe_w_up[i], moe_w_down[i]).reshape(b, n, d)


def _modulation(silu_cond, w_ada_l, b_ada_l):
    m = (silu_cond @ w_ada_l + b_ada_l)[:, None, :]
    return jnp.split(m, 6, axis=-1)


def setup_inputs(seed: int = 0) -> dict:
    key = jax.random.key(seed)
    ks = iter(jax.random.split(key, 48))
    n_dense = (DEPTH + 1) // 2
    n_moe = DEPTH // 2

    def nrm(shape, scale):
        return jax.random.normal(next(ks), shape, F32) * scale

    def gain(shape):
        return 1.0 + nrm(shape, 0.1)

    return {
        'x_prompt': nrm((BATCH, SEQ, D_MODEL), 1.0),
        'x_sample': nrm((DEC_BATCH, DEC_SEQ, D_MODEL), 1.0),
        'cache_win_k': nrm((DEC_BATCH, DEPTH, WIN_KV_HEADS, PAST_LEN, HEAD_DIM), 1.0),
        'cache_win_v': nrm((DEC_BATCH, DEPTH, WIN_KV_HEADS, PAST_LEN, HEAD_DIM), 1.0),
        'cache_diff_k': nrm((DEC_BATCH, DEPTH, DIFF_HEADS, PAST_LEN, HEAD_DIM), 1.0),
        'cache_diff_v': nrm((DEC_BATCH, DEPTH, DIFF_HEADS, PAST_LEN, HEAD_DIM), 1.0),
        'cache_mla_ckv': nrm((DEC_BATCH, DEPTH, PAST_LEN, MLA_KV_RANK), 1.0),
        'cache_mla_kpe': nrm((DEC_BATCH, DEPTH, PAST_LEN, MLA_ROPE), 1.0),
        'cache_na_k': nrm((DEC_BATCH, DEPTH, NA_HEADS, PAST_LEN, HEAD_DIM), 1.0),
        'cache_na_v': nrm((DEC_BATCH, DEPTH, NA_HEADS, PAST_LEN, HEAD_DIM), 1.0),
        'c': nrm((DEC_BATCH, D_MODEL), 1.0),
        'c_ctx': nrm((D_MODEL,), 1.0),
        'w_ada': nrm((DEPTH, D_MODEL, 6 * D_MODEL), D_MODEL ** -0.5),
        'b_ada': nrm((DEPTH, 6 * D_MODEL), 0.02),
        'norm_mix_pre': gain((DEPTH, D_MODEL)),
        'norm_mix_post': gain((DEPTH, D_MODEL)),
        'norm_ffn_pre': gain((DEPTH, D_MODEL)),
        'norm_ffn_post': gain((DEPTH, D_MODEL)),
        'w_in': nrm((DEPTH, D_MODEL, IN_WIDTH), D_MODEL ** -0.5),
        'w_o': nrm((DEPTH, MIX_WIDTH, D_MODEL), MIX_WIDTH ** -0.5),
        'win_sink': nrm((DEPTH, WIN_HEADS), 1.0),
        'diff_lq1': nrm((DEPTH, DIFF_HALF), 0.1),
        'diff_lk1': nrm((DEPTH, DIFF_HALF), 0.1),
        'diff_lq2': nrm((DEPTH, DIFF_HALF), 0.1),
        'diff_lk2': nrm((DEPTH, DIFF_HALF), 0.1),
        'diff_subln': gain((DEPTH, HEAD_DIM)),
        'mla_q_norm': gain((DEPTH, MLA_Q_RANK)),
        'mla_kv_norm': gain((DEPTH, MLA_KV_RANK)),
        'mla_wuq': nrm((DEPTH, MLA_Q_RANK, MLA_HEADS * (MLA_NOPE + MLA_ROPE)), MLA_Q_RANK ** -0.5),
        'mla_wukv': nrm((DEPTH, MLA_KV_RANK, MLA_HEADS * (MLA_NOPE + MLA_VDIM)), MLA_KV_RANK ** -0.5),
        'na_rpb': nrm((DEPTH, NA_HEADS, 2 * NA_ROWS - 1, 2 * NA_COLS - 1), 0.1),
        'ffn_w_gate': nrm((n_dense, D_MODEL, D_FF), D_MODEL ** -0.5),
        'ffn_w_up': nrm((n_dense, D_MODEL, D_FF), D_MODEL ** -0.5),
        'ffn_w_down': nrm((n_dense, D_FF, D_MODEL), D_FF ** -0.5),
        'moe_router': nrm((n_moe, D_MODEL, N_EXPERTS), D_MODEL ** -0.5),
        'moe_w_gate': nrm((n_moe, N_EXPERTS, D_MODEL, D_FF_EXPERT), D_MODEL ** -0.5),
        'moe_w_up': nrm((n_moe, N_EXPERTS, D_MODEL, D_FF_EXPERT), D_MODEL ** -0.5),
        'moe_w_down': nrm((n_moe, N_EXPERTS, D_FF_EXPERT, D_MODEL), D_FF_EXPERT ** -0.5),
    }


def reference(x_prompt, x_sample, cache_win_k, cache_win_v, cache_diff_k, cache_diff_v,
              cache_mla_ckv, cache_mla_kpe, cache_na_k, cache_na_v, c, c_ctx,
              w_ada, b_ada, norm_mix_pre, norm_mix_post, norm_ffn_pre, norm_ffn_post,
              w_in, w_o, win_sink, diff_lq1, diff_lk1, diff_lq2, diff_lk2, diff_subln,
              mla_q_norm, mla_kv_norm, mla_wuq, mla_wukv, na_rpb,
              ffn_w_gate, ffn_w_up, ffn_w_down, moe_router, moe_w_gate, moe_w_up, moe_w_down):
    silu_ctx = jax.nn.silu(c_ctx)[None]
    silu_lat = jax.nn.silu(c)
    xp, xs = x_prompt, x_sample
    ctx_new = [[] for _ in range(8)]
    for l in range(DEPTH):
        lam_init = 0.8 - 0.6 * math.exp(-0.3 * l)
        lam_l = (diff_lq1[l], diff_lk1[l], diff_lq2[l], diff_lk2[l])
        mix_w = (w_in[l], w_o[l], win_sink[l], lam_l, lam_init, diff_subln[l],
                 mla_q_norm[l], mla_kv_norm[l], mla_wuq[l], mla_wukv[l])
        ffn_w = (ffn_w_gate, ffn_w_up, ffn_w_down, moe_router, moe_w_gate, moe_w_up, moe_w_down)

        sh1, sc1, gt1, sh2, sc2, gt2 = _modulation(silu_ctx, w_ada[l], b_ada[l])
        h = _rmsnorm(xp, norm_mix_pre[l]) * (1.0 + sc1) + sh1
        y, ctx_l = _mixer_context(h, *mix_w)
        xp = xp + gt1 * _rmsnorm(y, norm_mix_post[l])
        h = _rmsnorm(xp, norm_ffn_pre[l]) * (1.0 + sc2) + sh2
        xp = xp + gt2 * _rmsnorm(_channel_mixer(h, l, *ffn_w), norm_ffn_post[l])
        for i, t in enumerate(ctx_l):
            ctx_new[i].append(t)

        cached = (cache_win_k[:, l], cache_win_v[:, l], cache_diff_k[:, l], cache_diff_v[:, l],
                  cache_mla_ckv[:, l], cache_mla_kpe[:, l], cache_na_k[:, l], cache_na_v[:, l])
        sh1, sc1, gt1, sh2, sc2, gt2 = _modulation(silu_lat, w_ada[l], b_ada[l])
        h = _rmsnorm(xs, norm_mix_pre[l]) * (1.0 + sc1) + sh1
        y = _mixer_latent(h, cached, *mix_w, na_rpb[l])
        xs = xs + gt1 * _rmsnorm(y, norm_mix_post[l])
        h = _rmsnorm(xs, norm_ffn_pre[l]) * (1.0 + sc2) + sh2
        xs = xs + gt2 * _rmsnorm(_channel_mixer(h, l, *ffn_w), norm_ffn_post[l])

    new_win_k = jnp.stack(ctx_new[0], axis=1)
    new_win_v = jnp.stack(ctx_new[1], axis=1)
    new_diff_k = jnp.stack(ctx_new[2], axis=1)
    new_diff_v = jnp.stack(ctx_new[3], axis=1)
    new_mla_ckv = jnp.stack(ctx_new[4], axis=1)
    new_mla_kpe = jnp.stack(ctx_new[5], axis=1)
    new_na_k = jnp.stack(ctx_new[6], axis=1)
    new_na_v = jnp.stack(ctx_new[7], axis=1)
    return (xp, xs, new_win_k, new_win_v, new_diff_k, new_diff_v, new_mla_ckv, new_mla_kpe, new_na_k, new_na_v)
```

```python
import functools
import math

import jax
import jax.numpy as jnp
from jax import lax
from jax.experimental import pallas as pl
from jax.experimental.pallas import tpu as pltpu

F32 = jnp.float32
BF16 = jnp.bfloat16

D = 1024
DEPTH = 4
N_CTX_B, L_CTX = 16, 256
N_LAT_B, N_LAT = 4, 1024
T_CTX = N_CTX_B * L_CTX
T_LAT = N_LAT_B * N_LAT
T = T_CTX + T_LAT
HD = 64
GRID_W = 64
WINDOW = 128
EPS = 1e-6
NEG_INF = -1e30
ROPE_THETA = 10000.0
MLA_Q_RANK = 192
D_FF = 2816
N_EXPERTS = 8
D_FF_EXPERT = 3584

PW = 2560
C_CKV, C_KPE, C_CQ, C_D = 1280, 1408, 1536, 1792
PC_W = 1152

TM = 512
ITEM_ROWS = 2048
SUB_ROWS = 256
N_ITEMS = N_EXPERTS + (2 * T) // ITEM_ROWS
FF_CHUNK = 512
FFN_CHUNK = 1408
VMEM_LIMIT = 56 * 1024 * 1024


def _cparams(sem):
    return pltpu.CompilerParams(dimension_semantics=sem, vmem_limit_bytes=VMEM_LIMIT)


def _rms(x, g):
    return x * lax.rsqrt(jnp.mean(x * x, axis=-1, keepdims=True) + EPS) * g


def _silu(a):
    return a / (1.0 + jnp.exp(-a))


def _dot(a, b):
    return jnp.dot(a, b, preferred_element_type=F32)


def _dot_t(a, b):
    return lax.dot_general(a, b, (((1,), (1,)), ((), ())), preferred_element_type=F32)


def _softmax_pv(scores, values, extra_logit=None):
    m = jnp.max(scores[0], axis=-1, keepdims=True)
    for s in scores[1:]:
        m = jnp.maximum(m, jnp.max(s, axis=-1, keepdims=True))
    if extra_logit is not None:
        m = jnp.maximum(m, extra_logit)
    l = None
    o = None
    for s, v in zip(scores, values):
        p = jnp.exp(s - m)
        ls = jnp.sum(p, axis=-1, keepdims=True)
        os_ = _dot(p.astype(BF16), v)
        l = ls if l is None else l + ls
        o = os_ if o is None else o + os_
    if extra_logit is not None:
        l = l + jnp.exp(extra_logit - m)
    return o / l


def _rope(x, c, sa, sb, shift):
    return x * c + pltpu.roll(x, 128 - shift, 1) * sa + pltpu.roll(x, shift, 1) * sb


def _mod_row(i):
    nct = T_CTX // TM
    per_b = N_LAT // TM
    return jnp.where(i >= nct, 1 + (i - nct) // per_b, 0)


def _mods_kernel(cond_ref, w_ref, b_ref, o_ref):
    a = _silu(cond_ref[...]).astype(BF16)
    o_ref[...] = _dot(a, w_ref[...].astype(BF16)) + b_ref[...]


def _mods(cond, w_ada, b_ada):
    tn = 1536
    n = 6 * D
    return pl.pallas_call(
        _mods_kernel,
        grid=(DEPTH, n // tn),
        in_specs=[pl.BlockSpec((8, D), lambda l, j: (0, 0)),
                  pl.BlockSpec((None, D, tn), lambda l, j: (l, 0, j)),
                  pl.BlockSpec((None, 1, tn), lambda l, j: (l, 0, j))],
        out_specs=pl.BlockSpec((None, 8, tn), lambda l, j: (l, 0, j)),
        out_shape=jax.ShapeDtypeStruct((DEPTH, 8, n), F32),
        compiler_params=_cparams(("arbitrary", "arbitrary")),
        name="mods",
    )(cond, w_ada, b_ada.reshape(DEPTH, 1, n))


def _in_proj_kernel(x_ref, mod_ref, g_ref, w_ref, qn_ref, kvn_ref, wuq_ref, wukv_ref,
                    pa_ref, pb_ref, pc_ref, pd_ref):
    m = mod_ref[...]
    h = _rms(x_ref[...], g_ref[...]) * (1.0 + m[1:2]) + m[0:1]
    p = _dot(h.astype(BF16), w_ref[...])
    pa_ref[...] = p[:, 0:512].astype(BF16)
    pb_ref[...] = p[:, 512:1280].astype(BF16)
    pd_ref[...] = p[:, C_D:PW].astype(BF16)
    ckv_n = _rms(p[:, C_CKV:C_CKV + 128], kvn_ref[...])
    z = p[:, C_CQ:C_CQ + 256]
    cqn = z * lax.rsqrt(jnp.sum(z * z, axis=-1, keepdims=True) / MLA_Q_RANK + EPS) * qn_ref[...]
    qc = _dot(cqn.astype(BF16), wuq_ref[...])
    kv = _dot(ckv_n.astype(BF16), wukv_ref[...])
    pc_ref[:, 0:384] = qc.astype(BF16)
    pc_ref[:, 384:512] = p[:, C_KPE:C_KPE + 128].astype(BF16)
    pc_ref[:, 512:1024] = kv.astype(BF16)
    pc_ref[:, 1024:1152] = ckv_n.astype(BF16)


def _in_proj(l, x, mods, g_pre, w_in_p, qn_p, kvn, wuq_p, wukv_p):
    row = lambda i: (i, 0)
    lyr2 = lambda i: (l, 0, 0)
    return pl.pallas_call(
        _in_proj_kernel,
        grid=(T // TM,),
        in_specs=[pl.BlockSpec((TM, D), row),
                  pl.BlockSpec((None, None, 6, D), lambda i: (l, _mod_row(i), 0, 0)),
                  pl.BlockSpec((None, 1, D), lyr2),
                  pl.BlockSpec((None, D, PW), lyr2),
                  pl.BlockSpec((None, 1, 256), lyr2),
                  pl.BlockSpec((None, 1, 128), lyr2),
                  pl.BlockSpec((None, 256, 384), lyr2),
                  pl.BlockSpec((None, 128, 512), lyr2)],
        out_specs=[pl.BlockSpec((TM, 512), row), pl.BlockSpec((TM, 768), row),
                   pl.BlockSpec((TM, PC_W), row), pl.BlockSpec((TM, 768), row)],
        out_shape=[jax.ShapeDtypeStruct((T, 512), BF16), jax.ShapeDtypeStruct((T, 768), BF16),
                   jax.ShapeDtypeStruct((T, PC_W), BF16), jax.ShapeDtypeStruct((T, 768), BF16)],
        compiler_params=_cparams(("arbitrary",)),
        name="in_proj",
    )(x, mods, g_pre, w_in_p, qn_p, kvn, wuq_p, wukv_p)


def _attn_ctx_kernel(sink_ref, lam_ref, pa, pb, pc, pd, subln_ref, o_ref,
                     ka_o, va_o, kb_o, vb_o, ckv_o, kpe_o, kd_o, vd_o):
    scale64 = HD ** -0.5
    for h in range(4):
        g = h // 2
        s = _dot_t(pa[:, 64 * h:64 * h + 64], pa[:, 256 + 64 * g:320 + 64 * g]) * scale64
        o = _softmax_pv([s], [pa[:, 384 + 64 * g:448 + 64 * g]], sink_ref[h])
        o_ref[:, 64 * h:64 * h + 64] = o.astype(BF16)
    for g in range(2):
        ka_o[g] = pa[:, 256 + 64 * g:320 + 64 * g].astype(F32)
        va_o[g] = pa[:, 384 + 64 * g:448 + 64 * g].astype(F32)
    lam = lam_ref[0]
    out_scale = lam_ref[1]
    scale32 = (HD // 2) ** -0.5
    for h in range(4):
        q0, k0, v0 = 64 * h, 256 + 64 * h, 512 + 64 * h
        v = pb[:, v0:v0 + 64]
        s1 = _dot_t(pb[:, q0:q0 + 32], pb[:, k0:k0 + 32]) * scale32
        s2 = _dot_t(pb[:, q0 + 32:q0 + 64], pb[:, k0 + 32:k0 + 64]) * scale32
        o = _softmax_pv([s1], [v]) - lam * _softmax_pv([s2], [v])
        o_ref[:, 256 + 64 * h:320 + 64 * h] = (_rms(o, subln_ref[...]) * out_scale).astype(BF16)
        kb_o[h] = pb[:, k0:k0 + 64].astype(F32)
        vb_o[h] = v.astype(F32)
    scale96 = (HD + HD // 2) ** -0.5
    kpe = pc[:, 384:416]
    for h in range(4):
        s = (_dot_t(pc[:, 64 * h:64 * h + 64], pc[:, 512 + 64 * h:576 + 64 * h])
             + _dot_t(pc[:, 256 + 32 * h:288 + 32 * h], kpe)) * scale96
        o = _softmax_pv([s], [pc[:, 768 + 64 * h:832 + 64 * h]])
        o_ref[:, 512 + 64 * h:576 + 64 * h] = o.astype(BF16)
    ckv_o[...] = pc[:, 1024:1152].astype(F32)
    kpe_o[...] = kpe.astype(F32)
    for h in range(4):
        q0, k0, v0 = 64 * h, 256 + 64 * h, 512 + 64 * h
        s = _dot_t(pd[:, q0:q0 + 64], pd[:, k0:k0 + 64]) * scale64
        o = _softmax_pv([s], [pd[:, v0:v0 + 64]])
        o_ref[:, 768 + 64 * h:832 + 64 * h] = o.astype(BF16)
        kd_o[h] = pd[:, k0:k0 + 64].astype(F32)
        vd_o[h] = pd[:, v0:v0 + 64].astype(F32)


def _attn_ctx(sink, lam, pa, pb, pc, pd, subln):
    row = lambda b: (b, 0)
    smem = pl.BlockSpec(memory_space=pltpu.SMEM)
    hd4 = lambda n: pl.BlockSpec((None, n, L_CTX, HD), lambda b: (b, 0, 0, 0))
    hshape = lambda n: jax.ShapeDtypeStruct((N_CTX_B, n, L_CTX, HD), F32)
    return pl.pallas_call(
        _attn_ctx_kernel,
        grid=(N_CTX_B,),
        in_specs=[smem, smem,
                  pl.BlockSpec((L_CTX, 512), row), pl.BlockSpec((L_CTX, 768), row),
                  pl.BlockSpec((L_CTX, PC_W), row), pl.BlockSpec((L_CTX, 768), row),
                  pl.BlockSpec((1, HD), lambda b: (0, 0))],
        out_specs=[pl.BlockSpec((L_CTX, D), row),
                   hd4(2), hd4(2), hd4(4), hd4(4),
                   pl.BlockSpec((None, L_CTX, 128), lambda b: (b, 0, 0)),
                   pl.BlockSpec((None, L_CTX, 32), lambda b: (b, 0, 0)),
                   hd4(4), hd4(4)],
        out_shape=[jax.ShapeDtypeStruct((T_CTX, D), BF16),
                   hshape(2), hshape(2), hshape(4), hshape(4),
                   jax.ShapeDtypeStruct((N_CTX_B, L_CTX, 128), F32),
                   jax.ShapeDtypeStruct((N_CTX_B, L_CTX, 32), F32),
                   hshape(4), hshape(4)],
        compiler_params=_cparams(("arbitrary",)),
        name="attn_ctx",
    )(sink, lam, pa, pb, pc, pd, subln)


_LAT_ROW = lambda b: (T_CTX // N_LAT + b, 0)


def _cache_spec(l, n):
    return pl.BlockSpec((None, None, n, L_CTX, HD), lambda b: (b, l, 0, 0, 0))


def _tab_spec():
    return pl.BlockSpec((N_LAT, 128), lambda b: (0, 0))


def _lat_win_kernel(sink_ref, pa, ck_ref, cv_ref, c_ref, sa_ref, sb_ref, o_ref, qs, kpad, vpad):
    c, sa, sb = c_ref[...], sa_ref[...], sb_ref[...]
    scale64 = HD ** -0.5
    for j in range(2):
        q = _rope(pa[:, 128 * j:128 * j + 128].astype(F32), c, sa, sb, 16) * scale64
        qs[:, 128 * j:128 * j + 128] = q.astype(BF16)
    kr = _rope(pa[:, 256:384].astype(F32), c, sa, sb, 16).astype(BF16)
    zeros = jnp.zeros((WINDOW, HD), BF16)
    for g in range(2):
        kpad[g, 0:WINDOW, :] = zeros
        kpad[g, WINDOW:WINDOW + N_LAT, :] = kr[:, 64 * g:64 * g + 64]
        kpad[g, WINDOW + N_LAT:, :] = zeros
        vpad[g, 0:WINDOW, :] = zeros
        vpad[g, WINDOW:WINDOW + N_LAT, :] = pa[:, 384 + 64 * g:448 + 64 * g]
        vpad[g, WINDOW + N_LAT:, :] = zeros
    qi = lax.broadcasted_iota(jnp.int32, (WINDOW, 3 * WINDOW), 0)
    kj = lax.broadcasted_iota(jnp.int32, (WINDOW, 3 * WINDOW), 1)
    band = jnp.abs(kj - WINDOW - qi) <= WINDOW
    for h in range(4):
        g = h // 2
        kc = ck_ref[g].astype(BF16)
        vc = cv_ref[g].astype(BF16)
        sink = sink_ref[h]

        def body(nb, carry, h=h, g=g, kc=kc, vc=vc, sink=sink):
            r0 = pl.multiple_of(nb * WINDOW, WINDOW)
            q = qs[pl.ds(r0, WINDOW), 64 * h:64 * h + 64]
            kw = kpad[g, pl.ds(r0, 3 * WINDOW), :]
            vw = vpad[g, pl.ds(r0, 3 * WINDOW), :]
            kpos = r0 - WINDOW + kj
            valid = band & (kpos >= 0) & (kpos < N_LAT)
            s_loc = jnp.where(valid, _dot_t(q, kw), NEG_INF)
            s_ctx = _dot_t(q, kc)
            o = _softmax_pv([s_loc, s_ctx], [vw, vc], sink)
            o_ref[pl.ds(r0, WINDOW), 64 * h:64 * h + 64] = o.astype(BF16)
            return carry

        lax.fori_loop(0, N_LAT // WINDOW, body, 0)


def _lat_win(l, sink, pa, ck, cv, tabs):
    return pl.pallas_call(
        _lat_win_kernel,
        grid=(N_LAT_B,),
        in_specs=[pl.BlockSpec(memory_space=pltpu.SMEM),
                  pl.BlockSpec((N_LAT, 512), _LAT_ROW),
                  _cache_spec(l, 2), _cache_spec(l, 2),
                  _tab_spec(), _tab_spec(), _tab_spec()],
        out_specs=pl.BlockSpec((N_LAT, 256), lambda b: (b, 0)),
        out_shape=jax.ShapeDtypeStruct((T_LAT, 256), BF16),
        scratch_shapes=[pltpu.VMEM((N_LAT, 256), BF16),
                        pltpu.VMEM((2, N_LAT + 2 * WINDOW, HD), BF16),
                        pltpu.VMEM((2, N_LAT + 2 * WINDOW, HD), BF16)],
        compiler_params=_cparams(("arbitrary",)),
        name="lat_win",
    )(sink, pa, ck, cv, *tabs)


QT = 256
N_KEYS = N_LAT + L_CTX


def _lat_diff_kernel(lam_ref, pb, ck_ref, cv_ref, subln_ref, c_ref, sa_ref, sb_ref, o_ref, qs, kall, vall):
    c, sa, sb = c_ref[...], sa_ref[...], sb_ref[...]
    for j in range(2):
        qs[:, 128 * j:128 * j + 128] = _rope(pb[:, 128 * j:128 * j + 128].astype(F32), c, sa, sb, 8).astype(BF16)
        kr = _rope(pb[:, 256 + 128 * j:384 + 128 * j].astype(F32), c, sa, sb, 8).astype(BF16)
        for r in range(2):
            kall[2 * j + r, 0:N_LAT, :] = kr[:, 64 * r:64 * r + 64]
    for h in range(4):
        kall[h, N_LAT:, :] = ck_ref[h].astype(BF16)
        vall[h, 0:N_LAT, :] = pb[:, 512 + 64 * h:576 + 64 * h]
        vall[h, N_LAT:, :] = cv_ref[h].astype(BF16)
    lam = lam_ref[0]
    out_scale = lam_ref[1]
    scale32 = (HD // 2) ** -0.5
    for h in range(4):
        def body(t, carry, h=h):
            r0 = pl.multiple_of(t * QT, QT)
            v = vall[h]
            s1 = _dot_t(qs[pl.ds(r0, QT), 64 * h:64 * h + 32], kall[h, :, 0:32]) * scale32
            s2 = _dot_t(qs[pl.ds(r0, QT), 64 * h + 32:64 * h + 64], kall[h, :, 32:64]) * scale32
            o = _softmax_pv([s1], [v]) - lam * _softmax_pv([s2], [v])
            o_ref[pl.ds(r0, QT), 64 * h:64 * h + 64] = (_rms(o, subln_ref[...]) * out_scale).astype(BF16)
            return carry

        lax.fori_loop(0, N_LAT // QT, body, 0)


def _lat_diff(l, lam, pb, ck, cv, subln, tabs):
    return pl.pallas_call(
        _lat_diff_kernel,
        grid=(N_LAT_B,),
        in_specs=[pl.BlockSpec(memory_space=pltpu.SMEM),
                  pl.BlockSpec((N_LAT, 768), _LAT_ROW),
                  _cache_spec(l, 4), _cache_spec(l, 4),
                  pl.BlockSpec((1, HD), lambda b: (0, 0)),
                  _tab_spec(), _tab_spec(), _tab_spec()],
        out_specs=pl.BlockSpec((N_LAT, 256), lambda b: (b, 0)),
        out_shape=jax.ShapeDtypeStruct((T_LAT, 256), BF16),
        scratch_shapes=[pltpu.VMEM((N_LAT, 256), BF16),
                        pltpu.VMEM((4, N_KEYS, HD), BF16),
                        pltpu.VMEM((4, N_KEYS, HD), BF16)],
        compiler_params=_cparams(("arbitrary",)),
        name="lat_diff",
    )(lam, pb, ck, cv, subln, *tabs)


def _lat_mla_kernel(pc, ckv_ref, kpe_ref, wukv_ref, c_ref, sa_ref, sb_ref, o_ref, qcat, kcat, vall):
    c, sa, sb = c_ref[...], sa_ref[...], sb_ref[...]
    qr = _rope(pc[:, 256:384].astype(F32), c, sa, sb, 8).astype(BF16)
    kpe_l = _rope(pc[:, 384:512].astype(F32), c, sa, sb, 8).astype(BF16)
    kvc = _dot(ckv_ref[...].astype(BF16), wukv_ref[...]).astype(BF16)
    kpe_c = kpe_ref[...].astype(BF16)
    for h in range(4):
        qcat[h, :, 0:64] = pc[:, 64 * h:64 * h + 64]
        qcat[h, :, 64:96] = qr[:, 32 * h:32 * h + 32]
        kcat[h, 0:N_LAT, 0:64] = pc[:, 512 + 64 * h:576 + 64 * h]
        kcat[h, 0:N_LAT, 64:96] = kpe_l[:, 0:32]
        kcat[h, N_LAT:, 0:64] = kvc[:, 64 * h:64 * h + 64]
        kcat[h, N_LAT:, 64:96] = kpe_c
        vall[h, 0:N_LAT, :] = pc[:, 768 + 64 * h:832 + 64 * h]
        vall[h, N_LAT:, :] = kvc[:, 256 + 64 * h:320 + 64 * h]
    scale96 = (HD + HD // 2) ** -0.5
    for h in range(4):
        def body(t, carry, h=h):
            r0 = pl.multiple_of(t * QT, QT)
            s = _dot_t(qcat[h, pl.ds(r0, QT), :], kcat[h]) * scale96
            o = _softmax_pv([s], [vall[h]])
            o_ref[pl.ds(r0, QT), 64 * h:64 * h + 64] = o.astype(BF16)
            return carry

        lax.fori_loop(0, N_LAT // QT, body, 0)


def _lat_mla(l, pc, ckv, kpe, wukv_p, tabs):
    return pl.pallas_call(
        _lat_mla_kernel,
        grid=(N_LAT_B,),
        in_specs=[pl.BlockSpec((N_LAT, PC_W), _LAT_ROW),
                  pl.BlockSpec((None, None, L_CTX, 128), lambda b: (b, l, 0, 0)),
                  pl.BlockSpec((None, None, L_CTX, 32), lambda b: (b, l, 0, 0)),
                  pl.BlockSpec((None, 128, 512), lambda b: (l, 0, 0)),
                  _tab_spec(), _tab_spec(), _tab_spec()],
        out_specs=pl.BlockSpec((N_LAT, 256), lambda b: (b, 0)),
        out_shape=jax.ShapeDtypeStruct((T_LAT, 256), BF16),
        scratch_shapes=[pltpu.VMEM((4, N_LAT, 96), BF16),
                        pltpu.VMEM((4, N_KEYS, 96), BF16),
                        pltpu.VMEM((4, N_KEYS, HD), BF16)],
        compiler_params=_cparams(("arbitrary",)),
        name="lat_mla",
    )(pc, ckv, kpe, wukv_p, *tabs)


NA_ROWS = 8
NA_KEYS = NA_ROWS * GRID_W


def _lat_na_kernel(pd, ck_ref, cv_ref, bias_ref, o_ref):
    scale64 = HD ** -0.5
    n_rows = N_LAT // GRID_W
    for h in range(4):
        kc = ck_ref[h].astype(BF16)
        vc = cv_ref[h].astype(BF16)

        def body(r, carry, h=h, kc=kc, vc=vc):
            rs = jnp.clip(r - NA_ROWS // 2, 0, n_rows - NA_ROWS)
            q0 = pl.multiple_of(r * GRID_W, GRID_W)
            k0 = pl.multiple_of(rs * GRID_W, GRID_W)
            q = (pd[pl.ds(q0, GRID_W), 64 * h:64 * h + 64].astype(F32) * scale64).astype(BF16)
            kw = pd[pl.ds(k0, NA_KEYS), 256 + 64 * h:320 + 64 * h]
            vw = pd[pl.ds(k0, NA_KEYS), 512 + 64 * h:576 + 64 * h]
            s_loc = _dot_t(q, kw) + bias_ref[h, r - rs]
            s_ctx = _dot_t(q, kc)
            o = _softmax_pv([s_loc, s_ctx], [vw, vc])
            o_ref[pl.ds(q0, GRID_W), 64 * h:64 * h + 64] = o.astype(BF16)
            return carry

        lax.fori_loop(0, n_rows, body, 0)


def _lat_na(l, pd, ck, cv, bias):
    return pl.pallas_call(
        _lat_na_kernel,
        grid=(N_LAT_B,),
        in_specs=[pl.BlockSpec((N_LAT, 768), _LAT_ROW),
                  _cache_spec(l, 4), _cache_spec(l, 4),
                  pl.BlockSpec((None, 4, NA_ROWS, GRID_W, NA_KEYS), lambda b: (l, 0, 0, 0, 0))],
        out_specs=pl.BlockSpec((N_LAT, 256), lambda b: (b, 0)),
        out_shape=jax.ShapeDtypeStruct((T_LAT, 256), BF16),
        compiler_params=_cparams(("arbitrary",)),
        name="lat_na",
    )(pd, ck, cv, bias)


def _out_proj_kernel(with_router, *refs):
    if with_router:
        (oc_ref, oa_ref, ob_ref, om_ref, od_ref, x_ref, mod_ref, wo_ref, gpost_ref, gffn_ref, r_ref,
         x1_ref, h2_ref, route_ref) = refs
    else:
        (oc_ref, oa_ref, ob_ref, om_ref, od_ref, x_ref, mod_ref, wo_ref, gpost_ref, gffn_ref,
         x1_ref, h2_ref) = refs
    i = pl.program_id(0)
    nct = T_CTX // TM

    def finish(y):
        m = mod_ref[...]
        x1 = x_ref[...] + m[2:3] * _rms(y, gpost_ref[...])
        x1_ref[...] = x1
        h2 = (_rms(x1, gffn_ref[...]) * (1.0 + m[4:5]) + m[3:4]).astype(BF16)
        h2_ref[...] = h2
        if with_router:
            lg = _dot(h2, r_ref[...])
            lane = lax.broadcasted_iota(jnp.int32, lg.shape, 1)
            lg = jnp.where(lane < N_EXPERTS, lg, -jnp.inf)
            m1 = jnp.max(lg, axis=-1, keepdims=True)
            lane_f = lane.astype(F32)
            i1 = jnp.min(jnp.where(lg == m1, lane_f, 128.0), axis=-1, keepdims=True)
            lg2 = jnp.where(lane_f == i1, -jnp.inf, lg)
            m2 = jnp.max(lg2, axis=-1, keepdims=True)
            i2 = jnp.min(jnp.where(lg2 == m2, lane_f, 128.0), axis=-1, keepdims=True)
            e = jnp.exp(m2 - m1)
            g1 = 1.0 / (1.0 + e)
            g2 = e / (1.0 + e)
            route_ref[...] = jnp.where(lane == 0, i1,
                                       jnp.where(lane == 1, i2,
                                                 jnp.where(lane == 2, g1, jnp.where(lane == 3, g2, 0.0))))

    @pl.when(i < nct)
    def _():
        finish(_dot(oc_ref[...], wo_ref[...]))

    @pl.when(i >= nct)
    def _():
        y = _dot(oa_ref[...], wo_ref[0:256, :])
        y = y + _dot(ob_ref[...], wo_ref[256:512, :])
        y = y + _dot(om_ref[...], wo_ref[512:768, :])
        y = y + _dot(od_ref[...], wo_ref[768:1024, :])
        finish(y)


def _out_proj(l, o_ctx, o_lat, x, mods, w_o, g_post, g_ffn, router_p):
    nct = T_CTX // TM
    row = lambda i: (i, 0)
    lyr2 = lambda i: (l, 0, 0)
    lat = lambda i: (jnp.maximum(i - nct, 0), 0)
    with_router = router_p is not None
    in_specs = [pl.BlockSpec((TM, D), lambda i: (jnp.minimum(i, nct - 1), 0)),
                pl.BlockSpec((TM, 256), lat), pl.BlockSpec((TM, 256), lat),
                pl.BlockSpec((TM, 256), lat), pl.BlockSpec((TM, 256), lat),
                pl.BlockSpec((TM, D), row),
                pl.BlockSpec((None, None, 6, D), lambda i: (l, _mod_row(i), 0, 0)),
                pl.BlockSpec((None, D, D), lyr2),
                pl.BlockSpec((None, 1, D), lyr2),
                pl.BlockSpec((None, 1, D), lyr2)]
    out_specs = [pl.BlockSpec((TM, D), row), pl.BlockSpec((TM, D), row)]
    out_shape = [jax.ShapeDtypeStruct((T, D), F32), jax.ShapeDtypeStruct((T, D), BF16)]
    args = [o_ctx, *o_lat, x, mods, w_o, g_post, g_ffn]
    if with_router:
        in_specs.append(pl.BlockSpec((D, 128), lambda i: (0, 0)))
        out_specs.append(pl.BlockSpec((TM, 128), row))
        out_shape.append(jax.ShapeDtypeStruct((T, 128), F32))
        args.append(router_p)
    return pl.pallas_call(
        functools.partial(_out_proj_kernel, with_router),
        grid=(T // TM,),
        in_specs=in_specs, out_specs=out_specs, out_shape=out_shape,
        compiler_params=_cparams(("arbitrary",)),
        name="out_proj_router" if with_router else "out_proj",
    )(*args)


FFN_TM = 1024


def _ffn_kernel(h_ref, x_ref, mod_ref, g_ref, wg_ref, wu_ref, wd_ref, o_ref, acc_ref):
    c = pl.program_id(1)
    h = h_ref[...]
    t = (_silu(_dot(h, wg_ref[...])) * _dot(h, wu_ref[...])).astype(BF16)
    part = _dot(t, wd_ref[...])

    @pl.when(c == 0)
    def _():
        acc_ref[...] = part

    @pl.when(c > 0)
    def _():
        acc_ref[...] += part

    @pl.when(c == pl.num_programs(1) - 1)
    def _():
        m = mod_ref[...]
        o_ref[...] = x_ref[...] + m[5:6] * _rms(acc_ref[...], g_ref[...])


def _ffn_mod_row(i):
    nct = T_CTX // FFN_TM
    return jnp.where(i >= nct, 1 + (i - nct) // (N_LAT // FFN_TM), 0)


def _ffn(l, i_dense, h2, x1, mods, g_post, wg, wu, wd):
    row = lambda i, c: (i, 0)
    return pl.pallas_call(
        _ffn_kernel,
        grid=(T // FFN_TM, D_FF // FFN_CHUNK),
        in_specs=[pl.BlockSpec((FFN_TM, D), row),
                  pl.BlockSpec((FFN_TM, D), row),
                  pl.BlockSpec((None, None, 6, D), lambda i, c: (l, _ffn_mod_row(i), 0, 0)),
                  pl.BlockSpec((None, 1, D), lambda i, c: (l, 0, 0)),
                  pl.BlockSpec((None, D, FFN_CHUNK), lambda i, c: (i_dense, 0, c)),
                  pl.BlockSpec((None, D, FFN_CHUNK), lambda i, c: (i_dense, 0, c)),
                  pl.BlockSpec((None, FFN_CHUNK, D), lambda i, c: (i_dense, c, 0))],
        out_specs=pl.BlockSpec((FFN_TM, D), row),
        out_shape=jax.ShapeDtypeStruct((T, D), F32),
        scratch_shapes=[pltpu.VMEM((FFN_TM, D), F32)],
        compiler_params=_cparams(("arbitrary", "arbitrary")),
        name="ffn",
    )(h2, x1, mods, g_post, wg, wu, wd)


def _moe_kernel(ie_ref, ix_ref, ns_ref, x_ref, wg_ref, wu_ref, wd_ref, o_ref, wgb, wub, wdb):
    i = pl.program_id(0)
    c = pl.program_id(1)
    ns = ns_ref[i]

    @pl.when(ns > 0)
    def _():
        @pl.when(c == 0)
        def _():
            o_ref[...] = jnp.zeros(o_ref.shape, F32)

        wgb[...] = wg_ref[...].astype(BF16)
        wub[...] = wu_ref[...].astype(BF16)
        wdb[...] = wd_ref[...].astype(BF16)

        def body(s, carry):
            r0 = pl.multiple_of(s * SUB_ROWS, SUB_ROWS)
            xs = x_ref[pl.ds(r0, SUB_ROWS), :]
            t = (_silu(_dot(xs, wgb[...])) * _dot(xs, wub[...])).astype(BF16)
            o_ref[pl.ds(r0, SUB_ROWS), :] += _dot(t, wdb[...])
            return carry

        lax.fori_loop(0, ns, body, 0)


def _moe(i_moe, item_e, item_x, item_ns, x_sorted, wg, wu, wd):
    nc = D_FF_EXPERT // FF_CHUNK
    chunk = lambda i, c, ie, ix, ns: jnp.where(ns[i] > 0, c, nc - 1)
    grid_spec = pltpu.PrefetchScalarGridSpec(
        num_scalar_prefetch=3,
        grid=(N_ITEMS, nc),
        in_specs=[pl.BlockSpec((ITEM_ROWS, D), lambda i, c, ie, ix, ns: (ix[i], 0)),
                  pl.BlockSpec((None, None, D, FF_CHUNK),
                               lambda i, c, ie, ix, ns: (i_moe, ie[i], 0, chunk(i, c, ie, ix, ns))),
                  pl.BlockSpec((None, None, D, FF_CHUNK),
                               lambda i, c, ie, ix, ns: (i_moe, ie[i], 0, chunk(i, c, ie, ix, ns))),
                  pl.BlockSpec((None, None, FF_CHUNK, D),
                               lambda i, c, ie, ix, ns: (i_moe, ie[i], chunk(i, c, ie, ix, ns), 0))],
        out_specs=pl.BlockSpec((ITEM_ROWS, D), lambda i, c, ie, ix, ns: (ix[i], 0)),
        scratch_shapes=[pltpu.VMEM((D, FF_CHUNK), BF16), pltpu.VMEM((D, FF_CHUNK), BF16),
                        pltpu.VMEM((FF_CHUNK, D), BF16)],
    )
    return pl.pallas_call(
        _moe_kernel,
        grid_spec=grid_spec,
        out_shape=jax.ShapeDtypeStruct((N_ITEMS * ITEM_ROWS, D), F32),
        compiler_params=_cparams(("arbitrary", "arbitrary")),
        name="moe_experts",
    )(item_e, item_x, item_ns, x_sorted, wg, wu, wd)


def _post_kernel(y_ref, x_ref, mod_ref, g_ref, o_ref):
    m = mod_ref[...]
    o_ref[...] = x_ref[...] + m[5:6] * _rms(y_ref[...], g_ref[...])


def _post(l, y, x1, mods, g_post):
    row = lambda i: (i, 0)
    return pl.pallas_call(
        _post_kernel,
        grid=(T // TM,),
        in_specs=[pl.BlockSpec((TM, D), row), pl.BlockSpec((TM, D), row),
                  pl.BlockSpec((None, None, 6, D), lambda i: (l, _mod_row(i), 0, 0)),
                  pl.BlockSpec((None, 1, D), lambda i: (l, 0, 0))],
        out_specs=pl.BlockSpec((TM, D), row),
        out_shape=jax.ShapeDtypeStruct((T, D), F32),
        compiler_params=_cparams(("arbitrary",)),
        name="moe_post",
    )(y, x1, mods, g_post)


def _route_plan(route):
    i32 = jnp.int32
    e_pairs = route[:, 0:2].astype(i32).reshape(-1)
    onehot = (e_pairs[:, None] == jnp.arange(N_EXPERTS, dtype=i32)[None, :]).astype(i32)
    csum = jnp.cumsum(onehot, axis=0)
    rank = jnp.take_along_axis(csum, e_pairs[:, None], axis=1)[:, 0] - 1
    counts = csum[-1]
    n_items = (counts + ITEM_ROWS - 1) // ITEM_ROWS
    ends = jnp.cumsum(n_items)
    base = ends - n_items
    slot = (base[e_pairs] + rank // ITEM_ROWS) * ITEM_ROWS + rank % ITEM_ROWS
    total = ends[-1]
    ii = jnp.arange(N_ITEMS, dtype=i32)
    last = total - 1
    ii_c = jnp.minimum(ii, last)
    item_e = jnp.sum((ends[None, :] <= ii_c[:, None]).astype(i32), axis=1)
    rows = jnp.clip(counts[item_e] - (ii_c - base[item_e]) * ITEM_ROWS, 0, ITEM_ROWS)
    rows = jnp.where(ii < total, rows, 0)
    item_ns = (rows + SUB_ROWS - 1) // SUB_ROWS
    tok_of_slot = jnp.zeros((N_ITEMS * ITEM_ROWS,), i32).at[slot].set(jnp.arange(2 * T, dtype=i32) // 2)
    return slot.reshape(T, 2), tok_of_slot, item_e.astype(i32), ii_c.astype(i32), item_ns.astype(i32)


def _rope_tables(d):
    half, quarter = d // 2, d // 4
    t = jnp.arange(N_LAT)
    pr, pc = t // GRID_W, t % GRID_W
    i = jnp.arange(128) % d
    j = (i % half) % quarter
    inv = ROPE_THETA ** (-(2 * j).astype(F32) / half)
    pos = jnp.where((i < half)[None, :], pr[:, None], pc[:, None]).astype(F32)
    ang = pos * inv[None, :]
    low = ((i % half) < quarter)[None, :]
    sin = jnp.sin(ang)
    return jnp.cos(ang), jnp.where(low, -sin, 0.0), jnp.where(low, 0.0, sin)


def _na_bias_table(rpb):
    d = jnp.arange(NA_ROWS)
    i = jnp.arange(NA_ROWS)
    q = jnp.arange(GRID_W)
    k = jnp.arange(GRID_W)
    rel_r = i[None, :] - d[:, None] + (NA_ROWS - 1)
    rel_c = jnp.clip(k[None, :] - q[:, None], -15, 15) + 15
    cs = jnp.clip(q - 8, 0, GRID_W - 16)
    valid = (k[None, :] >= cs[:, None]) & (k[None, :] < cs[:, None] + 16)
    tab = rpb[:, :, rel_r[:, None, :, None], rel_c[None, :, None, :]]
    tab = jnp.where(valid[None, None, None, :, None, :], tab, NEG_INF)
    return tab.reshape(DEPTH, 4, NA_ROWS, GRID_W, NA_KEYS)


def _pad_in_proj(w_in):
    z = lambda n: jnp.zeros((DEPTH, D, n), w_in.dtype)
    return jnp.concatenate([w_in[:, :, 0:1280], w_in[:, :, 1472:1600], w_in[:, :, 1600:1632], z(96),
                            w_in[:, :, 1280:1472], z(64), w_in[:, :, 1632:2400]], axis=-1).astype(BF16)


def _split_heads_cols(w, first, second):
    lead = w.shape[:-1]
    w4 = w.reshape(*lead, 4, first + second)
    return jnp.concatenate([w4[..., :first].reshape(*lead, 4 * first),
                            w4[..., first:].reshape(*lead, 4 * second)], axis=-1)


def kernel(x_prompt, x_sample, cache_win_k, cache_win_v, cache_diff_k, cache_diff_v, cache_mla_ckv, cache_mla_kpe, cache_na_k, cache_na_v, c, c_ctx, w_ada, b_ada, norm_mix_pre, norm_mix_post, norm_ffn_pre, norm_ffn_post, w_in, w_o, win_sink, diff_lq1, diff_lk1, diff_lq2, diff_lk2, diff_subln, mla_q_norm, mla_kv_norm, mla_wuq, mla_wukv, na_rpb, ffn_w_gate, ffn_w_up, ffn_w_down, moe_router, moe_w_gate, moe_w_up, moe_w_down):
    w_in_p = _pad_in_proj(w_in)
    wuq_p = jnp.pad(_split_heads_cols(mla_wuq, 64, 32), ((0, 0), (0, 256 - MLA_Q_RANK), (0, 0))).astype(BF16)
    wukv_p = _split_heads_cols(mla_wukv, 64, 64).astype(BF16)
    qn_p = jnp.pad(mla_q_norm, ((0, 0), (0, 256 - MLA_Q_RANK))).reshape(DEPTH, 1, 256)
    kvn = mla_kv_norm.reshape(DEPTH, 1, 128)
    w_o_b = w_o.astype(BF16)
    ffn_wg, ffn_wu, ffn_wd = ffn_w_gate.astype(BF16), ffn_w_up.astype(BF16), ffn_w_down.astype(BF16)
    router_p = jnp.pad(moe_router, ((0, 0), (0, 0), (0, 128 - N_EXPERTS))).astype(BF16)
    g_mix_pre = norm_mix_pre.reshape(DEPTH, 1, D)
    g_mix_post = norm_mix_post.reshape(DEPTH, 1, D)
    g_ffn_pre = norm_ffn_pre.reshape(DEPTH, 1, D)
    g_ffn_post = norm_ffn_post.reshape(DEPTH, 1, D)
    tabs64 = _rope_tables(64)
    tabs32 = _rope_tables(32)
    na_bias = _na_bias_table(na_rpb)
    lam_init = jnp.asarray([0.8 - 0.6 * math.exp(-0.3 * l) for l in range(DEPTH)], F32)
    lam = (jnp.exp(jnp.sum(diff_lq1 * diff_lk1, axis=-1)) - jnp.exp(jnp.sum(diff_lq2 * diff_lk2, axis=-1))
           + lam_init)
    lam_tab = jnp.stack([lam, 1.0 - lam_init], axis=-1)
    subln = diff_subln.reshape(DEPTH, 1, HD)

    cond = jnp.concatenate([c_ctx[None], c, jnp.zeros((3, D), F32)], axis=0)
    mods = _mods(cond, w_ada, b_ada).reshape(DEPTH, 8, 6, D)

    x = jnp.concatenate([x_prompt.reshape(T_CTX, D), x_sample.reshape(T_LAT, D)], axis=0)
    new_caches = [[] for _ in range(8)]
    for l in range(DEPTH):
        pa, pb, pc, pd = _in_proj(l, x, mods, g_mix_pre, w_in_p, qn_p, kvn, wuq_p, wukv_p)
        ctx_out = _attn_ctx(win_sink[l], lam_tab[l], pa, pb, pc, pd, subln[l])
        o_ctx = ctx_out[0]
        for j, t in enumerate(ctx_out[1:]):
            new_caches[j].append(t)
        o_lat = [_lat_win(l, win_sink[l], pa, cache_win_k, cache_win_v, tabs64),
                 _lat_diff(l, lam_tab[l], pb, cache_diff_k, cache_diff_v, subln[l], tabs32),
                 _lat_mla(l, pc, cache_mla_ckv, cache_mla_kpe, wukv_p, tabs32),
                 _lat_na(l, pd, cache_na_k, cache_na_v, na_bias)]
        if l % 2 == 0:
            x1, h2 = _out_proj(l, o_ctx, o_lat, x, mods, w_o_b, g_mix_post, g_ffn_pre, None)
            x = _ffn(l, l // 2, h2, x1, mods, g_ffn_post, ffn_wg, ffn_wu, ffn_wd)
        else:
            x1, h2, route = _out_proj(l, o_ctx, o_lat, x, mods, w_o_b, g_mix_post, g_ffn_pre, router_p[l // 2])
            slot, tok_of_slot, item_e, item_x, item_ns = _route_plan(route)
            x_sorted = h2[tok_of_slot]
            y_sorted = _moe(l // 2, item_e, item_x, item_ns, x_sorted, moe_w_gate, moe_w_up, moe_w_down)
            y = route[:, 2:3] * y_sorted[slot[:, 0]] + route[:, 3:4] * y_sorted[slot[:, 1]]
            x = _post(l, y, x1, mods, g_ffn_post)

    y_prompt = x[:T_CTX].reshape(N_CTX_B, L_CTX, D)
    y_sample = x[T_CTX:].reshape(N_LAT_B, N_LAT, D)
    return (y_prompt, y_sample) + tuple(jnp.stack(ts, axis=1) for ts in new_caches)
```

```python
import functools
import math

import jax
import jax.numpy as jnp
from jax import lax
from jax.experimental import pallas as pl
from jax.experimental.pallas import tpu as pltpu

F32 = jnp.float32
BF16 = jnp.bfloat16

D = 1024
DEPTH = 4
N_CTX_B, L_CTX = 16, 256
N_LAT_B, N_LAT = 4, 1024
T_CTX = N_CTX_B * L_CTX
T_LAT = N_LAT_B * N_LAT
T = T_CTX + T_LAT
HD = 64
GRID_W = 64
WINDOW = 128
EPS = 1e-6
NEG_INF = -1e30
ROPE_THETA = 10000.0
MLA_Q_RANK = 192
D_FF = 2816
N_EXPERTS = 8
D_FF_EXPERT = 3584

PW = 2560
C_CKV, C_KPE, C_CQ, C_D = 1280, 1408, 1536, 1792
PC_W = 1152

TM = 512
EXPERT_CAP = T
ITEM_ROWS = 1024
SUB_ROWS = 512
N_ITEMS = N_EXPERTS + (2 * T) // ITEM_ROWS
FF_CHUNK = 512
DISPATCH_TM = 1024
FFN_CHUNK = 1408
VMEM_LIMIT = 56 * 1024 * 1024


def _cparams(sem):
    return pltpu.CompilerParams(dimension_semantics=sem, vmem_limit_bytes=VMEM_LIMIT)


def _rms(x, g):
    return x * lax.rsqrt(jnp.mean(x * x, axis=-1, keepdims=True) + EPS) * g


def _silu(a):
    return a / (1.0 + jnp.exp(-a))


def _dot(a, b):
    return jnp.dot(a, b, preferred_element_type=F32)


def _dot_t(a, b):
    return lax.dot_general(a, b, (((1,), (1,)), ((), ())), preferred_element_type=F32)


def _softmax_pv(scores, values, extra_logit=None):
    m = jnp.max(scores[0], axis=-1, keepdims=True)
    for s in scores[1:]:
        m = jnp.maximum(m, jnp.max(s, axis=-1, keepdims=True))
    if extra_logit is not None:
        m = jnp.maximum(m, extra_logit)
    l = None
    o = None
    for s, v in zip(scores, values):
        p = jnp.exp(s - m)
        ls = jnp.sum(p, axis=-1, keepdims=True)
        os_ = _dot(p.astype(BF16), v)
        l = ls if l is None else l + ls
        o = os_ if o is None else o + os_
    if extra_logit is not None:
        l = l + jnp.exp(extra_logit - m)
    return o / l


def _rope(x, c, sa, sb, shift):
    return x * c + pltpu.roll(x, 128 - shift, 1) * sa + pltpu.roll(x, shift, 1) * sb


def _mod_row(i):
    nct = T_CTX // TM
    per_b = N_LAT // TM
    return jnp.where(i >= nct, 1 + (i - nct) // per_b, 0)


def _mods_kernel(cond_ref, w_ref, b_ref, o_ref):
    a = _silu(cond_ref[...]).astype(BF16)
    o_ref[...] = _dot(a, w_ref[...].astype(BF16)) + b_ref[...]


def _mods(cond, w_ada, b_ada):
    tn = 1536
    n = 6 * D
    return pl.pallas_call(
        _mods_kernel,
        grid=(DEPTH, n // tn),
        in_specs=[pl.BlockSpec((8, D), lambda l, j: (0, 0)),
                  pl.BlockSpec((None, D, tn), lambda l, j: (l, 0, j)),
                  pl.BlockSpec((None, 1, tn), lambda l, j: (l, 0, j))],
        out_specs=pl.BlockSpec((None, 8, tn), lambda l, j: (l, 0, j)),
        out_shape=jax.ShapeDtypeStruct((DEPTH, 8, n), F32),
        compiler_params=_cparams(("arbitrary", "arbitrary")),
        name="mods",
    )(cond, w_ada, b_ada.reshape(DEPTH, 1, n))


def _in_proj_kernel(x_ref, mod_ref, g_ref, w_ref, qn_ref, kvn_ref, wuq_ref, wukv_ref,
                    pa_ref, pb_ref, pc_ref, pd_ref):
    m = mod_ref[...]
    h = _rms(x_ref[...], g_ref[...]) * (1.0 + m[1:2]) + m[0:1]
    p = _dot(h.astype(BF16), w_ref[...])
    pa_ref[...] = p[:, 0:512].astype(BF16)
    pb_ref[...] = p[:, 512:1280].astype(BF16)
    pd_ref[...] = p[:, C_D:PW].astype(BF16)
    ckv_n = _rms(p[:, C_CKV:C_CKV + 128], kvn_ref[...])
    z = p[:, C_CQ:C_CQ + 256]
    cqn = z * lax.rsqrt(jnp.sum(z * z, axis=-1, keepdims=True) / MLA_Q_RANK + EPS) * qn_ref[...]
    qc = _dot(cqn.astype(BF16), wuq_ref[...])
    kv = _dot(ckv_n.astype(BF16), wukv_ref[...])
    pc_ref[:, 0:384] = qc.astype(BF16)
    pc_ref[:, 384:512] = p[:, C_KPE:C_KPE + 128].astype(BF16)
    pc_ref[:, 512:1024] = kv.astype(BF16)
    pc_ref[:, 1024:1152] = ckv_n.astype(BF16)


def _in_proj(l, x, mods, g_pre, w_in_p, qn_p, kvn, wuq_p, wukv_p):
    row = lambda i: (i, 0)
    lyr2 = lambda i: (l, 0, 0)
    return pl.pallas_call(
        _in_proj_kernel,
        grid=(T // TM,),
        in_specs=[pl.BlockSpec((TM, D), row),
                  pl.BlockSpec((None, None, 6, D), lambda i: (l, _mod_row(i), 0, 0)),
                  pl.BlockSpec((None, 1, D), lyr2),
                  pl.BlockSpec((None, D, PW), lyr2),
                  pl.BlockSpec((None, 1, 256), lyr2),
                  pl.BlockSpec((None, 1, 128), lyr2),
                  pl.BlockSpec((None, 256, 384), lyr2),
                  pl.BlockSpec((None, 128, 512), lyr2)],
        out_specs=[pl.BlockSpec((TM, 512), row), pl.BlockSpec((TM, 768), row),
                   pl.BlockSpec((TM, PC_W), row), pl.BlockSpec((TM, 768), row)],
        out_shape=[jax.ShapeDtypeStruct((T, 512), BF16), jax.ShapeDtypeStruct((T, 768), BF16),
                   jax.ShapeDtypeStruct((T, PC_W), BF16), jax.ShapeDtypeStruct((T, 768), BF16)],
        compiler_params=_cparams(("arbitrary",)),
        name="in_proj",
    )(x, mods, g_pre, w_in_p, qn_p, kvn, wuq_p, wukv_p)


def _attn_ctx_kernel(sink_ref, lam_ref, pa, pb, pc, pd, subln_ref, o_ref,
                     ka_o, va_o, kb_o, vb_o, ckv_o, kpe_o, kd_o, vd_o):
    for h in range(4):
        g = h // 2
        s = _dot_t(pa[:, 64 * h:64 * h + 64], pa[:, 256 + 64 * g:320 + 64 * g])
        o = _softmax_pv([s], [pa[:, 384 + 64 * g:448 + 64 * g]], sink_ref[h])
        o_ref[:, 64 * h:64 * h + 64] = o.astype(BF16)
    for g in range(2):
        ka_o[g] = pa[:, 256 + 64 * g:320 + 64 * g].astype(F32)
        va_o[g] = pa[:, 384 + 64 * g:448 + 64 * g].astype(F32)
    lam = lam_ref[0]
    out_scale = lam_ref[1]
    for h in range(4):
        q0, k0, v0 = 64 * h, 256 + 64 * h, 512 + 64 * h
        v = pb[:, v0:v0 + 64]
        s1 = _dot_t(pb[:, q0:q0 + 32], pb[:, k0:k0 + 32])
        s2 = _dot_t(pb[:, q0 + 32:q0 + 64], pb[:, k0 + 32:k0 + 64])
        o = _softmax_pv([s1], [v]) - lam * _softmax_pv([s2], [v])
        o_ref[:, 256 + 64 * h:320 + 64 * h] = (_rms(o, subln_ref[...]) * out_scale).astype(BF16)
        kb_o[h] = pb[:, k0:k0 + 64].astype(F32)
        vb_o[h] = v.astype(F32)
    kpe = pc[:, 384:416]
    for h in range(4):
        s = (_dot_t(pc[:, 64 * h:64 * h + 64], pc[:, 512 + 64 * h:576 + 64 * h])
             + _dot_t(pc[:, 256 + 32 * h:288 + 32 * h], kpe))
        o = _softmax_pv([s], [pc[:, 768 + 64 * h:832 + 64 * h]])
        o_ref[:, 512 + 64 * h:576 + 64 * h] = o.astype(BF16)
    ckv_o[...] = pc[:, 1024:1152].astype(F32)
    kpe_o[...] = kpe.astype(F32)
    for h in range(4):
        q0, k0, v0 = 64 * h, 256 + 64 * h, 512 + 64 * h
        s = _dot_t(pd[:, q0:q0 + 64], pd[:, k0:k0 + 64])
        o = _softmax_pv([s], [pd[:, v0:v0 + 64]])
        o_ref[:, 768 + 64 * h:832 + 64 * h] = o.astype(BF16)
        kd_o[h] = pd[:, k0:k0 + 64].astype(F32)
        vd_o[h] = pd[:, v0:v0 + 64].astype(F32)


def _attn_ctx(sink, lam, pa, pb, pc, pd, subln):
    row = lambda b: (b, 0)
    smem = pl.BlockSpec(memory_space=pltpu.SMEM)
    hd4 = lambda n: pl.BlockSpec((None, n, L_CTX, HD), lambda b: (b, 0, 0, 0))
    hshape = lambda n: jax.ShapeDtypeStruct((N_CTX_B, n, L_CTX, HD), F32)
    return pl.pallas_call(
        _attn_ctx_kernel,
        grid=(N_CTX_B,),
        in_specs=[smem, smem,
                  pl.BlockSpec((L_CTX, 512), row), pl.BlockSpec((L_CTX, 768), row),
                  pl.BlockSpec((L_CTX, PC_W), row), pl.BlockSpec((L_CTX, 768), row),
                  pl.BlockSpec((1, HD), lambda b: (0, 0))],
        out_specs=[pl.BlockSpec((L_CTX, D), row),
                   hd4(2), hd4(2), hd4(4), hd4(4),
                   pl.BlockSpec((None, L_CTX, 128), lambda b: (b, 0, 0)),
                   pl.BlockSpec((None, L_CTX, 32), lambda b: (b, 0, 0)),
                   hd4(4), hd4(4)],
        out_shape=[jax.ShapeDtypeStruct((T_CTX, D), BF16),
                   hshape(2), hshape(2), hshape(4), hshape(4),
                   jax.ShapeDtypeStruct((N_CTX_B, L_CTX, 128), F32),
                   jax.ShapeDtypeStruct((N_CTX_B, L_CTX, 32), F32),
                   hshape(4), hshape(4)],
        compiler_params=_cparams(("arbitrary",)),
        name="attn_ctx",
    )(sink, lam, pa, pb, pc, pd, subln)


_LAT_ROW = lambda b: (T_CTX // N_LAT + b, 0)


def _cache_spec(l, n):
    return pl.BlockSpec((None, None, n, L_CTX, HD), lambda b: (b, l, 0, 0, 0))


def _tab_spec():
    return pl.BlockSpec((N_LAT, 128), lambda b: (0, 0))


def _lat_win_kernel(sink_ref, pa, ck_ref, cv_ref, c_ref, sa_ref, sb_ref, o_ref, qs, kpad, vpad, kcs, vcs):
    c, sa, sb = c_ref[...], sa_ref[...], sb_ref[...]
    for j in range(2):
        qs[:, 128 * j:128 * j + 128] = _rope(pa[:, 128 * j:128 * j + 128].astype(F32), c, sa, sb, 16).astype(BF16)
    kr = _rope(pa[:, 256:384].astype(F32), c, sa, sb, 16).astype(BF16)
    zeros = jnp.zeros((WINDOW, HD), BF16)
    for g in range(2):
        kpad[g, 0:WINDOW, :] = zeros
        kpad[g, WINDOW:WINDOW + N_LAT, :] = kr[:, 64 * g:64 * g + 64]
        kpad[g, WINDOW + N_LAT:, :] = zeros
        vpad[g, 0:WINDOW, :] = zeros
        vpad[g, WINDOW:WINDOW + N_LAT, :] = pa[:, 384 + 64 * g:448 + 64 * g]
        vpad[g, WINDOW + N_LAT:, :] = zeros
        kcs[g] = ck_ref[g].astype(BF16)
        vcs[g] = cv_ref[g].astype(BF16)
    qi = lax.broadcasted_iota(jnp.int32, (2 * WINDOW, 3 * WINDOW), 0) % WINDOW
    kj = lax.broadcasted_iota(jnp.int32, (2 * WINDOW, 3 * WINDOW), 1)
    band = jnp.abs(kj - WINDOW - qi) <= WINDOW
    first_head = lax.broadcasted_iota(jnp.int32, (2 * WINDOW, 1), 0) < WINDOW

    def body(nb, carry):
        r0 = pl.multiple_of(nb * WINDOW, WINDOW)
        kpos = r0 - WINDOW + kj
        valid = band & (kpos >= 0) & (kpos < N_LAT)
        for g in range(2):
            q = jnp.concatenate([qs[pl.ds(r0, WINDOW), 128 * g:128 * g + 64],
                                 qs[pl.ds(r0, WINDOW), 128 * g + 64:128 * g + 128]], axis=0)
            s_loc = jnp.where(valid, _dot_t(q, kpad[g, pl.ds(r0, 3 * WINDOW), :]), NEG_INF)
            s_ctx = _dot_t(q, kcs[g])
            sink = jnp.where(first_head, sink_ref[2 * g], sink_ref[2 * g + 1])
            o = _softmax_pv([s_loc, s_ctx], [vpad[g, pl.ds(r0, 3 * WINDOW), :], vcs[g]], sink).astype(BF16)
            o_ref[pl.ds(r0, WINDOW), 128 * g:128 * g + 64] = o[0:WINDOW]
            o_ref[pl.ds(r0, WINDOW), 128 * g + 64:128 * g + 128] = o[WINDOW:]
        return carry

    lax.fori_loop(0, N_LAT // WINDOW, body, 0)


def _lat_win(l, sink, pa, ck, cv, tabs):
    return pl.pallas_call(
        _lat_win_kernel,
        grid=(N_LAT_B,),
        in_specs=[pl.BlockSpec(memory_space=pltpu.SMEM),
                  pl.BlockSpec((N_LAT, 512), _LAT_ROW),
                  _cache_spec(l, 2), _cache_spec(l, 2),
                  _tab_spec(), _tab_spec(), _tab_spec()],
        out_specs=pl.BlockSpec((N_LAT, 256), lambda b: (b, 0)),
        out_shape=jax.ShapeDtypeStruct((T_LAT, 256), BF16),
        scratch_shapes=[pltpu.VMEM((N_LAT, 256), BF16),
                        pltpu.VMEM((2, N_LAT + 2 * WINDOW, HD), BF16),
                        pltpu.VMEM((2, N_LAT + 2 * WINDOW, HD), BF16),
                        pltpu.VMEM((2, L_CTX, HD), BF16),
                        pltpu.VMEM((2, L_CTX, HD), BF16)],
        compiler_params=_cparams(("arbitrary",)),
        name="lat_win",
    )(sink, pa, ck, cv, *tabs)


QT = 256
N_KEYS = N_LAT + L_CTX


def _lat_diff_kernel(lam_ref, pb, ck_ref, cv_ref, subln_ref, c_ref, sa_ref, sb_ref, o_ref, qs, kall, vall):
    c, sa, sb = c_ref[...], sa_ref[...], sb_ref[...]
    for j in range(2):
        qs[:, 128 * j:128 * j + 128] = _rope(pb[:, 128 * j:128 * j + 128].astype(F32), c, sa, sb, 8).astype(BF16)
        kr = _rope(pb[:, 256 + 128 * j:384 + 128 * j].astype(F32), c, sa, sb, 8).astype(BF16)
        for r in range(2):
            kall[2 * j + r, 0:N_LAT, :] = kr[:, 64 * r:64 * r + 64]
    for h in range(4):
        kall[h, N_LAT:, :] = ck_ref[h].astype(BF16)
        vall[h, 0:N_LAT, :] = pb[:, 512 + 64 * h:576 + 64 * h]
        vall[h, N_LAT:, :] = cv_ref[h].astype(BF16)
    lam = lam_ref[0]
    out_scale = lam_ref[1]
    for h in range(4):
        def body(t, carry, h=h):
            r0 = pl.multiple_of(t * QT, QT)
            v = vall[h]
            s1 = _dot_t(qs[pl.ds(r0, QT), 64 * h:64 * h + 32], kall[h, :, 0:32])
            s2 = _dot_t(qs[pl.ds(r0, QT), 64 * h + 32:64 * h + 64], kall[h, :, 32:64])
            o = _softmax_pv([s1], [v]) - lam * _softmax_pv([s2], [v])
            o_ref[pl.ds(r0, QT), 64 * h:64 * h + 64] = (_rms(o, subln_ref[...]) * out_scale).astype(BF16)
            return carry

        lax.fori_loop(0, N_LAT // QT, body, 0)


def _lat_diff(l, lam, pb, ck, cv, subln, tabs):
    return pl.pallas_call(
        _lat_diff_kernel,
        grid=(N_LAT_B,),
        in_specs=[pl.BlockSpec(memory_space=pltpu.SMEM),
                  pl.BlockSpec((N_LAT, 768), _LAT_ROW),
                  _cache_spec(l, 4), _cache_spec(l, 4),
                  pl.BlockSpec((1, HD), lambda b: (0, 0)),
                  _tab_spec(), _tab_spec(), _tab_spec()],
        out_specs=pl.BlockSpec((N_LAT, 256), lambda b: (b, 0)),
        out_shape=jax.ShapeDtypeStruct((T_LAT, 256), BF16),
        scratch_shapes=[pltpu.VMEM((N_LAT, 256), BF16),
                        pltpu.VMEM((4, N_KEYS, HD), BF16),
                        pltpu.VMEM((4, N_KEYS, HD), BF16)],
        compiler_params=_cparams(("arbitrary",)),
        name="lat_diff",
    )(lam, pb, ck, cv, subln, *tabs)


def _lat_mla_kernel(pc, ckv_ref, kpe_ref, wukv_ref, c_ref, sa_ref, sb_ref, o_ref, qcat, kcat, vall):
    c, sa, sb = c_ref[...], sa_ref[...], sb_ref[...]
    qr = _rope(pc[:, 256:384].astype(F32), c, sa, sb, 8).astype(BF16)
    kpe_l = _rope(pc[:, 384:512].astype(F32), c, sa, sb, 8).astype(BF16)
    kvc = _dot(ckv_ref[...].astype(BF16), wukv_ref[...]).astype(BF16)
    kpe_c = kpe_ref[...].astype(BF16)
    for h in range(4):
        qcat[h, :, 0:64] = pc[:, 64 * h:64 * h + 64]
        qcat[h, :, 64:96] = qr[:, 32 * h:32 * h + 32]
        kcat[h, 0:N_LAT, 0:64] = pc[:, 512 + 64 * h:576 + 64 * h]
        kcat[h, 0:N_LAT, 64:96] = kpe_l[:, 0:32]
        kcat[h, N_LAT:, 0:64] = kvc[:, 64 * h:64 * h + 64]
        kcat[h, N_LAT:, 64:96] = kpe_c
        vall[h, 0:N_LAT, :] = pc[:, 768 + 64 * h:832 + 64 * h]
        vall[h, N_LAT:, :] = kvc[:, 256 + 64 * h:320 + 64 * h]
    for h in range(4):
        def body(t, carry, h=h):
            r0 = pl.multiple_of(t * QT, QT)
            s = _dot_t(qcat[h, pl.ds(r0, QT), :], kcat[h])
            o = _softmax_pv([s], [vall[h]])
            o_ref[pl.ds(r0, QT), 64 * h:64 * h + 64] = o.astype(BF16)
            return carry

        lax.fori_loop(0, N_LAT // QT, body, 0)


def _lat_mla(l, pc, ckv, kpe, wukv_p, tabs):
    return pl.pallas_call(
        _lat_mla_kernel,
        grid=(N_LAT_B,),
        in_specs=[pl.BlockSpec((N_LAT, PC_W), _LAT_ROW),
                  pl.BlockSpec((None, None, L_CTX, 128), lambda b: (b, l, 0, 0)),
                  pl.BlockSpec((None, None, L_CTX, 32), lambda b: (b, l, 0, 0)),
                  pl.BlockSpec((None, 128, 512), lambda b: (l, 0, 0)),
                  _tab_spec(), _tab_spec(), _tab_spec()],
        out_specs=pl.BlockSpec((N_LAT, 256), lambda b: (b, 0)),
        out_shape=jax.ShapeDtypeStruct((T_LAT, 256), BF16),
        scratch_shapes=[pltpu.VMEM((4, N_LAT, 96), BF16),
                        pltpu.VMEM((4, N_KEYS, 96), BF16),
                        pltpu.VMEM((4, N_KEYS, HD), BF16)],
        compiler_params=_cparams(("arbitrary",)),
        name="lat_mla",
    )(pc, ckv, kpe, wukv_p, *tabs)


NA_ROWS = 8
NA_KEYS = NA_ROWS * GRID_W


def _lat_na_kernel(pd, ck_ref, cv_ref, bias_ref, o_ref, kcs, vcs):
    n_rows = N_LAT // GRID_W
    for h in range(4):
        kcs[h] = ck_ref[h].astype(BF16)
        vcs[h] = cv_ref[h].astype(BF16)

    def body(r, carry):
        rs = jnp.clip(r - NA_ROWS // 2, 0, n_rows - NA_ROWS)
        q0 = pl.multiple_of(r * GRID_W, GRID_W)
        k0 = pl.multiple_of(rs * GRID_W, GRID_W)
        for h in range(4):
            q = pd[pl.ds(q0, GRID_W), 64 * h:64 * h + 64]
            kw = pd[pl.ds(k0, NA_KEYS), 256 + 64 * h:320 + 64 * h]
            vw = pd[pl.ds(k0, NA_KEYS), 512 + 64 * h:576 + 64 * h]
            s_loc = _dot_t(q, kw) + bias_ref[h, r - rs]
            s_ctx = _dot_t(q, kcs[h])
            o = _softmax_pv([s_loc, s_ctx], [vw, vcs[h]])
            o_ref[pl.ds(q0, GRID_W), 64 * h:64 * h + 64] = o.astype(BF16)
        return carry

    lax.fori_loop(0, n_rows, body, 0)


def _lat_na(l, pd, ck, cv, bias):
    return pl.pallas_call(
        _lat_na_kernel,
        grid=(N_LAT_B,),
        in_specs=[pl.BlockSpec((N_LAT, 768), _LAT_ROW),
                  _cache_spec(l, 4), _cache_spec(l, 4),
                  pl.BlockSpec((None, 4, NA_ROWS, GRID_W, NA_KEYS), lambda b: (l, 0, 0, 0, 0))],
        out_specs=pl.BlockSpec((N_LAT, 256), lambda b: (b, 0)),
        out_shape=jax.ShapeDtypeStruct((T_LAT, 256), BF16),
        scratch_shapes=[pltpu.VMEM((4, L_CTX, HD), BF16), pltpu.VMEM((4, L_CTX, HD), BF16)],
        compiler_params=_cparams(("arbitrary",)),
        name="lat_na",
    )(pd, ck, cv, bias)


def _out_proj_kernel(with_router, *refs):
    if with_router:
        (oc_ref, oa_ref, ob_ref, om_ref, od_ref, x_ref, mod_ref, wo_ref, gpost_ref, gffn_ref, r_ref,
         x1_ref, h2_ref, route_ref, cnt_ref) = refs
    else:
        (oc_ref, oa_ref, ob_ref, om_ref, od_ref, x_ref, mod_ref, wo_ref, gpost_ref, gffn_ref,
         x1_ref, h2_ref) = refs
    i = pl.program_id(0)
    nct = T_CTX // TM

    if with_router:
        @pl.when(i == 0)
        def _():
            cnt_ref[...] = jnp.zeros(cnt_ref.shape, F32)

    def finish(y):
        m = mod_ref[...]
        x1 = x_ref[...] + m[2:3] * _rms(y, gpost_ref[...])
        x1_ref[...] = x1
        h2 = _rms(x1, gffn_ref[...]) * (1.0 + m[4:5]) + m[3:4]
        h2_ref[...] = h2.astype(h2_ref.dtype)
        if with_router:
            lg = _dot(h2.astype(BF16), r_ref[...])
            lane = lax.broadcasted_iota(jnp.int32, lg.shape, 1)
            lane_f = lane.astype(F32)
            lg = jnp.where(lane < N_EXPERTS, lg, -jnp.inf)
            m1 = jnp.max(lg, axis=-1, keepdims=True)
            i1 = jnp.min(jnp.where(lg == m1, lane_f, 128.0), axis=-1, keepdims=True)
            lg2 = jnp.where(lane_f == i1, -jnp.inf, lg)
            m2 = jnp.max(lg2, axis=-1, keepdims=True)
            i2 = jnp.min(jnp.where(lg2 == m2, lane_f, 128.0), axis=-1, keepdims=True)
            e = jnp.exp(m2 - m1)
            g1 = 1.0 / (1.0 + e)
            g2 = e / (1.0 + e)
            sel1 = lane_f == i1
            sel2 = lane_f == i2
            picked = jnp.where(sel1 | sel2, 1.0, 0.0)
            ri = lax.broadcasted_iota(jnp.int32, (TM, TM), 0)
            ci = lax.broadcasted_iota(jnp.int32, (TM, TM), 1)
            before = _dot(jnp.where(ci < ri, 1.0, 0.0).astype(BF16), picked.astype(BF16)) + cnt_ref[0:1, :]
            r1 = jnp.sum(jnp.where(sel1, before, 0.0), axis=-1, keepdims=True)
            r2 = jnp.sum(jnp.where(sel2, before, 0.0), axis=-1, keepdims=True)
            cnt_ref[...] = cnt_ref[...] + jnp.sum(picked, axis=0, keepdims=True)
            route = jnp.where(lane == 0, i1, jnp.where(lane == 1, i2, jnp.where(lane == 2, g1, 0.0)))
            route_ref[...] = jnp.where(lane == 3, g2, jnp.where(lane == 4, r1, jnp.where(lane == 5, r2, route)))

    @pl.when(i < nct)
    def _():
        finish(_dot(oc_ref[...], wo_ref[...]))

    @pl.when(i >= nct)
    def _():
        y = _dot(oa_ref[...], wo_ref[0:256, :])
        y = y + _dot(ob_ref[...], wo_ref[256:512, :])
        y = y + _dot(om_ref[...], wo_ref[512:768, :])
        y = y + _dot(od_ref[...], wo_ref[768:1024, :])
        finish(y)


def _out_proj(l, o_ctx, o_lat, x, mods, w_o, g_post, g_ffn, router_p):
    nct = T_CTX // TM
    row = lambda i: (i, 0)
    lyr2 = lambda i: (l, 0, 0)
    lat = lambda i: (jnp.maximum(i - nct, 0), 0)
    with_router = router_p is not None
    in_specs = [pl.BlockSpec((TM, D), lambda i: (jnp.minimum(i, nct - 1), 0)),
                pl.BlockSpec((TM, 256), lat), pl.BlockSpec((TM, 256), lat),
                pl.BlockSpec((TM, 256), lat), pl.BlockSpec((TM, 256), lat),
                pl.BlockSpec((TM, D), row),
                pl.BlockSpec((None, None, 6, D), lambda i: (l, _mod_row(i), 0, 0)),
                pl.BlockSpec((None, D, D), lyr2),
                pl.BlockSpec((None, 1, D), lyr2),
                pl.BlockSpec((None, 1, D), lyr2)]
    out_specs = [pl.BlockSpec((TM, D), row), pl.BlockSpec((TM, D), row)]
    out_shape = [jax.ShapeDtypeStruct((T, D), F32), jax.ShapeDtypeStruct((T, D), F32 if with_router else BF16)]
    args = [o_ctx, *o_lat, x, mods, w_o, g_post, g_ffn]
    if with_router:
        in_specs.append(pl.BlockSpec((D, 128), lambda i: (0, 0)))
        out_specs += [pl.BlockSpec((TM, 128), row), pl.BlockSpec((8, 128), lambda i: (0, 0))]
        out_shape += [jax.ShapeDtypeStruct((T, 128), F32), jax.ShapeDtypeStruct((8, 128), F32)]
        args.append(router_p)
    return pl.pallas_call(
        functools.partial(_out_proj_kernel, with_router),
        grid=(T // TM,),
        in_specs=in_specs, out_specs=out_specs, out_shape=out_shape,
        compiler_params=_cparams(("arbitrary",)),
        name="out_proj_router" if with_router else "out_proj",
    )(*args)


FFN_TM = 1024


def _ffn_kernel(h_ref, x_ref, mod_ref, g_ref, wg_ref, wu_ref, wd_ref, o_ref, acc_ref):
    c = pl.program_id(1)
    h = h_ref[...]
    t = (_silu(_dot(h, wg_ref[...])) * _dot(h, wu_ref[...])).astype(BF16)
    part = _dot(t, wd_ref[...])

    @pl.when(c == 0)
    def _():
        acc_ref[...] = part

    @pl.when(c > 0)
    def _():
        acc_ref[...] += part

    @pl.when(c == pl.num_programs(1) - 1)
    def _():
        m = mod_ref[...]
        o_ref[...] = x_ref[...] + m[5:6] * _rms(acc_ref[...], g_ref[...])


def _ffn_mod_row(i):
    nct = T_CTX // FFN_TM
    return jnp.where(i >= nct, 1 + (i - nct) // (N_LAT // FFN_TM), 0)


def _ffn(l, i_dense, h2, x1, mods, g_post, wg, wu, wd):
    row = lambda i, c: (i, 0)
    return pl.pallas_call(
        _ffn_kernel,
        grid=(T // FFN_TM, D_FF // FFN_CHUNK),
        in_specs=[pl.BlockSpec((FFN_TM, D), row),
                  pl.BlockSpec((FFN_TM, D), row),
                  pl.BlockSpec((None, None, 6, D), lambda i, c: (l, _ffn_mod_row(i), 0, 0)),
                  pl.BlockSpec((None, 1, D), lambda i, c: (l, 0, 0)),
                  pl.BlockSpec((None, D, FFN_CHUNK), lambda i, c: (i_dense, 0, c)),
                  pl.BlockSpec((None, D, FFN_CHUNK), lambda i, c: (i_dense, 0, c)),
                  pl.BlockSpec((None, FFN_CHUNK, D), lambda i, c: (i_dense, c, 0))],
        out_specs=pl.BlockSpec((FFN_TM, D), row),
        out_shape=jax.ShapeDtypeStruct((T, D), F32),
        scratch_shapes=[pltpu.VMEM((FFN_TM, D), F32)],
        compiler_params=_cparams(("arbitrary", "arbitrary")),
        name="ffn",
    )(h2, x1, mods, g_post, wg, wu, wd)


def _dispatch_kernel(slot_ref, h_hbm, xs_hbm, sem):
    base = pl.program_id(0) * DISPATCH_TM

    def row_copy(t, k):
        return pltpu.make_async_copy(h_hbm.at[pl.ds(base + t, 1)],
                                     xs_hbm.at[pl.ds(slot_ref[0, 2 * t + k], 1)], sem)

    def issue(t, carry):
        row_copy(t, 0).start()
        row_copy(t, 1).start()
        return carry

    def drain(t, carry):
        row_copy(t, 0).wait()
        row_copy(t, 1).wait()
        return carry

    lax.fori_loop(0, DISPATCH_TM, issue, 0)
    lax.fori_loop(0, DISPATCH_TM, drain, 0)


def _dispatch(slots, h2):
    return pl.pallas_call(
        _dispatch_kernel,
        grid=(T // DISPATCH_TM,),
        in_specs=[pl.BlockSpec((None, 1, 2 * DISPATCH_TM), lambda i: (i, 0, 0), memory_space=pltpu.SMEM),
                  pl.BlockSpec(memory_space=pl.ANY)],
        out_specs=pl.BlockSpec(memory_space=pl.ANY),
        out_shape=jax.ShapeDtypeStruct((N_EXPERTS * EXPERT_CAP, D), F32),
        scratch_shapes=[pltpu.SemaphoreType.DMA(())],
        compiler_params=_cparams(("arbitrary",)),
        name="moe_dispatch",
    )(slots.reshape(T // DISPATCH_TM, 1, 2 * DISPATCH_TM), h2)


def _moe_kernel(ie_ref, ib_ref, ns_ref, nr_ref, x_ref, wg_ref, wu_ref, wd_ref, o_ref, wgb, wub, wdb):
    i = pl.program_id(0)
    c = pl.program_id(1)
    ns = ns_ref[i]
    n_rows = nr_ref[i]

    @pl.when(ns > 0)
    def _():
        @pl.when(c == 0)
        def _():
            o_ref[...] = jnp.zeros(o_ref.shape, F32)

        wgb[...] = wg_ref[...].astype(BF16)
        wub[...] = wu_ref[...].astype(BF16)
        wdb[...] = wd_ref[...].astype(BF16)

        def body(s, carry):
            r0 = pl.multiple_of(s * SUB_ROWS, SUB_ROWS)
            live = r0 + lax.broadcasted_iota(jnp.int32, (SUB_ROWS, 1), 0) < n_rows
            xs = jnp.where(live, x_ref[pl.ds(r0, SUB_ROWS), :], 0.0).astype(BF16)
            t = (_silu(_dot(xs, wgb[...])) * _dot(xs, wub[...])).astype(BF16)
            o_ref[pl.ds(r0, SUB_ROWS), :] += _dot(t, wdb[...])
            return carry

        lax.fori_loop(0, ns, body, 0)


def _moe(i_moe, item_e, item_blk, item_ns, item_rows, x_sorted, wg, wu, wd):
    nc = D_FF_EXPERT // FF_CHUNK
    chunk = lambda i, c, ns: jnp.where(ns[i] > 0, c, nc - 1)
    grid_spec = pltpu.PrefetchScalarGridSpec(
        num_scalar_prefetch=4,
        grid=(N_ITEMS, nc),
        in_specs=[pl.BlockSpec((ITEM_ROWS, D), lambda i, c, ie, ib, ns, nr: (ib[i], 0)),
                  pl.BlockSpec((None, None, D, FF_CHUNK),
                               lambda i, c, ie, ib, ns, nr: (i_moe, ie[i], 0, chunk(i, c, ns))),
                  pl.BlockSpec((None, None, D, FF_CHUNK),
                               lambda i, c, ie, ib, ns, nr: (i_moe, ie[i], 0, chunk(i, c, ns))),
                  pl.BlockSpec((None, None, FF_CHUNK, D),
                               lambda i, c, ie, ib, ns, nr: (i_moe, ie[i], chunk(i, c, ns), 0))],
        out_specs=pl.BlockSpec((ITEM_ROWS, D), lambda i, c, ie, ib, ns, nr: (ib[i], 0)),
        scratch_shapes=[pltpu.VMEM((D, FF_CHUNK), BF16), pltpu.VMEM((D, FF_CHUNK), BF16),
                        pltpu.VMEM((FF_CHUNK, D), BF16)],
    )
    return pl.pallas_call(
        _moe_kernel,
        grid_spec=grid_spec,
        out_shape=jax.ShapeDtypeStruct((N_EXPERTS * EXPERT_CAP, D), F32),
        compiler_params=_cparams(("arbitrary", "arbitrary")),
        name="moe_experts",
    )(item_e, item_blk, item_ns, item_rows, x_sorted, wg, wu, wd)


def _post_kernel(slot_ref, ys_hbm, route_ref, x_ref, mod_ref, g_ref, o_ref, buf, sem):
    def row_copy(t, k):
        return pltpu.make_async_copy(ys_hbm.at[pl.ds(slot_ref[0, 2 * t + k], 1)], buf.at[k, pl.ds(t, 1)], sem)

    def issue(t, carry):
        row_copy(t, 0).start()
        row_copy(t, 1).start()
        return carry

    def drain(t, carry):
        row_copy(t, 0).wait()
        row_copy(t, 1).wait()
        return carry

    lax.fori_loop(0, TM, issue, 0)
    lax.fori_loop(0, TM, drain, 0)
    r = route_ref[...]
    y = r[:, 2:3] * buf[0] + r[:, 3:4] * buf[1]
    m = mod_ref[...]
    o_ref[...] = x_ref[...] + m[5:6] * _rms(y, g_ref[...])


def _post(l, slots, y_sorted, route, x1, mods, g_post):
    row = lambda i: (i, 0)
    return pl.pallas_call(
        _post_kernel,
        grid=(T // TM,),
        in_specs=[pl.BlockSpec((None, 1, 2 * TM), lambda i: (i, 0, 0), memory_space=pltpu.SMEM),
                  pl.BlockSpec(memory_space=pl.ANY),
                  pl.BlockSpec((TM, 128), row), pl.BlockSpec((TM, D), row),
                  pl.BlockSpec((None, None, 6, D), lambda i: (l, _mod_row(i), 0, 0)),
                  pl.BlockSpec((None, 1, D), lambda i: (l, 0, 0))],
        out_specs=pl.BlockSpec((TM, D), row),
        out_shape=jax.ShapeDtypeStruct((T, D), F32),
        scratch_shapes=[pltpu.VMEM((2, TM, D), F32), pltpu.SemaphoreType.DMA(())],
        compiler_params=_cparams(("arbitrary",)),
        name="moe_post",
    )(slots.reshape(T // TM, 1, 2 * TM), y_sorted, route, x1, mods, g_post)


def _route_plan(route, counts):
    i32 = jnp.int32
    slots = (route[:, 0:2].astype(i32) * EXPERT_CAP + route[:, 4:6].astype(i32)).reshape(-1)
    n = counts[0, :N_EXPERTS].astype(i32)
    n_items = (n + ITEM_ROWS - 1) // ITEM_ROWS
    ends = jnp.cumsum(n_items)
    base = ends - n_items
    total = ends[-1]
    ii = jnp.arange(N_ITEMS, dtype=i32)
    ii_c = jnp.minimum(ii, total - 1)
    item_e = jnp.sum((ends[None, :] <= ii_c[:, None]).astype(i32), axis=1)
    j = ii_c - base[item_e]
    item_blk = item_e * (EXPERT_CAP // ITEM_ROWS) + j
    rows = jnp.where(ii < total, jnp.clip(n[item_e] - j * ITEM_ROWS, 0, ITEM_ROWS), 0)
    item_ns = (rows + SUB_ROWS - 1) // SUB_ROWS
    return slots, item_e.astype(i32), item_blk.astype(i32), item_ns.astype(i32), rows.astype(i32)


def _rope_tables(d):
    half, quarter = d // 2, d // 4
    t = jnp.arange(N_LAT)
    pr, pc = t // GRID_W, t % GRID_W
    i = jnp.arange(128) % d
    j = (i % half) % quarter
    inv = ROPE_THETA ** (-(2 * j).astype(F32) / half)
    pos = jnp.where((i < half)[None, :], pr[:, None], pc[:, None]).astype(F32)
    ang = pos * inv[None, :]
    low = ((i % half) < quarter)[None, :]
    sin = jnp.sin(ang)
    return jnp.cos(ang), jnp.where(low, -sin, 0.0), jnp.where(low, 0.0, sin)


def _na_bias_table(rpb):
    q = jnp.arange(GRID_W)
    k = jnp.arange(GRID_W)
    rel_c = jnp.clip(k[None, :] - q[:, None], -15, 15) + 15
    cs = jnp.clip(q - 8, 0, GRID_W - 16)
    valid = (k[None, :] >= cs[:, None]) & (k[None, :] < cs[:, None] + 16)
    pick = rel_c[:, :, None] == jnp.arange(31)[None, None, :]
    band = jnp.sum(jnp.where(pick[None, None, None], rpb[:, :, :, None, None, :], 0.0), axis=-1)
    band = jnp.where(valid[None, None, None], band, NEG_INF)
    tab = jnp.stack([band[:, :, NA_ROWS - 1 - dd:2 * NA_ROWS - 1 - dd] for dd in range(NA_ROWS)], axis=2)
    tab = tab.transpose(0, 1, 2, 4, 3, 5)
    return tab.reshape(DEPTH, 4, NA_ROWS, GRID_W, NA_KEYS)


def _pad_in_proj(w_in):
    z = lambda n: jnp.zeros((DEPTH, D, n), w_in.dtype)
    s64, s32 = HD ** -0.5, (HD // 2) ** -0.5
    return jnp.concatenate([w_in[:, :, 0:256] * s64, w_in[:, :, 256:512],
                            w_in[:, :, 512:768] * s32, w_in[:, :, 768:1280],
                            w_in[:, :, 1472:1600], w_in[:, :, 1600:1632], z(96),
                            w_in[:, :, 1280:1472], z(64),
                            w_in[:, :, 1632:1888] * s64, w_in[:, :, 1888:2400]], axis=-1).astype(BF16)


def _split_heads_cols(w, first, second):
    lead = w.shape[:-1]
    w4 = w.reshape(*lead, 4, first + second)
    return jnp.concatenate([w4[..., :first].reshape(*lead, 4 * first),
                            w4[..., first:].reshape(*lead, 4 * second)], axis=-1)


def kernel(x_prompt, x_sample, cache_win_k, cache_win_v, cache_diff_k, cache_diff_v, cache_mla_ckv, cache_mla_kpe, cache_na_k, cache_na_v, c, c_ctx, w_ada, b_ada, norm_mix_pre, norm_mix_post, norm_ffn_pre, norm_ffn_post, w_in, w_o, win_sink, diff_lq1, diff_lk1, diff_lq2, diff_lk2, diff_subln, mla_q_norm, mla_kv_norm, mla_wuq, mla_wukv, na_rpb, ffn_w_gate, ffn_w_up, ffn_w_down, moe_router, moe_w_gate, moe_w_up, moe_w_down):
    w_in_p = _pad_in_proj(w_in)
    wuq_p = jnp.pad(_split_heads_cols(mla_wuq, 64, 32) * (HD + HD // 2) ** -0.5,
                    ((0, 0), (0, 256 - MLA_Q_RANK), (0, 0))).astype(BF16)
    wukv_p = _split_heads_cols(mla_wukv, 64, 64).astype(BF16)
    qn_p = jnp.pad(mla_q_norm, ((0, 0), (0, 256 - MLA_Q_RANK))).reshape(DEPTH, 1, 256)
    kvn = mla_kv_norm.reshape(DEPTH, 1, 128)
    w_o_b = w_o.astype(BF16)
    ffn_wg, ffn_wu, ffn_wd = ffn_w_gate.astype(BF16), ffn_w_up.astype(BF16), ffn_w_down.astype(BF16)
    router_p = jnp.pad(moe_router, ((0, 0), (0, 0), (0, 128 - N_EXPERTS))).astype(BF16)
    g_mix_pre = norm_mix_pre.reshape(DEPTH, 1, D)
    g_mix_post = norm_mix_post.reshape(DEPTH, 1, D)
    g_ffn_pre = norm_ffn_pre.reshape(DEPTH, 1, D)
    g_ffn_post = norm_ffn_post.reshape(DEPTH, 1, D)
    tabs64 = _rope_tables(64)
    tabs32 = _rope_tables(32)
    na_bias = _na_bias_table(na_rpb)
    lam_init = jnp.asarray([0.8 - 0.6 * math.exp(-0.3 * l) for l in range(DEPTH)], F32)
    lam = (jnp.exp(jnp.sum(diff_lq1 * diff_lk1, axis=-1)) - jnp.exp(jnp.sum(diff_lq2 * diff_lk2, axis=-1))
           + lam_init)
    lam_tab = jnp.stack([lam, 1.0 - lam_init], axis=-1)
    subln = diff_subln.reshape(DEPTH, 1, HD)

    cond = jnp.concatenate([c_ctx[None], c, jnp.zeros((3, D), F32)], axis=0)
    mods = _mods(cond, w_ada, b_ada).reshape(DEPTH, 8, 6, D)

    x = jnp.concatenate([x_prompt.reshape(T_CTX, D), x_sample.reshape(T_LAT, D)], axis=0)
    new_caches = [[] for _ in range(8)]
    for l in range(DEPTH):
        pa, pb, pc, pd = _in_proj(l, x, mods, g_mix_pre, w_in_p, qn_p, kvn, wuq_p, wukv_p)
        ctx_out = _attn_ctx(win_sink[l], lam_tab[l], pa, pb, pc, pd, subln[l])
        o_ctx = ctx_out[0]
        for j, t in enumerate(ctx_out[1:]):
            new_caches[j].append(t)
        o_lat = [_lat_win(l, win_sink[l], pa, cache_win_k, cache_win_v, tabs64),
                 _lat_diff(l, lam_tab[l], pb, cache_diff_k, cache_diff_v, subln[l], tabs32),
                 _lat_mla(l, pc, cache_mla_ckv, cache_mla_kpe, wukv_p, tabs32),
                 _lat_na(l, pd, cache_na_k, cache_na_v, na_bias)]
        if l % 2 == 0:
            x1, h2 = _out_proj(l, o_ctx, o_lat, x, mods, w_o_b, g_mix_post, g_ffn_pre, None)
            x = _ffn(l, l // 2, h2, x1, mods, g_ffn_post, ffn_wg, ffn_wu, ffn_wd)
        else:
            x1, h2, route, counts = _out_proj(l, o_ctx, o_lat, x, mods, w_o_b, g_mix_post, g_ffn_pre,
                                              router_p[l // 2])
            slots, item_e, item_blk, item_ns, item_rows = _route_plan(route, counts)
            x_sorted = _dispatch(slots, h2)
            y_sorted = _moe(l // 2, item_e, item_blk, item_ns, item_rows, x_sorted,
                            moe_w_gate, moe_w_up, moe_w_down)
            x = _post(l, slots, y_sorted, route, x1, mods, g_ffn_post)

    y_prompt = x[:T_CTX].reshape(N_CTX_B, L_CTX, D)
    y_sample = x[T_CTX:].reshape(N_LAT_B, N_LAT, D)
    return (y_prompt, y_sample) + tuple(jnp.stack(ts, axis=1) for ts in new_caches)
```

```python
import functools
import math

import jax
import jax.numpy as jnp
from jax import lax
from jax.experimental import pallas as pl
from jax.experimental.pallas import tpu as pltpu

F32 = jnp.float32
BF16 = jnp.bfloat16

D = 1024
DEPTH = 4
N_CTX_B, L_CTX = 16, 256
N_LAT_B, N_LAT = 4, 1024
T_CTX = N_CTX_B * L_CTX
T_LAT = N_LAT_B * N_LAT
T = T_CTX + T_LAT
HD = 64
GRID_W = 64
WINDOW = 128
EPS = 1e-6
NEG_INF = -1e30
ROPE_THETA = 10000.0
MLA_Q_RANK = 192
D_FF = 2816
N_EXPERTS = 8
D_FF_EXPERT = 3584

PW = 2560
C_CKV, C_KPE, C_CQ, C_D = 1280, 1408, 1536, 1792
PC_W = 1152

TM = 512
EXPERT_CAP = T
ITEM_ROWS = 1024
SUB_ROWS = 512
N_ITEMS = N_EXPERTS + (2 * T) // ITEM_ROWS
FF_CHUNK = 512
DISPATCH_TM = 1024
FFN_CHUNK = 1408
VMEM_LIMIT = 56 * 1024 * 1024


def _cparams(sem):
    return pltpu.CompilerParams(dimension_semantics=sem, vmem_limit_bytes=VMEM_LIMIT)


def _rms(x, g):
    return x * lax.rsqrt(jnp.mean(x * x, axis=-1, keepdims=True) + EPS) * g


def _silu(a):
    return a / (1.0 + jnp.exp(-a))


def _dot(a, b):
    return jnp.dot(a, b, preferred_element_type=F32)


def _dot_t(a, b):
    return lax.dot_general(a, b, (((1,), (1,)), ((), ())), preferred_element_type=F32)


def _with_ones(v):
    return jnp.concatenate([v, jnp.ones(v.shape, v.dtype)], axis=1)


def _attend_scores(scores, vexts, extra_logit=None):
    mx = jnp.max(scores[0], axis=-1, keepdims=True)
    for s in scores[1:]:
        mx = jnp.maximum(mx, jnp.max(s, axis=-1, keepdims=True))
    if extra_logit is not None:
        mx = jnp.maximum(mx, extra_logit)
    r = None
    for s, v in zip(scores, vexts):
        part = _dot(jnp.exp(s - mx).astype(BF16), v)
        r = part if r is None else r + part
    den = r[:, 128:256]
    if extra_logit is not None:
        den = den + jnp.exp(extra_logit - mx)
    return r[:, 0:128] / den


def _attend(q, keys, vexts, extra_logit=None):
    return _attend_scores([_dot_t(q, k) for k in keys], vexts, extra_logit)


def _keep_lanes(x, *ranges):
    lane = lax.broadcasted_iota(jnp.int32, x.shape, 1)
    keep = None
    for lo, width in ranges:
        k = (lane >= lo) & (lane < lo + width)
        keep = k if keep is None else keep | k
    return jnp.where(keep, x, jnp.zeros_like(x))


def _low_half(n):
    return lax.broadcasted_iota(jnp.int32, (n, 128), 1) < HD


def _rms_halves(o, low, g):
    sq = o * o
    ms = jnp.where(low, jnp.sum(jnp.where(low, sq, 0.0), axis=-1, keepdims=True),
                   jnp.sum(jnp.where(low, 0.0, sq), axis=-1, keepdims=True)) / HD
    return o * lax.rsqrt(ms + EPS) * g


def _rope(x, c, sa, sb, shift):
    return x * c + pltpu.roll(x, 128 - shift, 1) * sa + pltpu.roll(x, shift, 1) * sb


def _mod_row(i):
    nct = T_CTX // TM
    per_b = N_LAT // TM
    return jnp.where(i >= nct, 1 + (i - nct) // per_b, 0)


def _mods_kernel(cond_ref, w_ref, b_ref, o_ref):
    a = _silu(cond_ref[...]).astype(BF16)
    o_ref[...] = _dot(a, w_ref[...].astype(BF16)) + b_ref[...]


def _mods(cond, w_ada, b_ada):
    tn = 1536
    n = 6 * D
    return pl.pallas_call(
        _mods_kernel,
        grid=(DEPTH, n // tn),
        in_specs=[pl.BlockSpec((8, D), lambda l, j: (0, 0)),
                  pl.BlockSpec((None, D, tn), lambda l, j: (l, 0, j)),
                  pl.BlockSpec((None, 1, tn), lambda l, j: (l, 0, j))],
        out_specs=pl.BlockSpec((None, 8, tn), lambda l, j: (l, 0, j)),
        out_shape=jax.ShapeDtypeStruct((DEPTH, 8, n), F32),
        compiler_params=_cparams(("arbitrary", "arbitrary")),
        name="mods",
    )(cond, w_ada, b_ada.reshape(DEPTH, 1, n))


def _in_proj_kernel(x_ref, mod_ref, g_ref, w_ref, qn_ref, kvn_ref, wuq_ref, wukv_ref,
                    pa_ref, pb_ref, pc_ref, pd_ref):
    m = mod_ref[...]
    h = _rms(x_ref[...], g_ref[...]) * (1.0 + m[1:2]) + m[0:1]
    p = _dot(h.astype(BF16), w_ref[...])
    pa_ref[...] = p[:, 0:512].astype(BF16)
    pb_ref[...] = p[:, 512:1280].astype(BF16)
    pd_ref[...] = p[:, C_D:PW].astype(BF16)
    ckv_n = _rms(p[:, C_CKV:C_CKV + 128], kvn_ref[...])
    z = p[:, C_CQ:C_CQ + 256]
    cqn = z * lax.rsqrt(jnp.sum(z * z, axis=-1, keepdims=True) / MLA_Q_RANK + EPS) * qn_ref[...]
    qc = _dot(cqn.astype(BF16), wuq_ref[...])
    kv = _dot(ckv_n.astype(BF16), wukv_ref[...])
    pc_ref[:, 0:384] = qc.astype(BF16)
    pc_ref[:, 384:512] = p[:, C_KPE:C_KPE + 128].astype(BF16)
    pc_ref[:, 512:1024] = kv.astype(BF16)
    pc_ref[:, 1024:1152] = ckv_n.astype(BF16)


def _in_proj(l, x, mods, g_pre, w_in_p, qn_p, kvn, wuq_p, wukv_p):
    row = lambda i: (i, 0)
    lyr2 = lambda i: (l, 0, 0)
    return pl.pallas_call(
        _in_proj_kernel,
        grid=(T // TM,),
        in_specs=[pl.BlockSpec((TM, D), row),
                  pl.BlockSpec((None, None, 6, D), lambda i: (l, _mod_row(i), 0, 0)),
                  pl.BlockSpec((None, 1, D), lyr2),
                  pl.BlockSpec((None, D, PW), lyr2),
                  pl.BlockSpec((None, 1, 256), lyr2),
                  pl.BlockSpec((None, 1, 128), lyr2),
                  pl.BlockSpec((None, 256, 384), lyr2),
                  pl.BlockSpec((None, 128, 512), lyr2)],
        out_specs=[pl.BlockSpec((TM, 512), row), pl.BlockSpec((TM, 768), row),
                   pl.BlockSpec((TM, PC_W), row), pl.BlockSpec((TM, 768), row)],
        out_shape=[jax.ShapeDtypeStruct((T, 512), BF16), jax.ShapeDtypeStruct((T, 768), BF16),
                   jax.ShapeDtypeStruct((T, PC_W), BF16), jax.ShapeDtypeStruct((T, 768), BF16)],
        compiler_params=_cparams(("arbitrary",)),
        name="in_proj",
    )(x, mods, g_pre, w_in_p, qn_p, kvn, wuq_p, wukv_p)


def _attn_ctx_kernel(sink_ref, lam_ref, pa, pb, pc, pd, subln_ref, o_ref,
                     ka_o, va_o, kb_o, vb_o, ckv_o, kpe_o, kd_o, vd_o):
    n = L_CTX
    low = _low_half(n)
    first = lax.broadcasted_iota(jnp.int32, (2 * n, 1), 0) < n
    ka, va = pa[:, 256:384], _with_ones(pa[:, 384:512])
    res = []
    for g in range(2):
        q = jnp.concatenate([_keep_lanes(pa[:, 0:128], (64 * g, 64)), _keep_lanes(pa[:, 128:256], (64 * g, 64))], axis=0)
        sink = jnp.where(first, sink_ref[2 * g], sink_ref[2 * g + 1])
        res.append(_attend(q, [ka], [va], sink))
    o_ref[:, 0:128] = jnp.where(low, res[0][0:n], res[1][0:n]).astype(BF16)
    o_ref[:, 128:256] = jnp.where(low, res[0][n:], res[1][n:]).astype(BF16)
    for g in range(2):
        ka_o[g] = pa[:, 256 + 64 * g:320 + 64 * g].astype(F32)
        va_o[g] = pa[:, 384 + 64 * g:448 + 64 * g].astype(F32)
    lam = lam_ref[0]
    out_scale = lam_ref[1]
    for j in range(2):
        qp = pb[:, 128 * j:128 * j + 128]
        q = jnp.concatenate([_keep_lanes(qp, (32 * i, 32)) for i in range(4)], axis=0)
        r = _attend(q, [pb[:, 256 + 128 * j:384 + 128 * j]], [_with_ones(pb[:, 512 + 128 * j:640 + 128 * j])])
        o = jnp.where(low, r[0:n] - lam * r[n:2 * n], r[2 * n:3 * n] - lam * r[3 * n:4 * n])
        o_ref[:, 256 + 128 * j:384 + 128 * j] = (_rms_halves(o, low, subln_ref[...]) * out_scale).astype(BF16)
    for h in range(4):
        kb_o[h] = pb[:, 256 + 64 * h:320 + 64 * h].astype(F32)
        vb_o[h] = pb[:, 512 + 64 * h:576 + 64 * h].astype(F32)
    kpe4 = pc[:, 384:512]
    for j in range(2):
        qn, kcat = pc[:, 128 * j:128 * j + 128], jnp.concatenate([pc[:, 512 + 128 * j:640 + 128 * j], kpe4], axis=1)
        qcat = jnp.concatenate([qn, pc[:, 256:384]], axis=1)
        q = jnp.concatenate([_keep_lanes(qcat, (64 * hh, 64), (128 + 32 * (2 * j + hh), 32)) for hh in range(2)], axis=0)
        r = _attend(q, [kcat], [_with_ones(pc[:, 768 + 128 * j:896 + 128 * j])])
        o_ref[:, 512 + 128 * j:640 + 128 * j] = jnp.where(low, r[0:n], r[n:]).astype(BF16)
    ckv_o[...] = pc[:, 1024:1152].astype(F32)
    kpe_o[...] = pc[:, 384:416].astype(F32)
    for j in range(2):
        qp = pd[:, 128 * j:128 * j + 128]
        q = jnp.concatenate([_keep_lanes(qp, (0, 64)), _keep_lanes(qp, (64, 64))], axis=0)
        r = _attend(q, [pd[:, 256 + 128 * j:384 + 128 * j]], [_with_ones(pd[:, 512 + 128 * j:640 + 128 * j])])
        o_ref[:, 768 + 128 * j:896 + 128 * j] = jnp.where(low, r[0:n], r[n:]).astype(BF16)
    for h in range(4):
        kd_o[h] = pd[:, 256 + 64 * h:320 + 64 * h].astype(F32)
        vd_o[h] = pd[:, 512 + 64 * h:576 + 64 * h].astype(F32)


def _attn_ctx(sink, lam, pa, pb, pc, pd, subln):
    row = lambda b: (b, 0)
    smem = pl.BlockSpec(memory_space=pltpu.SMEM)
    hd4 = lambda n: pl.BlockSpec((None, n, L_CTX, HD), lambda b: (b, 0, 0, 0))
    hshape = lambda n: jax.ShapeDtypeStruct((N_CTX_B, n, L_CTX, HD), F32)
    return pl.pallas_call(
        _attn_ctx_kernel,
        grid=(N_CTX_B,),
        in_specs=[smem, smem,
                  pl.BlockSpec((L_CTX, 512), row), pl.BlockSpec((L_CTX, 768), row),
                  pl.BlockSpec((L_CTX, PC_W), row), pl.BlockSpec((L_CTX, 768), row),
                  pl.BlockSpec((1, 128), lambda b: (0, 0))],
        out_specs=[pl.BlockSpec((L_CTX, D), row),
                   hd4(2), hd4(2), hd4(4), hd4(4),
                   pl.BlockSpec((None, L_CTX, 128), lambda b: (b, 0, 0)),
                   pl.BlockSpec((None, L_CTX, 32), lambda b: (b, 0, 0)),
                   hd4(4), hd4(4)],
        out_shape=[jax.ShapeDtypeStruct((T_CTX, D), BF16),
                   hshape(2), hshape(2), hshape(4), hshape(4),
                   jax.ShapeDtypeStruct((N_CTX_B, L_CTX, 128), F32),
                   jax.ShapeDtypeStruct((N_CTX_B, L_CTX, 32), F32),
                   hshape(4), hshape(4)],
        compiler_params=_cparams(("arbitrary",)),
        name="attn_ctx",
    )(sink, lam, pa, pb, pc, pd, subln)


_LAT_ROW = lambda b: (T_CTX // N_LAT + b, 0)


def _cache_spec(l, n):
    return pl.BlockSpec((None, None, n, L_CTX, HD), lambda b: (b, l, 0, 0, 0))


def _tab_spec():
    return pl.BlockSpec((N_LAT, 128), lambda b: (0, 0))


def _lat_win_kernel(sink_ref, pa, ck_ref, cv_ref, c_ref, sa_ref, sb_ref, o_ref, qs, kpad, vpad, kcs, vcs):
    c, sa, sb = c_ref[...], sa_ref[...], sb_ref[...]
    for j in range(2):
        qs[:, 128 * j:128 * j + 128] = _rope(pa[:, 128 * j:128 * j + 128].astype(F32), c, sa, sb, 16).astype(BF16)
    zeros = jnp.zeros((WINDOW, 128), BF16)
    kpad[0:WINDOW, :] = zeros
    kpad[WINDOW:WINDOW + N_LAT, :] = _rope(pa[:, 256:384].astype(F32), c, sa, sb, 16).astype(BF16)
    kpad[WINDOW + N_LAT:, :] = zeros
    vpad[0:WINDOW, :] = zeros
    vpad[WINDOW:WINDOW + N_LAT, :] = pa[:, 384:512]
    vpad[WINDOW + N_LAT:, :] = zeros
    for g in range(2):
        kcs[:, 64 * g:64 * g + 64] = ck_ref[g].astype(BF16)
        vcs[:, 64 * g:64 * g + 64] = cv_ref[g].astype(BF16)
    qi = lax.broadcasted_iota(jnp.int32, (2 * WINDOW, 3 * WINDOW), 0) % WINDOW
    kj = lax.broadcasted_iota(jnp.int32, (2 * WINDOW, 3 * WINDOW), 1)
    band = jnp.abs(kj - WINDOW - qi) <= WINDOW
    first_head = lax.broadcasted_iota(jnp.int32, (2 * WINDOW, 1), 0) < WINDOW
    low = _low_half(WINDOW)

    def body(nb, carry):
        r0 = pl.multiple_of(nb * WINDOW, WINDOW)
        kpos = r0 - WINDOW + kj
        valid = band & (kpos >= 0) & (kpos < N_LAT)
        kw = kpad[pl.ds(r0, 3 * WINDOW), :]
        vexts = [_with_ones(vpad[pl.ds(r0, 3 * WINDOW), :]), _with_ones(vcs[...])]
        res = []
        for g in range(2):
            q = jnp.concatenate([_keep_lanes(qs[pl.ds(r0, WINDOW), 0:128], (64 * g, 64)),
                                 _keep_lanes(qs[pl.ds(r0, WINDOW), 128:256], (64 * g, 64))], axis=0)
            s_loc = jnp.where(valid, _dot_t(q, kw), NEG_INF)
            sink = jnp.where(first_head, sink_ref[2 * g], sink_ref[2 * g + 1])
            res.append(_attend_scores([s_loc, _dot_t(q, kcs[...])], vexts, sink))
        o_ref[pl.ds(r0, WINDOW), 0:128] = jnp.where(low, res[0][0:WINDOW], res[1][0:WINDOW]).astype(BF16)
        o_ref[pl.ds(r0, WINDOW), 128:256] = jnp.where(low, res[0][WINDOW:], res[1][WINDOW:]).astype(BF16)
        return carry

    lax.fori_loop(0, N_LAT // WINDOW, body, 0)


def _lat_win(l, sink, pa, ck, cv, tabs):
    return pl.pallas_call(
        _lat_win_kernel,
        grid=(N_LAT_B,),
        in_specs=[pl.BlockSpec(memory_space=pltpu.SMEM),
                  pl.BlockSpec((N_LAT, 512), _LAT_ROW),
                  _cache_spec(l, 2), _cache_spec(l, 2),
                  _tab_spec(), _tab_spec(), _tab_spec()],
        out_specs=pl.BlockSpec((N_LAT, 256), lambda b: (b, 0)),
        out_shape=jax.ShapeDtypeStruct((T_LAT, 256), BF16),
        scratch_shapes=[pltpu.VMEM((N_LAT, 256), BF16),
                        pltpu.VMEM((N_LAT + 2 * WINDOW, 128), BF16),
                        pltpu.VMEM((N_LAT + 2 * WINDOW, 128), BF16),
                        pltpu.VMEM((L_CTX, 128), BF16),
                        pltpu.VMEM((L_CTX, 128), BF16)],
        compiler_params=_cparams(("arbitrary",)),
        name="lat_win",
    )(sink, pa, ck, cv, *tabs)


QT = 256
DIFF_QT = 128
N_KEYS = N_LAT + L_CTX


def _lat_diff_kernel(lam_ref, pb, ck_ref, cv_ref, subln_ref, c_ref, sa_ref, sb_ref, o_ref, qs, kall, vall):
    c, sa, sb = c_ref[...], sa_ref[...], sb_ref[...]
    for j in range(2):
        qs[:, 128 * j:128 * j + 128] = _rope(pb[:, 128 * j:128 * j + 128].astype(F32), c, sa, sb, 8).astype(BF16)
        kall[j, 0:N_LAT, :] = _rope(pb[:, 256 + 128 * j:384 + 128 * j].astype(F32), c, sa, sb, 8).astype(BF16)
        vall[j, 0:N_LAT, :] = pb[:, 512 + 128 * j:640 + 128 * j]
        for hh in range(2):
            kall[j, N_LAT:, 64 * hh:64 * hh + 64] = ck_ref[2 * j + hh].astype(BF16)
            vall[j, N_LAT:, 64 * hh:64 * hh + 64] = cv_ref[2 * j + hh].astype(BF16)
    lam = lam_ref[0]
    out_scale = lam_ref[1]
    low = _low_half(DIFF_QT)
    n = DIFF_QT
    for j in range(2):
        def body(t, carry, j=j):
            r0 = pl.multiple_of(t * n, n)
            qp = qs[pl.ds(r0, n), 128 * j:128 * j + 128]
            q = jnp.concatenate([_keep_lanes(qp, (32 * i, 32)) for i in range(4)], axis=0)
            r = _attend(q, [kall[j]], [_with_ones(vall[j])])
            o = jnp.where(low, r[0:n] - lam * r[n:2 * n], r[2 * n:3 * n] - lam * r[3 * n:4 * n])
            o_ref[pl.ds(r0, n), 128 * j:128 * j + 128] = (_rms_halves(o, low, subln_ref[...]) * out_scale).astype(BF16)
            return carry

        lax.fori_loop(0, N_LAT // n, body, 0)


def _lat_diff(l, lam, pb, ck, cv, subln, tabs):
    return pl.pallas_call(
        _lat_diff_kernel,
        grid=(N_LAT_B,),
        in_specs=[pl.BlockSpec(memory_space=pltpu.SMEM),
                  pl.BlockSpec((N_LAT, 768), _LAT_ROW),
                  _cache_spec(l, 4), _cache_spec(l, 4),
                  pl.BlockSpec((1, 128), lambda b: (0, 0)),
                  _tab_spec(), _tab_spec(), _tab_spec()],
        out_specs=pl.BlockSpec((N_LAT, 256), lambda b: (b, 0)),
        out_shape=jax.ShapeDtypeStruct((T_LAT, 256), BF16),
        scratch_shapes=[pltpu.VMEM((N_LAT, 256), BF16),
                        pltpu.VMEM((2, N_KEYS, 128), BF16),
                        pltpu.VMEM((2, N_KEYS, 128), BF16)],
        compiler_params=_cparams(("arbitrary",)),
        name="lat_diff",
    )(lam, pb, ck, cv, subln, *tabs)


def _lat_mla_kernel(pc, ckv_ref, kpe_ref, wukv_ref, c_ref, sa_ref, sb_ref, o_ref, qcat, kcat, vall):
    c, sa, sb = c_ref[...], sa_ref[...], sb_ref[...]
    qr = _rope(pc[:, 256:384].astype(F32), c, sa, sb, 8).astype(BF16)
    kpe_l = _rope(pc[:, 384:512].astype(F32), c, sa, sb, 8).astype(BF16)
    kvc = _dot(ckv_ref[...].astype(BF16), wukv_ref[...]).astype(BF16)
    kpe_c = kpe_ref[...].astype(BF16)
    for j in range(2):
        qcat[j, :, 0:128] = pc[:, 128 * j:128 * j + 128]
        qcat[j, :, 128:256] = qr
        kcat[j, 0:N_LAT, 0:128] = pc[:, 512 + 128 * j:640 + 128 * j]
        kcat[j, 0:N_LAT, 128:256] = kpe_l
        kcat[j, N_LAT:, 0:128] = kvc[:, 128 * j:128 * j + 128]
        for i in range(4):
            kcat[j, N_LAT:, 128 + 32 * i:160 + 32 * i] = kpe_c
        vall[j, 0:N_LAT, :] = pc[:, 768 + 128 * j:896 + 128 * j]
        vall[j, N_LAT:, :] = kvc[:, 256 + 128 * j:384 + 128 * j]
    low = _low_half(QT)
    for j in range(2):
        def body(t, carry, j=j):
            r0 = pl.multiple_of(t * QT, QT)
            q2 = qcat[j, pl.ds(r0, QT), :]
            q = jnp.concatenate([_keep_lanes(q2, (64 * hh, 64), (128 + 32 * (2 * j + hh), 32)) for hh in range(2)],
                                axis=0)
            r = _attend(q, [kcat[j]], [_with_ones(vall[j])])
            o_ref[pl.ds(r0, QT), 128 * j:128 * j + 128] = jnp.where(low, r[0:QT], r[QT:]).astype(BF16)
            return carry

        lax.fori_loop(0, N_LAT // QT, body, 0)


def _lat_mla(l, pc, ckv, kpe, wukv_p, tabs):
    return pl.pallas_call(
        _lat_mla_kernel,
        grid=(N_LAT_B,),
        in_specs=[pl.BlockSpec((N_LAT, PC_W), _LAT_ROW),
                  pl.BlockSpec((None, None, L_CTX, 128), lambda b: (b, l, 0, 0)),
                  pl.BlockSpec((None, None, L_CTX, 32), lambda b: (b, l, 0, 0)),
                  pl.BlockSpec((None, 128, 512), lambda b: (l, 0, 0)),
                  _tab_spec(), _tab_spec(), _tab_spec()],
        out_specs=pl.BlockSpec((N_LAT, 256), lambda b: (b, 0)),
        out_shape=jax.ShapeDtypeStruct((T_LAT, 256), BF16),
        scratch_shapes=[pltpu.VMEM((2, N_LAT, 256), BF16),
                        pltpu.VMEM((2, N_KEYS, 256), BF16),
                        pltpu.VMEM((2, N_KEYS, 128), BF16)],
        compiler_params=_cparams(("arbitrary",)),
        name="lat_mla",
    )(pc, ckv, kpe, wukv_p, *tabs)


NA_ROWS = 8
NA_KEYS = NA_ROWS * GRID_W


def _lat_na_kernel(pd, ck_ref, cv_ref, bias_ref, o_ref, kcs, vcs):
    n_rows = N_LAT // GRID_W
    for h in range(4):
        kcs[h // 2, :, 64 * (h % 2):64 * (h % 2) + 64] = ck_ref[h].astype(BF16)
        vcs[h // 2, :, 64 * (h % 2):64 * (h % 2) + 64] = cv_ref[h].astype(BF16)
    low = _low_half(GRID_W)

    def body(r, carry):
        rs = jnp.clip(r - NA_ROWS // 2, 0, n_rows - NA_ROWS)
        q0 = pl.multiple_of(r * GRID_W, GRID_W)
        k0 = pl.multiple_of(rs * GRID_W, GRID_W)
        for j in range(2):
            qp = pd[pl.ds(q0, GRID_W), 128 * j:128 * j + 128]
            q = jnp.concatenate([_keep_lanes(qp, (0, 64)), _keep_lanes(qp, (64, 64))], axis=0)
            s_loc = _dot_t(q, pd[pl.ds(k0, NA_KEYS), 256 + 128 * j:384 + 128 * j]) + bias_ref[j, r - rs]
            s_ctx = _dot_t(q, kcs[j])
            res = _attend_scores([s_loc, s_ctx],
                                 [_with_ones(pd[pl.ds(k0, NA_KEYS), 512 + 128 * j:640 + 128 * j]), _with_ones(vcs[j])])
            o_ref[pl.ds(q0, GRID_W), 128 * j:128 * j + 128] = jnp.where(low, res[0:GRID_W], res[GRID_W:]).astype(BF16)
        return carry

    lax.fori_loop(0, n_rows, body, 0)


def _lat_na(l, pd, ck, cv, bias):
    return pl.pallas_call(
        _lat_na_kernel,
        grid=(N_LAT_B,),
        in_specs=[pl.BlockSpec((N_LAT, 768), _LAT_ROW),
                  _cache_spec(l, 4), _cache_spec(l, 4),
                  pl.BlockSpec((None, 2, NA_ROWS, 2 * GRID_W, NA_KEYS), lambda b: (l, 0, 0, 0, 0))],
        out_specs=pl.BlockSpec((N_LAT, 256), lambda b: (b, 0)),
        out_shape=jax.ShapeDtypeStruct((T_LAT, 256), BF16),
        scratch_shapes=[pltpu.VMEM((2, L_CTX, 128), BF16), pltpu.VMEM((2, L_CTX, 128), BF16)],
        compiler_params=_cparams(("arbitrary",)),
        name="lat_na",
    )(pd, ck, cv, bias)


def _out_proj_kernel(with_router, *refs):
    if with_router:
        (oc_ref, oa_ref, ob_ref, om_ref, od_ref, x_ref, mod_ref, wo_ref, gpost_ref, gffn_ref, r_ref,
         x1_ref, h2_ref, route_ref, cnt_ref) = refs
    else:
        (oc_ref, oa_ref, ob_ref, om_ref, od_ref, x_ref, mod_ref, wo_ref, gpost_ref, gffn_ref,
         x1_ref, h2_ref) = refs
    i = pl.program_id(0)
    nct = T_CTX // TM

    if with_router:
        @pl.when(i == 0)
        def _():
            cnt_ref[...] = jnp.zeros(cnt_ref.shape, F32)

    def finish(y):
        m = mod_ref[...]
        x1 = x_ref[...] + m[2:3] * _rms(y, gpost_ref[...])
        x1_ref[...] = x1
        h2 = _rms(x1, gffn_ref[...]) * (1.0 + m[4:5]) + m[3:4]
        h2_ref[...] = h2.astype(h2_ref.dtype)
        if with_router:
            lg = _dot(h2.astype(BF16), r_ref[...])
            lane = lax.broadcasted_iota(jnp.int32, lg.shape, 1)
            lane_f = lane.astype(F32)
            lg = jnp.where(lane < N_EXPERTS, lg, -jnp.inf)
            m1 = jnp.max(lg, axis=-1, keepdims=True)
            i1 = jnp.min(jnp.where(lg == m1, lane_f, 128.0), axis=-1, keepdims=True)
            lg2 = jnp.where(lane_f == i1, -jnp.inf, lg)
            m2 = jnp.max(lg2, axis=-1, keepdims=True)
            i2 = jnp.min(jnp.where(lg2 == m2, lane_f, 128.0), axis=-1, keepdims=True)
            e = jnp.exp(m2 - m1)
            g1 = 1.0 / (1.0 + e)
            g2 = e / (1.0 + e)
            sel1 = lane_f == i1
            sel2 = lane_f == i2
            picked = jnp.where(sel1 | sel2, 1.0, 0.0)
            ri = lax.broadcasted_iota(jnp.int32, (TM, TM), 0)
            ci = lax.broadcasted_iota(jnp.int32, (TM, TM), 1)
            before = _dot(jnp.where(ci < ri, 1.0, 0.0).astype(BF16), picked.astype(BF16)) + cnt_ref[0:1, :]
            r1 = jnp.sum(jnp.where(sel1, before, 0.0), axis=-1, keepdims=True)
            r2 = jnp.sum(jnp.where(sel2, before, 0.0), axis=-1, keepdims=True)
            cnt_ref[...] = cnt_ref[...] + jnp.sum(picked, axis=0, keepdims=True)
            route = jnp.where(lane == 0, i1, jnp.where(lane == 1, i2, jnp.where(lane == 2, g1, 0.0)))
            route_ref[...] = jnp.where(lane == 3, g2, jnp.where(lane == 4, r1, jnp.where(lane == 5, r2, route)))

    @pl.when(i < nct)
    def _():
        finish(_dot(oc_ref[...], wo_ref[...]))

    @pl.when(i >= nct)
    def _():
        y = _dot(oa_ref[...], wo_ref[0:256, :])
        y = y + _dot(ob_ref[...], wo_ref[256:512, :])
        y = y + _dot(om_ref[...], wo_ref[512:768, :])
        y = y + _dot(od_ref[...], wo_ref[768:1024, :])
        finish(y)


def _out_proj(l, o_ctx, o_lat, x, mods, w_o, g_post, g_ffn, router_p):
    nct = T_CTX // TM
    row = lambda i: (i, 0)
    lyr2 = lambda i: (l, 0, 0)
    lat = lambda i: (jnp.maximum(i - nct, 0), 0)
    with_router = router_p is not None
    in_specs = [pl.BlockSpec((TM, D), lambda i: (jnp.minimum(i, nct - 1), 0)),
                pl.BlockSpec((TM, 256), lat), pl.BlockSpec((TM, 256), lat),
                pl.BlockSpec((TM, 256), lat), pl.BlockSpec((TM, 256), lat),
                pl.BlockSpec((TM, D), row),
                pl.BlockSpec((None, None, 6, D), lambda i: (l, _mod_row(i), 0, 0)),
                pl.BlockSpec((None, D, D), lyr2),
                pl.BlockSpec((None, 1, D), lyr2),
                pl.BlockSpec((None, 1, D), lyr2)]
    out_specs = [pl.BlockSpec((TM, D), row), pl.BlockSpec((TM, D), row)]
    out_shape = [jax.ShapeDtypeStruct((T, D), F32), jax.ShapeDtypeStruct((T, D), F32 if with_router else BF16)]
    args = [o_ctx, *o_lat, x, mods, w_o, g_post, g_ffn]
    if with_router:
        in_specs.append(pl.BlockSpec((D, 128), lambda i: (0, 0)))
        out_specs += [pl.BlockSpec((TM, 128), row), pl.BlockSpec((8, 128), lambda i: (0, 0))]
        out_shape += [jax.ShapeDtypeStruct((T, 128), F32), jax.ShapeDtypeStruct((8, 128), F32)]
        args.append(router_p)
    return pl.pallas_call(
        functools.partial(_out_proj_kernel, with_router),
        grid=(T // TM,),
        in_specs=in_specs, out_specs=out_specs, out_shape=out_shape,
        compiler_params=_cparams(("arbitrary",)),
        name="out_proj_router" if with_router else "out_proj",
    )(*args)


FFN_TM = 1024


def _ffn_kernel(h_ref, x_ref, mod_ref, g_ref, wg_ref, wu_ref, wd_ref, o_ref, acc_ref):
    c = pl.program_id(1)
    h = h_ref[...]
    t = (_silu(_dot(h, wg_ref[...])) * _dot(h, wu_ref[...])).astype(BF16)
    part = _dot(t, wd_ref[...])

    @pl.when(c == 0)
    def _():
        acc_ref[...] = part

    @pl.when(c > 0)
    def _():
        acc_ref[...] += part

    @pl.when(c == pl.num_programs(1) - 1)
    def _():
        m = mod_ref[...]
        o_ref[...] = x_ref[...] + m[5:6] * _rms(acc_ref[...], g_ref[...])


def _ffn_mod_row(i):
    nct = T_CTX // FFN_TM
    return jnp.where(i >= nct, 1 + (i - nct) // (N_LAT // FFN_TM), 0)


def _ffn(l, i_dense, h2, x1, mods, g_post, wg, wu, wd):
    row = lambda i, c: (i, 0)
    return pl.pallas_call(
        _ffn_kernel,
        grid=(T // FFN_TM, D_FF // FFN_CHUNK),
        in_specs=[pl.BlockSpec((FFN_TM, D), row),
                  pl.BlockSpec((FFN_TM, D), row),
                  pl.BlockSpec((None, None, 6, D), lambda i, c: (l, _ffn_mod_row(i), 0, 0)),
                  pl.BlockSpec((None, 1, D), lambda i, c: (l, 0, 0)),
                  pl.BlockSpec((None, D, FFN_CHUNK), lambda i, c: (i_dense, 0, c)),
                  pl.BlockSpec((None, D, FFN_CHUNK), lambda i, c: (i_dense, 0, c)),
                  pl.BlockSpec((None, FFN_CHUNK, D), lambda i, c: (i_dense, c, 0))],
        out_specs=pl.BlockSpec((FFN_TM, D), row),
        out_shape=jax.ShapeDtypeStruct((T, D), F32),
        scratch_shapes=[pltpu.VMEM((FFN_TM, D), F32)],
        compiler_params=_cparams(("arbitrary", "arbitrary")),
        name="ffn",
    )(h2, x1, mods, g_post, wg, wu, wd)


def _dispatch_kernel(slot_ref, h_ref, xs_hbm, sem):
    def row_copy(t, k):
        return pltpu.make_async_copy(h_ref.at[pl.ds(t, 1)], xs_hbm.at[pl.ds(slot_ref[0, 2 * t + k], 1)], sem)

    def issue(t, carry):
        row_copy(t, 0).start()
        row_copy(t, 1).start()
        return carry

    def drain(t, carry):
        row_copy(t, 0).wait()
        row_copy(t, 1).wait()
        return carry

    lax.fori_loop(0, DISPATCH_TM, issue, 0, unroll=8)
    lax.fori_loop(0, DISPATCH_TM, drain, 0, unroll=8)


def _dispatch(slots, h2):
    return pl.pallas_call(
        _dispatch_kernel,
        grid=(T // DISPATCH_TM,),
        in_specs=[pl.BlockSpec((None, 1, 2 * DISPATCH_TM), lambda i: (i, 0, 0), memory_space=pltpu.SMEM),
                  pl.BlockSpec((DISPATCH_TM, D), lambda i: (i, 0))],
        out_specs=pl.BlockSpec(memory_space=pl.ANY),
        out_shape=jax.ShapeDtypeStruct((N_EXPERTS * EXPERT_CAP, D), F32),
        scratch_shapes=[pltpu.SemaphoreType.DMA(())],
        compiler_params=_cparams(("arbitrary",)),
        name="moe_dispatch",
    )(slots.reshape(T // DISPATCH_TM, 1, 2 * DISPATCH_TM), h2)


def _moe_kernel(ie_ref, ib_ref, ns_ref, nr_ref, x_ref, wg_ref, wu_ref, wd_ref, o_ref, wgb, wub, wdb):
    i = pl.program_id(0)
    c = pl.program_id(1)
    ns = ns_ref[i]
    n_rows = nr_ref[i]

    @pl.when(ns > 0)
    def _():
        @pl.when(c == 0)
        def _():
            o_ref[...] = jnp.zeros(o_ref.shape, F32)

        wgb[...] = wg_ref[...].astype(BF16)
        wub[...] = wu_ref[...].astype(BF16)
        wdb[...] = wd_ref[...].astype(BF16)

        def body(s, carry):
            r0 = pl.multiple_of(s * SUB_ROWS, SUB_ROWS)
            live = r0 + lax.broadcasted_iota(jnp.int32, (SUB_ROWS, 1), 0) < n_rows
            xs = jnp.where(live, x_ref[pl.ds(r0, SUB_ROWS), :], 0.0).astype(BF16)
            t = (_silu(_dot(xs, wgb[...])) * _dot(xs, wub[...])).astype(BF16)
            o_ref[pl.ds(r0, SUB_ROWS), :] += _dot(t, wdb[...])
            return carry

        lax.fori_loop(0, ns, body, 0)


def _moe(i_moe, item_e, item_blk, item_ns, item_rows, x_sorted, wg, wu, wd):
    nc = D_FF_EXPERT // FF_CHUNK
    chunk = lambda i, c, ns: jnp.where(ns[i] > 0, c, nc - 1)
    grid_spec = pltpu.PrefetchScalarGridSpec(
        num_scalar_prefetch=4,
        grid=(N_ITEMS, nc),
        in_specs=[pl.BlockSpec((ITEM_ROWS, D), lambda i, c, ie, ib, ns, nr: (ib[i], 0)),
                  pl.BlockSpec((None, None, D, FF_CHUNK),
                               lambda i, c, ie, ib, ns, nr: (i_moe, ie[i], 0, chunk(i, c, ns))),
                  pl.BlockSpec((None, None, D, FF_CHUNK),
                               lambda i, c, ie, ib, ns, nr: (i_moe, ie[i], 0, chunk(i, c, ns))),
                  pl.BlockSpec((None, None, FF_CHUNK, D),
                               lambda i, c, ie, ib, ns, nr: (i_moe, ie[i], chunk(i, c, ns), 0))],
        out_specs=pl.BlockSpec((ITEM_ROWS, D), lambda i, c, ie, ib, ns, nr: (ib[i], 0)),
        scratch_shapes=[pltpu.VMEM((D, FF_CHUNK), BF16), pltpu.VMEM((D, FF_CHUNK), BF16),
                        pltpu.VMEM((FF_CHUNK, D), BF16)],
    )
    return pl.pallas_call(
        _moe_kernel,
        grid_spec=grid_spec,
        out_shape=jax.ShapeDtypeStruct((N_EXPERTS * EXPERT_CAP, D), F32),
        compiler_params=_cparams(("arbitrary", "arbitrary")),
        name="moe_experts",
    )(item_e, item_blk, item_ns, item_rows, x_sorted, wg, wu, wd)


def _post_kernel(slot_ref, ys_hbm, route_ref, x_ref, mod_ref, g_ref, o_ref, buf, sem):
    def row_copy(t, k):
        return pltpu.make_async_copy(ys_hbm.at[pl.ds(slot_ref[0, 2 * t + k], 1)], buf.at[k, pl.ds(t, 1)], sem)

    def issue(t, carry):
        row_copy(t, 0).start()
        row_copy(t, 1).start()
        return carry

    def drain(t, carry):
        row_copy(t, 0).wait()
        row_copy(t, 1).wait()
        return carry

    lax.fori_loop(0, TM, issue, 0, unroll=8)
    lax.fori_loop(0, TM, drain, 0, unroll=8)
    r = route_ref[...]
    y = r[:, 2:3] * buf[0] + r[:, 3:4] * buf[1]
    m = mod_ref[...]
    o_ref[...] = x_ref[...] + m[5:6] * _rms(y, g_ref[...])


def _post(l, slots, y_sorted, route, x1, mods, g_post):
    row = lambda i: (i, 0)
    return pl.pallas_call(
        _post_kernel,
        grid=(T // TM,),
        in_specs=[pl.BlockSpec((None, 1, 2 * TM), lambda i: (i, 0, 0), memory_space=pltpu.SMEM),
                  pl.BlockSpec(memory_space=pl.ANY),
                  pl.BlockSpec((TM, 128), row), pl.BlockSpec((TM, D), row),
                  pl.BlockSpec((None, None, 6, D), lambda i: (l, _mod_row(i), 0, 0)),
                  pl.BlockSpec((None, 1, D), lambda i: (l, 0, 0))],
        out_specs=pl.BlockSpec((TM, D), row),
        out_shape=jax.ShapeDtypeStruct((T, D), F32),
        scratch_shapes=[pltpu.VMEM((2, TM, D), F32), pltpu.SemaphoreType.DMA(())],
        compiler_params=_cparams(("arbitrary",)),
        name="moe_post",
    )(slots.reshape(T // TM, 1, 2 * TM), y_sorted, route, x1, mods, g_post)


def _route_plan(route, counts):
    i32 = jnp.int32
    slots = (route[:, 0:2].astype(i32) * EXPERT_CAP + route[:, 4:6].astype(i32)).reshape(-1)
    n = counts[0, :N_EXPERTS].astype(i32)
    n_items = (n + ITEM_ROWS - 1) // ITEM_ROWS
    ends = jnp.cumsum(n_items)
    base = ends - n_items
    total = ends[-1]
    ii = jnp.arange(N_ITEMS, dtype=i32)
    ii_c = jnp.minimum(ii, total - 1)
    item_e = jnp.sum((ends[None, :] <= ii_c[:, None]).astype(i32), axis=1)
    j = ii_c - base[item_e]
    item_blk = item_e * (EXPERT_CAP // ITEM_ROWS) + j
    rows = jnp.where(ii < total, jnp.clip(n[item_e] - j * ITEM_ROWS, 0, ITEM_ROWS), 0)
    item_ns = (rows + SUB_ROWS - 1) // SUB_ROWS
    return slots, item_e.astype(i32), item_blk.astype(i32), item_ns.astype(i32), rows.astype(i32)


def _rope_tables(d):
    half, quarter = d // 2, d // 4
    t = jnp.arange(N_LAT)
    pr, pc = t // GRID_W, t % GRID_W
    i = jnp.arange(128) % d
    j = (i % half) % quarter
    inv = ROPE_THETA ** (-(2 * j).astype(F32) / half)
    pos = jnp.where((i < half)[None, :], pr[:, None], pc[:, None]).astype(F32)
    ang = pos * inv[None, :]
    low = ((i % half) < quarter)[None, :]
    sin = jnp.sin(ang)
    return jnp.cos(ang), jnp.where(low, -sin, 0.0), jnp.where(low, 0.0, sin)


def _na_bias_table(rpb):
    q = jnp.arange(GRID_W)
    k = jnp.arange(GRID_W)
    rel_c = jnp.clip(k[None, :] - q[:, None], -15, 15) + 15
    cs = jnp.clip(q - 8, 0, GRID_W - 16)
    valid = (k[None, :] >= cs[:, None]) & (k[None, :] < cs[:, None] + 16)
    pick = rel_c[:, :, None] == jnp.arange(31)[None, None, :]
    band = jnp.sum(jnp.where(pick[None, None, None], rpb[:, :, :, None, None, :], 0.0), axis=-1)
    band = jnp.where(valid[None, None, None], band, NEG_INF)
    tab = jnp.stack([band[:, :, NA_ROWS - 1 - dd:2 * NA_ROWS - 1 - dd] for dd in range(NA_ROWS)], axis=2)
    tab = tab.transpose(0, 1, 2, 4, 3, 5)
    tab = tab.reshape(DEPTH, 2, 2, NA_ROWS, GRID_W, NA_KEYS).transpose(0, 1, 3, 2, 4, 5)
    return tab.reshape(DEPTH, 2, NA_ROWS, 2 * GRID_W, NA_KEYS)


def _pad_in_proj(w_in):
    z = lambda n: jnp.zeros((DEPTH, D, n), w_in.dtype)
    s64, s32 = HD ** -0.5, (HD // 2) ** -0.5
    qa = w_in[:, :, 0:256] * s64
    kpe = w_in[:, :, 1600:1632]
    return jnp.concatenate([qa[:, :, 0:64], qa[:, :, 128:192], qa[:, :, 64:128], qa[:, :, 192:256],
                            w_in[:, :, 256:512],
                            w_in[:, :, 512:768] * s32, w_in[:, :, 768:1280],
                            w_in[:, :, 1472:1600], kpe, kpe, kpe, kpe,
                            w_in[:, :, 1280:1472], z(64),
                            w_in[:, :, 1632:1888] * s64, w_in[:, :, 1888:2400]], axis=-1).astype(BF16)


def _split_heads_cols(w, first, second):
    lead = w.shape[:-1]
    w4 = w.reshape(*lead, 4, first + second)
    return jnp.concatenate([w4[..., :first].reshape(*lead, 4 * first),
                            w4[..., first:].reshape(*lead, 4 * second)], axis=-1)


def kernel(x_prompt, x_sample, cache_win_k, cache_win_v, cache_diff_k, cache_diff_v, cache_mla_ckv, cache_mla_kpe, cache_na_k, cache_na_v, c, c_ctx, w_ada, b_ada, norm_mix_pre, norm_mix_post, norm_ffn_pre, norm_ffn_post, w_in, w_o, win_sink, diff_lq1, diff_lk1, diff_lq2, diff_lk2, diff_subln, mla_q_norm, mla_kv_norm, mla_wuq, mla_wukv, na_rpb, ffn_w_gate, ffn_w_up, ffn_w_down, moe_router, moe_w_gate, moe_w_up, moe_w_down):
    w_in_p = _pad_in_proj(w_in)
    wuq_p = jnp.pad(_split_heads_cols(mla_wuq, 64, 32) * (HD + HD // 2) ** -0.5,
                    ((0, 0), (0, 256 - MLA_Q_RANK), (0, 0))).astype(BF16)
    wukv_p = _split_heads_cols(mla_wukv, 64, 64).astype(BF16)
    qn_p = jnp.pad(mla_q_norm, ((0, 0), (0, 256 - MLA_Q_RANK))).reshape(DEPTH, 1, 256)
    kvn = mla_kv_norm.reshape(DEPTH, 1, 128)
    w_o_b = jnp.concatenate([w_o[:, 0:64], w_o[:, 128:192], w_o[:, 64:128], w_o[:, 192:]], axis=1).astype(BF16)
    ffn_wg, ffn_wu, ffn_wd = ffn_w_gate.astype(BF16), ffn_w_up.astype(BF16), ffn_w_down.astype(BF16)
    router_p = jnp.pad(moe_router, ((0, 0), (0, 0), (0, 128 - N_EXPERTS))).astype(BF16)
    g_mix_pre = norm_mix_pre.reshape(DEPTH, 1, D)
    g_mix_post = norm_mix_post.reshape(DEPTH, 1, D)
    g_ffn_pre = norm_ffn_pre.reshape(DEPTH, 1, D)
    g_ffn_post = norm_ffn_post.reshape(DEPTH, 1, D)
    tabs64 = _rope_tables(64)
    tabs32 = _rope_tables(32)
    na_bias = _na_bias_table(na_rpb)
    lam_init = jnp.asarray([0.8 - 0.6 * math.exp(-0.3 * l) for l in range(DEPTH)], F32)
    lam = (jnp.exp(jnp.sum(diff_lq1 * diff_lk1, axis=-1)) - jnp.exp(jnp.sum(diff_lq2 * diff_lk2, axis=-1))
           + lam_init)
    lam_tab = jnp.stack([lam, 1.0 - lam_init], axis=-1)
    subln = jnp.tile(diff_subln, (1, 2)).reshape(DEPTH, 1, 2 * HD)

    cond = jnp.concatenate([c_ctx[None], c, jnp.zeros((3, D), F32)], axis=0)
    mods = _mods(cond, w_ada, b_ada).reshape(DEPTH, 8, 6, D)

    x = jnp.concatenate([x_prompt.reshape(T_CTX, D), x_sample.reshape(T_LAT, D)], axis=0)
    new_caches = [[] for _ in range(8)]
    for l in range(DEPTH):
        pa, pb, pc, pd = _in_proj(l, x, mods, g_mix_pre, w_in_p, qn_p, kvn, wuq_p, wukv_p)
        ctx_out = _attn_ctx(win_sink[l], lam_tab[l], pa, pb, pc, pd, subln[l])
        o_ctx = ctx_out[0]
        for j, t in enumerate(ctx_out[1:]):
            new_caches[j].append(t)
        o_lat = [_lat_win(l, win_sink[l], pa, cache_win_k, cache_win_v, tabs64),
                 _lat_diff(l, lam_tab[l], pb, cache_diff_k, cache_diff_v, subln[l], tabs32),
                 _lat_mla(l, pc, cache_mla_ckv, cache_mla_kpe, wukv_p, tabs32),
                 _lat_na(l, pd, cache_na_k, cache_na_v, na_bias)]
        if l % 2 == 0:
            x1, h2 = _out_proj(l, o_ctx, o_lat, x, mods, w_o_b, g_mix_post, g_ffn_pre, None)
            x = _ffn(l, l // 2, h2, x1, mods, g_ffn_post, ffn_wg, ffn_wu, ffn_wd)
        else:
            x1, h2, route, counts = _out_proj(l, o_ctx, o_lat, x, mods, w_o_b, g_mix_post, g_ffn_pre,
                                              router_p[l // 2])
            slots, item_e, item_blk, item_ns, item_rows = _route_plan(route, counts)
            x_sorted = _dispatch(slots, h2)
            y_sorted = _moe(l // 2, item_e, item_blk, item_ns, item_rows, x_sorted,
                            moe_w_gate, moe_w_up, moe_w_down)
            x = _post(l, slots, y_sorted, route, x1, mods, g_ffn_post)

    y_prompt = x[:T_CTX].reshape(N_CTX_B, L_CTX, D)
    y_sample = x[T_CTX:].reshape(N_LAT_B, N_LAT, D)
    return (y_prompt, y_sample) + tuple(jnp.stack(ts, axis=1) for ts in new_caches)
```

```python
import functools
import math

import jax
import jax.numpy as jnp
from jax import lax
from jax.experimental import pallas as pl
from jax.experimental.pallas import tpu as pltpu

F32 = jnp.float32
BF16 = jnp.bfloat16

D = 1024
DEPTH = 4
N_CTX_B, L_CTX = 16, 256
N_LAT_B, N_LAT = 4, 1024
T_CTX = N_CTX_B * L_CTX
T_LAT = N_LAT_B * N_LAT
T = T_CTX + T_LAT
HD = 64
GRID_W = 64
WINDOW = 128
EPS = 1e-6
NEG_INF = -1e30
ROPE_THETA = 10000.0
MLA_Q_RANK = 192
D_FF = 2816
N_EXPERTS = 8
D_FF_EXPERT = 3584

PW = 2560
C_CKV, C_KPE, C_CQ, C_D = 1280, 1408, 1536, 1792
PC_W = 1152

TM = 512
EXPERT_CAP = T
ITEM_ROWS = 1024
SUB_ROWS = 512
TAIL_ROWS = 256
N_ITEMS = N_EXPERTS + (2 * T) // ITEM_ROWS
FF_CHUNK = 896
DISPATCH_TM = 1024
FFN_CHUNK = 1408
VMEM_LIMIT = 56 * 1024 * 1024


def _cparams(sem):
    return pltpu.CompilerParams(dimension_semantics=sem, vmem_limit_bytes=VMEM_LIMIT)


def _rms(x, g):
    return x * lax.rsqrt(jnp.mean(x * x, axis=-1, keepdims=True) + EPS) * g


def _silu(a):
    return a / (1.0 + jnp.exp(-a))


def _dot(a, b):
    return jnp.dot(a, b, preferred_element_type=F32)


def _dot_t(a, b):
    return lax.dot_general(a, b, (((1,), (1,)), ((), ())), preferred_element_type=F32)


def _with_ones(v):
    return jnp.concatenate([v, jnp.ones(v.shape, v.dtype)], axis=1)


def _attend_scores(scores, vexts, extra_logit=None):
    mx = jnp.max(scores[0], axis=-1, keepdims=True)
    for s in scores[1:]:
        mx = jnp.maximum(mx, jnp.max(s, axis=-1, keepdims=True))
    if extra_logit is not None:
        mx = jnp.maximum(mx, extra_logit)
    r = None
    for s, v in zip(scores, vexts):
        part = _dot(jnp.exp(s - mx).astype(BF16), v)
        r = part if r is None else r + part
    den = r[:, 128:256]
    if extra_logit is not None:
        den = den + jnp.exp(extra_logit - mx)
    return r[:, 0:128] / den


def _attend(q, keys, vexts, extra_logit=None):
    return _attend_scores([_dot_t(q, k) for k in keys], vexts, extra_logit)


def _keep_lanes(x, *ranges):
    lane = lax.broadcasted_iota(jnp.int32, x.shape, 1)
    keep = None
    for lo, width in ranges:
        k = (lane >= lo) & (lane < lo + width)
        keep = k if keep is None else keep | k
    return jnp.where(keep, x, jnp.zeros_like(x))


def _low_half(n):
    return lax.broadcasted_iota(jnp.int32, (n, 128), 1) < HD


def _rms_halves(o, low, g):
    sq = o * o
    ms = jnp.where(low, jnp.sum(jnp.where(low, sq, 0.0), axis=-1, keepdims=True),
                   jnp.sum(jnp.where(low, 0.0, sq), axis=-1, keepdims=True)) / HD
    return o * lax.rsqrt(ms + EPS) * g


def _rope(x, c, sa, sb, shift):
    return x * c + pltpu.roll(x, 128 - shift, 1) * sa + pltpu.roll(x, shift, 1) * sb


def _mod_row(i):
    nct = T_CTX // TM
    per_b = N_LAT // TM
    return jnp.where(i >= nct, 1 + (i - nct) // per_b, 0)


def _mods_kernel(cond_ref, w_ref, b_ref, o_ref):
    a = _silu(cond_ref[...]).astype(BF16)
    o_ref[...] = _dot(a, w_ref[...].astype(BF16)) + b_ref[...]


def _mods(cond, w_ada, b_ada):
    tn = 1536
    n = 6 * D
    return pl.pallas_call(
        _mods_kernel,
        grid=(DEPTH, n // tn),
        in_specs=[pl.BlockSpec((8, D), lambda l, j: (0, 0)),
                  pl.BlockSpec((None, D, tn), lambda l, j: (l, 0, j)),
                  pl.BlockSpec((None, 1, tn), lambda l, j: (l, 0, j))],
        out_specs=pl.BlockSpec((None, 8, tn), lambda l, j: (l, 0, j)),
        out_shape=jax.ShapeDtypeStruct((DEPTH, 8, n), F32),
        compiler_params=_cparams(("arbitrary", "arbitrary")),
        name="mods",
    )(cond, w_ada, b_ada.reshape(DEPTH, 1, n))


def _x_specs(xs):
    if len(xs) == 1:
        return [pl.BlockSpec((TM, D), lambda i: (i, 0))]
    nct = T_CTX // TM
    return [pl.BlockSpec((TM, D), lambda i: (jnp.minimum(i, nct - 1), 0)),
            pl.BlockSpec((TM, D), lambda i: (jnp.maximum(i - nct, 0), 0))]


def _x_tile(x_refs):
    if len(x_refs) == 1:
        return x_refs[0][...]
    return jnp.where(pl.program_id(0) < T_CTX // TM, x_refs[0][...], x_refs[1][...])


def _in_proj_kernel(n_x, *refs):
    x_refs = refs[:n_x]
    mod_ref, g_ref, w_ref, qn_ref, kvn_ref, wuq_ref, wukv_ref, pa_ref, pb_ref, pc_ref, pd_ref = refs[n_x:]
    m = mod_ref[...]
    h = _rms(_x_tile(x_refs), g_ref[...]) * (1.0 + m[1:2]) + m[0:1]
    p = _dot(h.astype(BF16), w_ref[...])
    pa_ref[...] = p[:, 0:512].astype(BF16)
    pb_ref[...] = p[:, 512:1280].astype(BF16)
    pd_ref[...] = p[:, C_D:PW].astype(BF16)
    ckv_n = _rms(p[:, C_CKV:C_CKV + 128], kvn_ref[...])
    z = p[:, C_CQ:C_CQ + 256]
    cqn = z * lax.rsqrt(jnp.sum(z * z, axis=-1, keepdims=True) / MLA_Q_RANK + EPS) * qn_ref[...]
    qc = _dot(cqn.astype(BF16), wuq_ref[...])
    kv = _dot(ckv_n.astype(BF16), wukv_ref[...])
    pc_ref[:, 0:384] = qc.astype(BF16)
    pc_ref[:, 384:512] = p[:, C_KPE:C_KPE + 128].astype(BF16)
    pc_ref[:, 512:1024] = kv.astype(BF16)
    pc_ref[:, 1024:1152] = ckv_n.astype(BF16)


def _in_proj(l, xs, mods, g_pre, w_in_p, qn_p, kvn, wuq_p, wukv_p):
    row = lambda i: (i, 0)
    lyr2 = lambda i: (l, 0, 0)
    return pl.pallas_call(
        functools.partial(_in_proj_kernel, len(xs)),
        grid=(T // TM,),
        in_specs=_x_specs(xs) + [
                  pl.BlockSpec((None, None, 6, D), lambda i: (l, _mod_row(i), 0, 0)),
                  pl.BlockSpec((None, 1, D), lyr2),
                  pl.BlockSpec((None, D, PW), lyr2),
                  pl.BlockSpec((None, 1, 256), lyr2),
                  pl.BlockSpec((None, 1, 128), lyr2),
                  pl.BlockSpec((None, 256, 384), lyr2),
                  pl.BlockSpec((None, 128, 512), lyr2)],
        out_specs=[pl.BlockSpec((TM, 512), row), pl.BlockSpec((TM, 768), row),
                   pl.BlockSpec((TM, PC_W), row), pl.BlockSpec((TM, 768), row)],
        out_shape=[jax.ShapeDtypeStruct((T, 512), BF16), jax.ShapeDtypeStruct((T, 768), BF16),
                   jax.ShapeDtypeStruct((T, PC_W), BF16), jax.ShapeDtypeStruct((T, 768), BF16)],
        compiler_params=_cparams(("arbitrary",)),
        name="in_proj",
    )(*xs, mods, g_pre, w_in_p, qn_p, kvn, wuq_p, wukv_p)


def _attn_ctx_kernel(n_carried, sink_ref, lam_ref, pa, pb, pc, pd, subln_ref, *refs):
    o_ref, ka_o, va_o, kb_o, vb_o, ckv_o, kpe_o, kd_o, vd_o = refs[n_carried:]
    n = L_CTX
    low = _low_half(n)
    first = lax.broadcasted_iota(jnp.int32, (2 * n, 1), 0) < n
    ka, va = pa[:, 256:384], _with_ones(pa[:, 384:512])
    res = []
    for g in range(2):
        q = jnp.concatenate([_keep_lanes(pa[:, 0:128], (64 * g, 64)), _keep_lanes(pa[:, 128:256], (64 * g, 64))], axis=0)
        sink = jnp.where(first, sink_ref[2 * g], sink_ref[2 * g + 1])
        res.append(_attend(q, [ka], [va], sink))
    o_ref[:, 0:128] = jnp.where(low, res[0][0:n], res[1][0:n]).astype(BF16)
    o_ref[:, 128:256] = jnp.where(low, res[0][n:], res[1][n:]).astype(BF16)
    for g in range(2):
        ka_o[g] = pa[:, 256 + 64 * g:320 + 64 * g].astype(F32)
        va_o[g] = pa[:, 384 + 64 * g:448 + 64 * g].astype(F32)
    lam = lam_ref[0]
    out_scale = lam_ref[1]
    for j in range(2):
        qp = pb[:, 128 * j:128 * j + 128]
        q = jnp.concatenate([_keep_lanes(qp, (32 * i, 32)) for i in range(4)], axis=0)
        r = _attend(q, [pb[:, 256 + 128 * j:384 + 128 * j]], [_with_ones(pb[:, 512 + 128 * j:640 + 128 * j])])
        o = jnp.where(low, r[0:n] - lam * r[n:2 * n], r[2 * n:3 * n] - lam * r[3 * n:4 * n])
        o_ref[:, 256 + 128 * j:384 + 128 * j] = (_rms_halves(o, low, subln_ref[...]) * out_scale).astype(BF16)
    for h in range(4):
        kb_o[h] = pb[:, 256 + 64 * h:320 + 64 * h].astype(F32)
        vb_o[h] = pb[:, 512 + 64 * h:576 + 64 * h].astype(F32)
    kpe4 = pc[:, 384:512]
    for j in range(2):
        qn, kcat = pc[:, 128 * j:128 * j + 128], jnp.concatenate([pc[:, 512 + 128 * j:640 + 128 * j], kpe4], axis=1)
        qcat = jnp.concatenate([qn, pc[:, 256:384]], axis=1)
        q = jnp.concatenate([_keep_lanes(qcat, (64 * hh, 64), (128 + 32 * (2 * j + hh), 32)) for hh in range(2)], axis=0)
        r = _attend(q, [kcat], [_with_ones(pc[:, 768 + 128 * j:896 + 128 * j])])
        o_ref[:, 512 + 128 * j:640 + 128 * j] = jnp.where(low, r[0:n], r[n:]).astype(BF16)
    ckv_o[...] = pc[:, 1024:1152].astype(F32)
    kpe_o[...] = pc[:, 384:416].astype(F32)
    for j in range(2):
        qp = pd[:, 128 * j:128 * j + 128]
        q = jnp.concatenate([_keep_lanes(qp, (0, 64)), _keep_lanes(qp, (64, 64))], axis=0)
        r = _attend(q, [pd[:, 256 + 128 * j:384 + 128 * j]], [_with_ones(pd[:, 512 + 128 * j:640 + 128 * j])])
        o_ref[:, 768 + 128 * j:896 + 128 * j] = jnp.where(low, r[0:n], r[n:]).astype(BF16)
    for h in range(4):
        kd_o[h] = pd[:, 256 + 64 * h:320 + 64 * h].astype(F32)
        vd_o[h] = pd[:, 512 + 64 * h:576 + 64 * h].astype(F32)


def _attn_ctx(l, sink, lam, pa, pb, pc, pd, subln, carried):
    row = lambda b: (b, 0)
    smem = pl.BlockSpec(memory_space=pltpu.SMEM)
    hd4 = lambda n: pl.BlockSpec((None, None, n, L_CTX, HD), lambda b: (b, l, 0, 0, 0))
    hshape = lambda n: jax.ShapeDtypeStruct((N_CTX_B, DEPTH, n, L_CTX, HD), F32)
    n_in = 7
    return pl.pallas_call(
        functools.partial(_attn_ctx_kernel, len(carried)),
        grid=(N_CTX_B,),
        in_specs=[smem, smem,
                  pl.BlockSpec((L_CTX, 512), row), pl.BlockSpec((L_CTX, 768), row),
                  pl.BlockSpec((L_CTX, PC_W), row), pl.BlockSpec((L_CTX, 768), row),
                  pl.BlockSpec((1, 128), lambda b: (0, 0))] + [pl.BlockSpec(memory_space=pl.ANY)] * len(carried),
        out_specs=[pl.BlockSpec((L_CTX, D), row),
                   hd4(2), hd4(2), hd4(4), hd4(4),
                   pl.BlockSpec((None, None, L_CTX, 128), lambda b: (b, l, 0, 0)),
                   pl.BlockSpec((None, None, L_CTX, 32), lambda b: (b, l, 0, 0)),
                   hd4(4), hd4(4)],
        out_shape=[jax.ShapeDtypeStruct((T_CTX, D), BF16),
                   hshape(2), hshape(2), hshape(4), hshape(4),
                   jax.ShapeDtypeStruct((N_CTX_B, DEPTH, L_CTX, 128), F32),
                   jax.ShapeDtypeStruct((N_CTX_B, DEPTH, L_CTX, 32), F32),
                   hshape(4), hshape(4)],
        input_output_aliases={n_in + i: 1 + i for i in range(len(carried))},
        compiler_params=_cparams(("arbitrary",)),
        name="attn_ctx",
    )(sink, lam, pa, pb, pc, pd, subln, *carried)


_LAT_ROW = lambda b: (T_CTX // N_LAT + b, 0)


def _cache_spec(l, n):
    return pl.BlockSpec((None, None, n, L_CTX, HD), lambda b: (b, l, 0, 0, 0))


def _tab_spec():
    return pl.BlockSpec((N_LAT, 128), lambda b: (0, 0))


def _lat_win_kernel(sink_ref, pa, ck_ref, cv_ref, c_ref, sa_ref, sb_ref, o_ref, qs, kpad, vpad, kcs, vcs):
    c, sa, sb = c_ref[...], sa_ref[...], sb_ref[...]
    for j in range(2):
        qs[:, 128 * j:128 * j + 128] = _rope(pa[:, 128 * j:128 * j + 128].astype(F32), c, sa, sb, 16).astype(BF16)
    zeros = jnp.zeros((WINDOW, 128), BF16)
    kpad[0:WINDOW, :] = zeros
    kpad[WINDOW:WINDOW + N_LAT, :] = _rope(pa[:, 256:384].astype(F32), c, sa, sb, 16).astype(BF16)
    kpad[WINDOW + N_LAT:, :] = zeros
    vpad[0:WINDOW, :] = zeros
    vpad[WINDOW:WINDOW + N_LAT, :] = pa[:, 384:512]
    vpad[WINDOW + N_LAT:, :] = zeros
    for g in range(2):
        kcs[:, 64 * g:64 * g + 64] = ck_ref[g].astype(BF16)
        vcs[:, 64 * g:64 * g + 64] = cv_ref[g].astype(BF16)
    qi = lax.broadcasted_iota(jnp.int32, (2 * WINDOW, 3 * WINDOW), 0) % WINDOW
    kj = lax.broadcasted_iota(jnp.int32, (2 * WINDOW, 3 * WINDOW), 1)
    band = jnp.abs(kj - WINDOW - qi) <= WINDOW
    first_head = lax.broadcasted_iota(jnp.int32, (2 * WINDOW, 1), 0) < WINDOW
    low = _low_half(WINDOW)

    def body(nb, carry):
        r0 = pl.multiple_of(nb * WINDOW, WINDOW)
        kpos = r0 - WINDOW + kj
        valid = band & (kpos >= 0) & (kpos < N_LAT)
        kw = kpad[pl.ds(r0, 3 * WINDOW), :]
        vexts = [_with_ones(vpad[pl.ds(r0, 3 * WINDOW), :]), _with_ones(vcs[...])]
        res = []
        for g in range(2):
            q = jnp.concatenate([_keep_lanes(qs[pl.ds(r0, WINDOW), 0:128], (64 * g, 64)),
                                 _keep_lanes(qs[pl.ds(r0, WINDOW), 128:256], (64 * g, 64))], axis=0)
            s_loc = jnp.where(valid, _dot_t(q, kw), NEG_INF)
            sink = jnp.where(first_head, sink_ref[2 * g], sink_ref[2 * g + 1])
            res.append(_attend_scores([s_loc, _dot_t(q, kcs[...])], vexts, sink))
        o_ref[pl.ds(r0, WINDOW), 0:128] = jnp.where(low, res[0][0:WINDOW], res[1][0:WINDOW]).astype(BF16)
        o_ref[pl.ds(r0, WINDOW), 128:256] = jnp.where(low, res[0][WINDOW:], res[1][WINDOW:]).astype(BF16)
        return carry

    lax.fori_loop(0, N_LAT // WINDOW, body, 0)


def _lat_win(l, sink, pa, ck, cv, tabs):
    return pl.pallas_call(
        _lat_win_kernel,
        grid=(N_LAT_B,),
        in_specs=[pl.BlockSpec(memory_space=pltpu.SMEM),
                  pl.BlockSpec((N_LAT, 512), _LAT_ROW),
                  _cache_spec(l, 2), _cache_spec(l, 2),
                  _tab_spec(), _tab_spec(), _tab_spec()],
        out_specs=pl.BlockSpec((N_LAT, 256), lambda b: (b, 0)),
        out_shape=jax.ShapeDtypeStruct((T_LAT, 256), BF16),
        scratch_shapes=[pltpu.VMEM((N_LAT, 256), BF16),
                        pltpu.VMEM((N_LAT + 2 * WINDOW, 128), BF16),
                        pltpu.VMEM((N_LAT + 2 * WINDOW, 128), BF16),
                        pltpu.VMEM((L_CTX, 128), BF16),
                        pltpu.VMEM((L_CTX, 128), BF16)],
        compiler_params=_cparams(("arbitrary",)),
        name="lat_win",
    )(sink, pa, ck, cv, *tabs)


QT = 256
DIFF_QT = 128
N_KEYS = N_LAT + L_CTX


def _lat_diff_kernel(lam_ref, pb, ck_ref, cv_ref, subln_ref, c_ref, sa_ref, sb_ref, o_ref, qs, kall, vall):
    c, sa, sb = c_ref[...], sa_ref[...], sb_ref[...]
    for j in range(2):
        qs[:, 128 * j:128 * j + 128] = _rope(pb[:, 128 * j:128 * j + 128].astype(F32), c, sa, sb, 8).astype(BF16)
        kall[j, 0:N_LAT, :] = _rope(pb[:, 256 + 128 * j:384 + 128 * j].astype(F32), c, sa, sb, 8).astype(BF16)
        vall[j, 0:N_LAT, :] = pb[:, 512 + 128 * j:640 + 128 * j]
        for hh in range(2):
            kall[j, N_LAT:, 64 * hh:64 * hh + 64] = ck_ref[2 * j + hh].astype(BF16)
            vall[j, N_LAT:, 64 * hh:64 * hh + 64] = cv_ref[2 * j + hh].astype(BF16)
    lam = lam_ref[0]
    out_scale = lam_ref[1]
    low = _low_half(DIFF_QT)
    n = DIFF_QT

    def body(t, carry):
        r0 = pl.multiple_of(t * n, n)
        for j in range(2):
            qp = qs[pl.ds(r0, n), 128 * j:128 * j + 128]
            q = jnp.concatenate([_keep_lanes(qp, (32 * i, 32)) for i in range(4)], axis=0)
            r = _attend(q, [kall[j]], [_with_ones(vall[j])])
            o = jnp.where(low, r[0:n] - lam * r[n:2 * n], r[2 * n:3 * n] - lam * r[3 * n:4 * n])
            o_ref[pl.ds(r0, n), 128 * j:128 * j + 128] = (_rms_halves(o, low, subln_ref[...]) * out_scale).astype(BF16)
        return carry

    lax.fori_loop(0, N_LAT // n, body, 0)


def _lat_diff(l, lam, pb, ck, cv, subln, tabs):
    return pl.pallas_call(
        _lat_diff_kernel,
        grid=(N_LAT_B,),
        in_specs=[pl.BlockSpec(memory_space=pltpu.SMEM),
                  pl.BlockSpec((N_LAT, 768), _LAT_ROW),
                  _cache_spec(l, 4), _cache_spec(l, 4),
                  pl.BlockSpec((1, 128), lambda b: (0, 0)),
                  _tab_spec(), _tab_spec(), _tab_spec()],
        out_specs=pl.BlockSpec((N_LAT, 256), lambda b: (b, 0)),
        out_shape=jax.ShapeDtypeStruct((T_LAT, 256), BF16),
        scratch_shapes=[pltpu.VMEM((N_LAT, 256), BF16),
                        pltpu.VMEM((2, N_KEYS, 128), BF16),
                        pltpu.VMEM((2, N_KEYS, 128), BF16)],
        compiler_params=_cparams(("arbitrary",)),
        name="lat_diff",
    )(lam, pb, ck, cv, subln, *tabs)


def _lat_mla_kernel(pc, ckv_ref, kpe_ref, wukv_ref, c_ref, sa_ref, sb_ref, o_ref, qcat, kcat, vall):
    c, sa, sb = c_ref[...], sa_ref[...], sb_ref[...]
    qr = _rope(pc[:, 256:384].astype(F32), c, sa, sb, 8).astype(BF16)
    kpe_l = _rope(pc[:, 384:512].astype(F32), c, sa, sb, 8).astype(BF16)
    kvc = _dot(ckv_ref[...].astype(BF16), wukv_ref[...]).astype(BF16)
    kpe_c = kpe_ref[...].astype(BF16)
    for j in range(2):
        qcat[j, :, 0:128] = pc[:, 128 * j:128 * j + 128]
        qcat[j, :, 128:256] = qr
        kcat[j, 0:N_LAT, 0:128] = pc[:, 512 + 128 * j:640 + 128 * j]
        kcat[j, 0:N_LAT, 128:256] = kpe_l
        kcat[j, N_LAT:, 0:128] = kvc[:, 128 * j:128 * j + 128]
        for i in range(4):
            kcat[j, N_LAT:, 128 + 32 * i:160 + 32 * i] = kpe_c
        vall[j, 0:N_LAT, :] = pc[:, 768 + 128 * j:896 + 128 * j]
        vall[j, N_LAT:, :] = kvc[:, 256 + 128 * j:384 + 128 * j]
    low = _low_half(QT)

    def body(t, carry):
        r0 = pl.multiple_of(t * QT, QT)
        for j in range(2):
            q2 = qcat[j, pl.ds(r0, QT), :]
            q = jnp.concatenate([_keep_lanes(q2, (64 * hh, 64), (128 + 32 * (2 * j + hh), 32)) for hh in range(2)],
                                axis=0)
            r = _attend(q, [kcat[j]], [_with_ones(vall[j])])
            o_ref[pl.ds(r0, QT), 128 * j:128 * j + 128] = jnp.where(low, r[0:QT], r[QT:]).astype(BF16)
        return carry

    lax.fori_loop(0, N_LAT // QT, body, 0)


def _lat_mla(l, pc, ckv, kpe, wukv_p, tabs):
    return pl.pallas_call(
        _lat_mla_kernel,
        grid=(N_LAT_B,),
        in_specs=[pl.BlockSpec((N_LAT, PC_W), _LAT_ROW),
                  pl.BlockSpec((None, None, L_CTX, 128), lambda b: (b, l, 0, 0)),
                  pl.BlockSpec((None, None, L_CTX, 32), lambda b: (b, l, 0, 0)),
                  pl.BlockSpec((None, 128, 512), lambda b: (l, 0, 0)),
                  _tab_spec(), _tab_spec(), _tab_spec()],
        out_specs=pl.BlockSpec((N_LAT, 256), lambda b: (b, 0)),
        out_shape=jax.ShapeDtypeStruct((T_LAT, 256), BF16),
        scratch_shapes=[pltpu.VMEM((2, N_LAT, 256), BF16),
                        pltpu.VMEM((2, N_KEYS, 256), BF16),
                        pltpu.VMEM((2, N_KEYS, 128), BF16)],
        compiler_params=_cparams(("arbitrary",)),
        name="lat_mla",
    )(pc, ckv, kpe, wukv_p, *tabs)


NA_ROWS = 8
NA_KEYS = NA_ROWS * GRID_W


def _lat_na_kernel(pd, ck_ref, cv_ref, bias_ref, o_ref, kcs, vcs):
    n_rows = N_LAT // GRID_W
    for h in range(4):
        kcs[h // 2, :, 64 * (h % 2):64 * (h % 2) + 64] = ck_ref[h].astype(BF16)
        vcs[h // 2, :, 64 * (h % 2):64 * (h % 2) + 64] = cv_ref[h].astype(BF16)
    low = _low_half(GRID_W)

    def body(r, carry):
        rs = jnp.clip(r - NA_ROWS // 2, 0, n_rows - NA_ROWS)
        q0 = pl.multiple_of(r * GRID_W, GRID_W)
        k0 = pl.multiple_of(rs * GRID_W, GRID_W)
        for j in range(2):
            qp = pd[pl.ds(q0, GRID_W), 128 * j:128 * j + 128]
            q = jnp.concatenate([_keep_lanes(qp, (0, 64)), _keep_lanes(qp, (64, 64))], axis=0)
            s_loc = _dot_t(q, pd[pl.ds(k0, NA_KEYS), 256 + 128 * j:384 + 128 * j]) + bias_ref[j, r - rs]
            s_ctx = _dot_t(q, kcs[j])
            res = _attend_scores([s_loc, s_ctx],
                                 [_with_ones(pd[pl.ds(k0, NA_KEYS), 512 + 128 * j:640 + 128 * j]), _with_ones(vcs[j])])
            o_ref[pl.ds(q0, GRID_W), 128 * j:128 * j + 128] = jnp.where(low, res[0:GRID_W], res[GRID_W:]).astype(BF16)
        return carry

    lax.fori_loop(0, n_rows, body, 0)


def _lat_na(l, pd, ck, cv, bias):
    return pl.pallas_call(
        _lat_na_kernel,
        grid=(N_LAT_B,),
        in_specs=[pl.BlockSpec((N_LAT, 768), _LAT_ROW),
                  _cache_spec(l, 4), _cache_spec(l, 4),
                  pl.BlockSpec((None, 2, NA_ROWS, 2 * GRID_W, NA_KEYS), lambda b: (l, 0, 0, 0, 0))],
        out_specs=pl.BlockSpec((N_LAT, 256), lambda b: (b, 0)),
        out_shape=jax.ShapeDtypeStruct((T_LAT, 256), BF16),
        scratch_shapes=[pltpu.VMEM((2, L_CTX, 128), BF16), pltpu.VMEM((2, L_CTX, 128), BF16)],
        compiler_params=_cparams(("arbitrary",)),
        name="lat_na",
    )(pd, ck, cv, bias)


def _out_proj_kernel(with_router, n_x, *refs):
    x_refs = refs[:n_x]
    refs = refs[n_x:]
    if with_router:
        (oc_ref, oa_ref, ob_ref, om_ref, od_ref, mod_ref, wo_ref, gpost_ref, gffn_ref, r_ref,
         x1_ref, h2_ref, route_ref, cnt_ref) = refs
    else:
        (oc_ref, oa_ref, ob_ref, om_ref, od_ref, mod_ref, wo_ref, gpost_ref, gffn_ref,
         x1_ref, h2_ref) = refs
    i = pl.program_id(0)
    nct = T_CTX // TM

    if with_router:
        @pl.when(i == 0)
        def _():
            cnt_ref[...] = jnp.zeros(cnt_ref.shape, F32)

    def finish(y):
        m = mod_ref[...]
        x1 = _x_tile(x_refs) + m[2:3] * _rms(y, gpost_ref[...])
        x1_ref[...] = x1
        h2 = _rms(x1, gffn_ref[...]) * (1.0 + m[4:5]) + m[3:4]
        h2_ref[...] = h2.astype(h2_ref.dtype)
        if with_router:
            lg = _dot(h2.astype(BF16), r_ref[...])
            lane = lax.broadcasted_iota(jnp.int32, lg.shape, 1)
            lane_f = lane.astype(F32)
            lg = jnp.where(lane < N_EXPERTS, lg, -jnp.inf)
            m1 = jnp.max(lg, axis=-1, keepdims=True)
            i1 = jnp.min(jnp.where(lg == m1, lane_f, 128.0), axis=-1, keepdims=True)
            lg2 = jnp.where(lane_f == i1, -jnp.inf, lg)
            m2 = jnp.max(lg2, axis=-1, keepdims=True)
            i2 = jnp.min(jnp.where(lg2 == m2, lane_f, 128.0), axis=-1, keepdims=True)
            e = jnp.exp(m2 - m1)
            g1 = 1.0 / (1.0 + e)
            g2 = e / (1.0 + e)
            sel1 = lane_f == i1
            sel2 = lane_f == i2
            picked = jnp.where(sel1 | sel2, 1.0, 0.0)
            ri = lax.broadcasted_iota(jnp.int32, (TM, TM), 0)
            ci = lax.broadcasted_iota(jnp.int32, (TM, TM), 1)
            before = _dot(jnp.where(ci < ri, 1.0, 0.0).astype(BF16), picked.astype(BF16)) + cnt_ref[0:1, :]
            r1 = jnp.sum(jnp.where(sel1, before, 0.0), axis=-1, keepdims=True)
            r2 = jnp.sum(jnp.where(sel2, before, 0.0), axis=-1, keepdims=True)
            cnt_ref[...] = cnt_ref[...] + jnp.sum(picked, axis=0, keepdims=True)
            route = jnp.where(lane == 0, i1, jnp.where(lane == 1, i2, jnp.where(lane == 2, g1, 0.0)))
            route_ref[...] = jnp.where(lane == 3, g2, jnp.where(lane == 4, r1, jnp.where(lane == 5, r2, route)))

    @pl.when(i < nct)
    def _():
        finish(_dot(oc_ref[...], wo_ref[...]))

    @pl.when(i >= nct)
    def _():
        y = _dot(oa_ref[...], wo_ref[0:256, :])
        y = y + _dot(ob_ref[...], wo_ref[256:512, :])
        y = y + _dot(om_ref[...], wo_ref[512:768, :])
        y = y + _dot(od_ref[...], wo_ref[768:1024, :])
        finish(y)


def _out_proj(l, o_ctx, o_lat, xs, mods, w_o, g_post, g_ffn, router_p):
    nct = T_CTX // TM
    row = lambda i: (i, 0)
    lyr2 = lambda i: (l, 0, 0)
    lat = lambda i: (jnp.maximum(i - nct, 0), 0)
    with_router = router_p is not None
    in_specs = _x_specs(xs) + [
                pl.BlockSpec((TM, D), lambda i: (jnp.minimum(i, nct - 1), 0)),
                pl.BlockSpec((TM, 256), lat), pl.BlockSpec((TM, 256), lat),
                pl.BlockSpec((TM, 256), lat), pl.BlockSpec((TM, 256), lat),
                pl.BlockSpec((None, None, 6, D), lambda i: (l, _mod_row(i), 0, 0)),
                pl.BlockSpec((None, D, D), lyr2),
                pl.BlockSpec((None, 1, D), lyr2),
                pl.BlockSpec((None, 1, D), lyr2)]
    out_specs = [pl.BlockSpec((TM, D), row), pl.BlockSpec((TM, D), row)]
    out_shape = [jax.ShapeDtypeStruct((T, D), F32), jax.ShapeDtypeStruct((T, D), F32 if with_router else BF16)]
    args = [*xs, o_ctx, *o_lat, mods, w_o, g_post, g_ffn]
    if with_router:
        in_specs.append(pl.BlockSpec((D, 128), lambda i: (0, 0)))
        out_specs += [pl.BlockSpec((TM, 128), row), pl.BlockSpec((8, 128), lambda i: (0, 0))]
        out_shape += [jax.ShapeDtypeStruct((T, 128), F32), jax.ShapeDtypeStruct((8, 128), F32)]
        args.append(router_p)
    return pl.pallas_call(
        functools.partial(_out_proj_kernel, with_router, len(xs)),
        grid=(T // TM,),
        in_specs=in_specs, out_specs=out_specs, out_shape=out_shape,
        compiler_params=_cparams(("arbitrary",)),
        name="out_proj_router" if with_router else "out_proj",
    )(*args)


FFN_TM = 1024


def _ffn_kernel(h_ref, x_ref, mod_ref, g_ref, wg_ref, wu_ref, wd_ref, o_ref, acc_ref):
    c = pl.program_id(1)
    h = h_ref[...]
    t = (_silu(_dot(h, wg_ref[...])) * _dot(h, wu_ref[...])).astype(BF16)
    part = _dot(t, wd_ref[...])

    @pl.when(c == 0)
    def _():
        acc_ref[...] = part

    @pl.when(c > 0)
    def _():
        acc_ref[...] += part

    @pl.when(c == pl.num_programs(1) - 1)
    def _():
        m = mod_ref[...]
        o_ref[...] = x_ref[...] + m[5:6] * _rms(acc_ref[...], g_ref[...])


def _ffn_mod_row(i):
    nct = T_CTX // FFN_TM
    return jnp.where(i >= nct, 1 + (i - nct) // (N_LAT // FFN_TM), 0)


def _ffn(l, i_dense, h2, x1, mods, g_post, wg, wu, wd):
    row = lambda i, c: (i, 0)
    return pl.pallas_call(
        _ffn_kernel,
        grid=(T // FFN_TM, D_FF // FFN_CHUNK),
        in_specs=[pl.BlockSpec((FFN_TM, D), row),
                  pl.BlockSpec((FFN_TM, D), row),
                  pl.BlockSpec((None, None, 6, D), lambda i, c: (l, _ffn_mod_row(i), 0, 0)),
                  pl.BlockSpec((None, 1, D), lambda i, c: (l, 0, 0)),
                  pl.BlockSpec((None, D, FFN_CHUNK), lambda i, c: (i_dense, 0, c)),
                  pl.BlockSpec((None, D, FFN_CHUNK), lambda i, c: (i_dense, 0, c)),
                  pl.BlockSpec((None, FFN_CHUNK, D), lambda i, c: (i_dense, c, 0))],
        out_specs=pl.BlockSpec((FFN_TM, D), row),
        out_shape=jax.ShapeDtypeStruct((T, D), F32),
        scratch_shapes=[pltpu.VMEM((FFN_TM, D), F32)],
        compiler_params=_cparams(("arbitrary", "arbitrary")),
        name="ffn",
    )(h2, x1, mods, g_post, wg, wu, wd)


def _dispatch_kernel(slot_ref, h_ref, xs_hbm, sem):
    def row_copy(t, k):
        return pltpu.make_async_copy(h_ref.at[pl.ds(t, 1)], xs_hbm.at[pl.ds(slot_ref[0, 2 * t + k], 1)], sem)

    def issue(t, carry):
        row_copy(t, 0).start()
        row_copy(t, 1).start()
        return carry

    def drain(t, carry):
        row_copy(t, 0).wait()
        row_copy(t, 1).wait()
        return carry

    lax.fori_loop(0, DISPATCH_TM, issue, 0, unroll=8)
    lax.fori_loop(0, DISPATCH_TM, drain, 0, unroll=8)


def _dispatch(slots, h2):
    return pl.pallas_call(
        _dispatch_kernel,
        grid=(T // DISPATCH_TM,),
        in_specs=[pl.BlockSpec((None, 1, 2 * DISPATCH_TM), lambda i: (i, 0, 0), memory_space=pltpu.SMEM),
                  pl.BlockSpec((DISPATCH_TM, D), lambda i: (i, 0))],
        out_specs=pl.BlockSpec(memory_space=pl.ANY),
        out_shape=jax.ShapeDtypeStruct((N_EXPERTS * EXPERT_CAP, D), F32),
        scratch_shapes=[pltpu.SemaphoreType.DMA(())],
        compiler_params=_cparams(("arbitrary",)),
        name="moe_dispatch",
    )(slots.reshape(T // DISPATCH_TM, 1, 2 * DISPATCH_TM), h2)


def _moe_kernel(ie_ref, ib_ref, nr_ref, x_ref, wg_ref, wu_ref, wd_ref, o_ref, wgb, wub, wdb):
    i = pl.program_id(0)
    c = pl.program_id(1)
    n_rows = nr_ref[i]

    @pl.when(n_rows > 0)
    def _():
        @pl.when(c == 0)
        def _():
            o_ref[...] = jnp.zeros(o_ref.shape, F32)

        wgb[...] = wg_ref[...].astype(BF16)
        wub[...] = wu_ref[...].astype(BF16)
        wdb[...] = wd_ref[...].astype(BF16)

        def step(r0, rows):
            live = r0 + lax.broadcasted_iota(jnp.int32, (rows, 1), 0) < n_rows
            xs = jnp.where(live, x_ref[pl.ds(r0, rows), :], 0.0).astype(BF16)
            t = (_silu(_dot(xs, wgb[...])) * _dot(xs, wub[...])).astype(BF16)
            o_ref[pl.ds(r0, rows), :] += _dot(t, wdb[...])

        whole = lax.shift_right_logical(n_rows, SUB_ROWS.bit_length() - 1)
        left = jnp.bitwise_and(n_rows, SUB_ROWS - 1)
        n_full = whole + (left > TAIL_ROWS).astype(jnp.int32)

        def body(s, carry):
            step(pl.multiple_of(s * SUB_ROWS, SUB_ROWS), SUB_ROWS)
            return carry

        lax.fori_loop(0, n_full, body, 0)

        @pl.when((left > 0) & (left <= TAIL_ROWS))
        def _():
            step(pl.multiple_of(whole * SUB_ROWS, SUB_ROWS), TAIL_ROWS)


def _moe(i_moe, item_e, item_blk, item_rows, x_sorted, wg, wu, wd):
    nc = D_FF_EXPERT // FF_CHUNK
    chunk = lambda i, c, nr: jnp.where(nr[i] > 0, c, nc - 1)
    grid_spec = pltpu.PrefetchScalarGridSpec(
        num_scalar_prefetch=3,
        grid=(N_ITEMS, nc),
        in_specs=[pl.BlockSpec((ITEM_ROWS, D), lambda i, c, ie, ib, nr: (ib[i], 0)),
                  pl.BlockSpec((None, None, D, FF_CHUNK),
                               lambda i, c, ie, ib, nr: (i_moe, ie[i], 0, chunk(i, c, nr))),
                  pl.BlockSpec((None, None, D, FF_CHUNK),
                               lambda i, c, ie, ib, nr: (i_moe, ie[i], 0, chunk(i, c, nr))),
                  pl.BlockSpec((None, None, FF_CHUNK, D),
                               lambda i, c, ie, ib, nr: (i_moe, ie[i], chunk(i, c, nr), 0))],
        out_specs=pl.BlockSpec((ITEM_ROWS, D), lambda i, c, ie, ib, nr: (ib[i], 0)),
        scratch_shapes=[pltpu.VMEM((D, FF_CHUNK), BF16), pltpu.VMEM((D, FF_CHUNK), BF16),
                        pltpu.VMEM((FF_CHUNK, D), BF16)],
    )
    return pl.pallas_call(
        _moe_kernel,
        grid_spec=grid_spec,
        out_shape=jax.ShapeDtypeStruct((N_EXPERTS * EXPERT_CAP, D), F32),
        compiler_params=_cparams(("arbitrary", "arbitrary")),
        name="moe_experts",
    )(item_e, item_blk, item_rows, x_sorted, wg, wu, wd)


def _post_kernel(final, slot_ref, ys_hbm, route_ref, x_ref, mod_ref, g_ref, *refs):
    buf, sem = refs[-2:]
    def row_copy(t, k):
        return pltpu.make_async_copy(ys_hbm.at[pl.ds(slot_ref[0, 2 * t + k], 1)], buf.at[k, pl.ds(t, 1)], sem)

    def issue(t, carry):
        row_copy(t, 0).start()
        row_copy(t, 1).start()
        return carry

    def drain(t, carry):
        row_copy(t, 0).wait()
        row_copy(t, 1).wait()
        return carry

    lax.fori_loop(0, TM, issue, 0, unroll=8)
    lax.fori_loop(0, TM, drain, 0, unroll=8)
    r = route_ref[...]
    y = r[:, 2:3] * buf[0] + r[:, 3:4] * buf[1]
    m = mod_ref[...]
    out = x_ref[...] + m[5:6] * _rms(y, g_ref[...])
    if not final:
        refs[0][...] = out
    else:
        i = pl.program_id(0)

        @pl.when(i < T_CTX // TM)
        def _():
            refs[0][...] = out

        @pl.when(i >= T_CTX // TM)
        def _():
            refs[1][...] = out


def _post(l, slots, y_sorted, route, x1, mods, g_post, final):
    row = lambda i: (i, 0)
    nct = T_CTX // TM
    if final:
        out_specs = [pl.BlockSpec((TM, D), lambda i: (jnp.minimum(i, nct - 1), 0)),
                     pl.BlockSpec((TM, D), lambda i: (jnp.maximum(i - nct, 0), 0))]
        out_shape = [jax.ShapeDtypeStruct((T_CTX, D), F32), jax.ShapeDtypeStruct((T_LAT, D), F32)]
    else:
        out_specs = pl.BlockSpec((TM, D), row)
        out_shape = jax.ShapeDtypeStruct((T, D), F32)
    return pl.pallas_call(
        functools.partial(_post_kernel, final),
        grid=(T // TM,),
        in_specs=[pl.BlockSpec((None, 1, 2 * TM), lambda i: (i, 0, 0), memory_space=pltpu.SMEM),
                  pl.BlockSpec(memory_space=pl.ANY),
                  pl.BlockSpec((TM, 128), row), pl.BlockSpec((TM, D), row),
                  pl.BlockSpec((None, None, 6, D), lambda i: (l, _mod_row(i), 0, 0)),
                  pl.BlockSpec((None, 1, D), lambda i: (l, 0, 0))],
        out_specs=out_specs,
        out_shape=out_shape,
        scratch_shapes=[pltpu.VMEM((2, TM, D), F32), pltpu.SemaphoreType.DMA(())],
        compiler_params=_cparams(("arbitrary",)),
        name="moe_post_final" if final else "moe_post",
    )(slots.reshape(T // TM, 1, 2 * TM), y_sorted, route, x1, mods, g_post)


def _route_plan(route, counts):
    i32 = jnp.int32
    slots = (route[:, 0:2].astype(i32) * EXPERT_CAP + route[:, 4:6].astype(i32)).reshape(-1)
    n = counts[0, :N_EXPERTS].astype(i32)
    n_items = (n + ITEM_ROWS - 1) // ITEM_ROWS
    ends = jnp.cumsum(n_items)
    base = ends - n_items
    total = ends[-1]
    ii = jnp.arange(N_ITEMS, dtype=i32)
    ii_c = jnp.minimum(ii, total - 1)
    item_e = jnp.sum((ends[None, :] <= ii_c[:, None]).astype(i32), axis=1)
    j = ii_c - base[item_e]
    item_blk = item_e * (EXPERT_CAP // ITEM_ROWS) + j
    rows = jnp.where(ii < total, jnp.clip(n[item_e] - j * ITEM_ROWS, 0, ITEM_ROWS), 0)
    return slots, item_e.astype(i32), item_blk.astype(i32), rows.astype(i32)


def _rope_tables(d):
    half, quarter = d // 2, d // 4
    t = jnp.arange(N_LAT)
    pr, pc = t // GRID_W, t % GRID_W
    i = jnp.arange(128) % d
    j = (i % half) % quarter
    inv = ROPE_THETA ** (-(2 * j).astype(F32) / half)
    pos = jnp.where((i < half)[None, :], pr[:, None], pc[:, None]).astype(F32)
    ang = pos * inv[None, :]
    low = ((i % half) < quarter)[None, :]
    sin = jnp.sin(ang)
    return jnp.cos(ang), jnp.where(low, -sin, 0.0), jnp.where(low, 0.0, sin)


def _na_bias_table(rpb):
    q = jnp.arange(GRID_W)
    k = jnp.arange(GRID_W)
    rel_c = jnp.clip(k[None, :] - q[:, None], -15, 15) + 15
    cs = jnp.clip(q - 8, 0, GRID_W - 16)
    valid = (k[None, :] >= cs[:, None]) & (k[None, :] < cs[:, None] + 16)
    pick = rel_c[:, :, None] == jnp.arange(31)[None, None, :]
    band = jnp.sum(jnp.where(pick[None, None, None], rpb[:, :, :, None, None, :], 0.0), axis=-1)
    band = jnp.where(valid[None, None, None], band, NEG_INF)
    tab = jnp.stack([band[:, :, NA_ROWS - 1 - dd:2 * NA_ROWS - 1 - dd] for dd in range(NA_ROWS)], axis=2)
    tab = tab.transpose(0, 1, 2, 4, 3, 5)
    tab = tab.reshape(DEPTH, 2, 2, NA_ROWS, GRID_W, NA_KEYS).transpose(0, 1, 3, 2, 4, 5)
    return tab.reshape(DEPTH, 2, NA_ROWS, 2 * GRID_W, NA_KEYS)


def _pad_in_proj(w_in):
    z = lambda n: jnp.zeros((DEPTH, D, n), w_in.dtype)
    s64, s32 = HD ** -0.5, (HD // 2) ** -0.5
    qa = w_in[:, :, 0:256] * s64
    kpe = w_in[:, :, 1600:1632]
    return jnp.concatenate([qa[:, :, 0:64], qa[:, :, 128:192], qa[:, :, 64:128], qa[:, :, 192:256],
                            w_in[:, :, 256:512],
                            w_in[:, :, 512:768] * s32, w_in[:, :, 768:1280],
                            w_in[:, :, 1472:1600], kpe, kpe, kpe, kpe,
                            w_in[:, :, 1280:1472], z(64),
                            w_in[:, :, 1632:1888] * s64, w_in[:, :, 1888:2400]], axis=-1).astype(BF16)


def _split_heads_cols(w, first, second):
    lead = w.shape[:-1]
    w4 = w.reshape(*lead, 4, first + second)
    return jnp.concatenate([w4[..., :first].reshape(*lead, 4 * first),
                            w4[..., first:].reshape(*lead, 4 * second)], axis=-1)


def kernel(x_prompt, x_sample, cache_win_k, cache_win_v, cache_diff_k, cache_diff_v, cache_mla_ckv, cache_mla_kpe, cache_na_k, cache_na_v, c, c_ctx, w_ada, b_ada, norm_mix_pre, norm_mix_post, norm_ffn_pre, norm_ffn_post, w_in, w_o, win_sink, diff_lq1, diff_lk1, diff_lq2, diff_lk2, diff_subln, mla_q_norm, mla_kv_norm, mla_wuq, mla_wukv, na_rpb, ffn_w_gate, ffn_w_up, ffn_w_down, moe_router, moe_w_gate, moe_w_up, moe_w_down):
    w_in_p = _pad_in_proj(w_in)
    wuq_p = jnp.pad(_split_heads_cols(mla_wuq, 64, 32) * (HD + HD // 2) ** -0.5,
                    ((0, 0), (0, 256 - MLA_Q_RANK), (0, 0))).astype(BF16)
    wukv_p = _split_heads_cols(mla_wukv, 64, 64).astype(BF16)
    qn_p = jnp.pad(mla_q_norm, ((0, 0), (0, 256 - MLA_Q_RANK))).reshape(DEPTH, 1, 256)
    kvn = mla_kv_norm.reshape(DEPTH, 1, 128)
    w_o_b = jnp.concatenate([w_o[:, 0:64], w_o[:, 128:192], w_o[:, 64:128], w_o[:, 192:]], axis=1).astype(BF16)
    ffn_wg, ffn_wu, ffn_wd = ffn_w_gate.astype(BF16), ffn_w_up.astype(BF16), ffn_w_down.astype(BF16)
    router_p = jnp.pad(moe_router, ((0, 0), (0, 0), (0, 128 - N_EXPERTS))).astype(BF16)
    g_mix_pre = norm_mix_pre.reshape(DEPTH, 1, D)
    g_mix_post = norm_mix_post.reshape(DEPTH, 1, D)
    g_ffn_pre = norm_ffn_pre.reshape(DEPTH, 1, D)
    g_ffn_post = norm_ffn_post.reshape(DEPTH, 1, D)
    tabs64 = _rope_tables(64)
    tabs32 = _rope_tables(32)
    na_bias = _na_bias_table(na_rpb)
    lam_init = jnp.asarray([0.8 - 0.6 * math.exp(-0.3 * l) for l in range(DEPTH)], F32)
    lam = (jnp.exp(jnp.sum(diff_lq1 * diff_lk1, axis=-1)) - jnp.exp(jnp.sum(diff_lq2 * diff_lk2, axis=-1))
           + lam_init)
    lam_tab = jnp.stack([lam, 1.0 - lam_init], axis=-1)
    subln = jnp.tile(diff_subln, (1, 2)).reshape(DEPTH, 1, 2 * HD)

    cond = jnp.concatenate([c_ctx[None], c, jnp.zeros((3, D), F32)], axis=0)
    mods = _mods(cond, w_ada, b_ada).reshape(DEPTH, 8, 6, D)

    xs = (x_prompt.reshape(T_CTX, D), x_sample.reshape(T_LAT, D))
    caches = ()
    for l in range(DEPTH):
        pa, pb, pc, pd = _in_proj(l, xs, mods, g_mix_pre, w_in_p, qn_p, kvn, wuq_p, wukv_p)
        o_ctx, *caches = _attn_ctx(l, win_sink[l], lam_tab[l], pa, pb, pc, pd, subln[l], caches)
        o_lat = [_lat_win(l, win_sink[l], pa, cache_win_k, cache_win_v, tabs64),
                 _lat_diff(l, lam_tab[l], pb, cache_diff_k, cache_diff_v, subln[l], tabs32),
                 _lat_mla(l, pc, cache_mla_ckv, cache_mla_kpe, wukv_p, tabs32),
                 _lat_na(l, pd, cache_na_k, cache_na_v, na_bias)]
        if l % 2 == 0:
            x1, h2 = _out_proj(l, o_ctx, o_lat, xs, mods, w_o_b, g_mix_post, g_ffn_pre, None)
            xs = (_ffn(l, l // 2, h2, x1, mods, g_ffn_post, ffn_wg, ffn_wu, ffn_wd),)
        else:
            x1, h2, route, counts = _out_proj(l, o_ctx, o_lat, xs, mods, w_o_b, g_mix_post, g_ffn_pre,
                                              router_p[l // 2])
            slots, item_e, item_blk, item_rows = _route_plan(route, counts)
            x_sorted = _dispatch(slots, h2)
            y_sorted = _moe(l // 2, item_e, item_blk, item_rows, x_sorted, moe_w_gate, moe_w_up, moe_w_down)
            xs = _post(l, slots, y_sorted, route, x1, mods, g_ffn_post, final=l == DEPTH - 1)
            xs = tuple(xs) if l == DEPTH - 1 else (xs,)

    y_prompt, y_sample = xs
    return (y_prompt.reshape(N_CTX_B, L_CTX, D), y_sample.reshape(N_LAT_B, N_LAT, D)) + tuple(caches)
```

```python
import functools
import math

import jax
import jax.numpy as jnp
from jax import lax
from jax.experimental import pallas as pl
from jax.experimental.pallas import tpu as pltpu

F32 = jnp.float32
BF16 = jnp.bfloat16

D = 1024
DEPTH = 4
N_CTX_B, L_CTX = 16, 256
N_LAT_B, N_LAT = 4, 1024
T_CTX = N_CTX_B * L_CTX
T_LAT = N_LAT_B * N_LAT
T = T_CTX + T_LAT
HD = 64
GRID_W = 64
WINDOW = 128
EPS = 1e-6
NEG_INF = -1e30
ROPE_THETA = 10000.0
MLA_Q_RANK = 192
D_FF = 2816
N_EXPERTS = 8
D_FF_EXPERT = 3584

PW = 2560
C_CKV, C_KPE, C_CQ, C_D = 1280, 1408, 1536, 1792
PC_W = 1152

TM = 512
EXPERT_CAP = T
ITEM_ROWS = 1024
SUB_ROWS = 512
TAIL_ROWS = 256
N_ITEMS = N_EXPERTS + (2 * T) // ITEM_ROWS
FF_CHUNK = 896
DISPATCH_TM = 1024
POST_TM = 1024
FFN_CHUNK = 1408
VMEM_LIMIT = 56 * 1024 * 1024


def _cparams(sem):
    return pltpu.CompilerParams(dimension_semantics=sem, vmem_limit_bytes=VMEM_LIMIT)


def _rms(x, g):
    return x * lax.rsqrt(jnp.mean(x * x, axis=-1, keepdims=True) + EPS) * g


def _silu(a):
    return a / (1.0 + jnp.exp(-a))


def _dot(a, b):
    return jnp.dot(a, b, preferred_element_type=F32)


def _dot_t(a, b):
    return lax.dot_general(a, b, (((1,), (1,)), ((), ())), preferred_element_type=F32)


def _with_ones(v):
    return jnp.concatenate([v, jnp.ones(v.shape, v.dtype)], axis=1)


def _attend_scores(scores, vexts, extra_logit=None):
    mx = jnp.max(scores[0], axis=-1, keepdims=True)
    for s in scores[1:]:
        mx = jnp.maximum(mx, jnp.max(s, axis=-1, keepdims=True))
    if extra_logit is not None:
        mx = jnp.maximum(mx, extra_logit)
    r = None
    for s, v in zip(scores, vexts):
        part = _dot(jnp.exp(s - mx).astype(BF16), v)
        r = part if r is None else r + part
    den = r[:, 128:256]
    if extra_logit is not None:
        den = den + jnp.exp(extra_logit - mx)
    return r[:, 0:128] / den


def _attend(q, keys, vexts, extra_logit=None):
    return _attend_scores([_dot_t(q, k) for k in keys], vexts, extra_logit)


def _keep_lanes(x, *ranges):
    lane = lax.broadcasted_iota(jnp.int32, x.shape, 1)
    keep = None
    for lo, width in ranges:
        k = (lane >= lo) & (lane < lo + width)
        keep = k if keep is None else keep | k
    return jnp.where(keep, x, jnp.zeros_like(x))


def _low_half(n):
    return lax.broadcasted_iota(jnp.int32, (n, 128), 1) < HD


def _rms_halves(o, low, g):
    sq = o * o
    ms = jnp.where(low, jnp.sum(jnp.where(low, sq, 0.0), axis=-1, keepdims=True),
                   jnp.sum(jnp.where(low, 0.0, sq), axis=-1, keepdims=True)) / HD
    return o * lax.rsqrt(ms + EPS) * g


def _rope(x, c, sa, sb, shift):
    return x * c + pltpu.roll(x, 128 - shift, 1) * sa + pltpu.roll(x, shift, 1) * sb


def _mod_row(i):
    nct = T_CTX // TM
    per_b = N_LAT // TM
    return jnp.where(i >= nct, 1 + (i - nct) // per_b, 0)


def _mods_kernel(cond_ref, w_ref, b_ref, o_ref):
    a = _silu(cond_ref[...]).astype(BF16)
    o_ref[...] = _dot(a, w_ref[...].astype(BF16)) + b_ref[...]


def _mods(cond, w_ada, b_ada):
    tn = 1536
    n = 6 * D
    return pl.pallas_call(
        _mods_kernel,
        grid=(DEPTH, n // tn),
        in_specs=[pl.BlockSpec((8, D), lambda l, j: (0, 0)),
                  pl.BlockSpec((None, D, tn), lambda l, j: (l, 0, j)),
                  pl.BlockSpec((None, 1, tn), lambda l, j: (l, 0, j))],
        out_specs=pl.BlockSpec((None, 8, tn), lambda l, j: (l, 0, j)),
        out_shape=jax.ShapeDtypeStruct((DEPTH, 8, n), F32),
        compiler_params=_cparams(("arbitrary", "arbitrary")),
        name="mods",
    )(cond, w_ada, b_ada.reshape(DEPTH, 1, n))


def _x_specs(xs):
    if len(xs) == 1:
        return [pl.BlockSpec((TM, D), lambda i: (i, 0))]
    nct = T_CTX // TM
    return [pl.BlockSpec((TM, D), lambda i: (jnp.minimum(i, nct - 1), 0)),
            pl.BlockSpec((TM, D), lambda i: (jnp.maximum(i - nct, 0), 0))]


def _x_tile(x_refs):
    if len(x_refs) == 1:
        return x_refs[0][...]
    return jnp.where(pl.program_id(0) < T_CTX // TM, x_refs[0][...], x_refs[1][...])


def _in_proj_kernel(n_x, *refs):
    x_refs = refs[:n_x]
    mod_ref, g_ref, w_ref, qn_ref, kvn_ref, wuq_ref, wukv_ref, pa_ref, pb_ref, pc_ref, pd_ref = refs[n_x:]
    m = mod_ref[...]
    h = _rms(_x_tile(x_refs), g_ref[...]) * (1.0 + m[1:2]) + m[0:1]
    p = _dot_t(h.astype(BF16), w_ref[...])
    pa_ref[...] = p[:, 0:512].astype(BF16)
    pb_ref[...] = p[:, 512:1280].astype(BF16)
    pd_ref[...] = p[:, C_D:PW].astype(BF16)
    ckv_n = _rms(p[:, C_CKV:C_CKV + 128], kvn_ref[...])
    z = p[:, C_CQ:C_CQ + 256]
    cqn = z * lax.rsqrt(jnp.sum(z * z, axis=-1, keepdims=True) / MLA_Q_RANK + EPS) * qn_ref[...]
    qc = _dot(cqn.astype(BF16), wuq_ref[...])
    kv = _dot(ckv_n.astype(BF16), wukv_ref[...])
    pc_ref[:, 0:384] = qc.astype(BF16)
    pc_ref[:, 384:512] = p[:, C_KPE:C_KPE + 128].astype(BF16)
    pc_ref[:, 512:1024] = kv.astype(BF16)
    pc_ref[:, 1024:1152] = ckv_n.astype(BF16)


def _in_proj(l, xs, mods, g_pre, w_in_p, qn_p, kvn, wuq_p, wukv_p):
    row = lambda i: (i, 0)
    lyr2 = lambda i: (l, 0, 0)
    return pl.pallas_call(
        functools.partial(_in_proj_kernel, len(xs)),
        grid=(T // TM,),
        in_specs=_x_specs(xs) + [
                  pl.BlockSpec((None, None, 6, D), lambda i: (l, _mod_row(i), 0, 0)),
                  pl.BlockSpec((None, 1, D), lyr2),
                  pl.BlockSpec((None, PW, D), lyr2),
                  pl.BlockSpec((None, 1, 256), lyr2),
                  pl.BlockSpec((None, 1, 128), lyr2),
                  pl.BlockSpec((None, 256, 384), lyr2),
                  pl.BlockSpec((None, 128, 512), lyr2)],
        out_specs=[pl.BlockSpec((TM, 512), row), pl.BlockSpec((TM, 768), row),
                   pl.BlockSpec((TM, PC_W), row), pl.BlockSpec((TM, 768), row)],
        out_shape=[jax.ShapeDtypeStruct((T, 512), BF16), jax.ShapeDtypeStruct((T, 768), BF16),
                   jax.ShapeDtypeStruct((T, PC_W), BF16), jax.ShapeDtypeStruct((T, 768), BF16)],
        compiler_params=_cparams(("arbitrary",)),
        name="in_proj",
    )(*xs, mods, g_pre, w_in_p, qn_p, kvn, wuq_p, wukv_p)


def _attn_ctx_kernel(n_carried, sink_ref, lam_ref, pa, pb, pc, pd, subln_ref, *refs):
    o_ref, ka_o, va_o, kb_o, vb_o, ckv_o, kpe_o, kd_o, vd_o = refs[n_carried:]
    n = L_CTX
    low = _low_half(n)
    first = lax.broadcasted_iota(jnp.int32, (2 * n, 1), 0) < n
    ka, va = pa[:, 256:384], _with_ones(pa[:, 384:512])
    res = []
    for g in range(2):
        q = jnp.concatenate([_keep_lanes(pa[:, 0:128], (64 * g, 64)), _keep_lanes(pa[:, 128:256], (64 * g, 64))], axis=0)
        sink = jnp.where(first, sink_ref[2 * g], sink_ref[2 * g + 1])
        res.append(_attend(q, [ka], [va], sink))
    o_ref[:, 0:128] = jnp.where(low, res[0][0:n], res[1][0:n]).astype(BF16)
    o_ref[:, 128:256] = jnp.where(low, res[0][n:], res[1][n:]).astype(BF16)
    for g in range(2):
        ka_o[g] = pa[:, 256 + 64 * g:320 + 64 * g].astype(F32)
        va_o[g] = pa[:, 384 + 64 * g:448 + 64 * g].astype(F32)
    lam = lam_ref[0]
    out_scale = lam_ref[1]
    for j in range(2):
        qp = pb[:, 128 * j:128 * j + 128]
        q = jnp.concatenate([_keep_lanes(qp, (32 * i, 32)) for i in range(4)], axis=0)
        r = _attend(q, [pb[:, 256 + 128 * j:384 + 128 * j]], [_with_ones(pb[:, 512 + 128 * j:640 + 128 * j])])
        o = jnp.where(low, r[0:n] - lam * r[n:2 * n], r[2 * n:3 * n] - lam * r[3 * n:4 * n])
        o_ref[:, 256 + 128 * j:384 + 128 * j] = (_rms_halves(o, low, subln_ref[...]) * out_scale).astype(BF16)
    for h in range(4):
        kb_o[h] = pb[:, 256 + 64 * h:320 + 64 * h].astype(F32)
        vb_o[h] = pb[:, 512 + 64 * h:576 + 64 * h].astype(F32)
    kpe4 = pc[:, 384:512]
    for j in range(2):
        qn, kcat = pc[:, 128 * j:128 * j + 128], jnp.concatenate([pc[:, 512 + 128 * j:640 + 128 * j], kpe4], axis=1)
        qcat = jnp.concatenate([qn, pc[:, 256:384]], axis=1)
        q = jnp.concatenate([_keep_lanes(qcat, (64 * hh, 64), (128 + 32 * (2 * j + hh), 32)) for hh in range(2)], axis=0)
        r = _attend(q, [kcat], [_with_ones(pc[:, 768 + 128 * j:896 + 128 * j])])
        o_ref[:, 512 + 128 * j:640 + 128 * j] = jnp.where(low, r[0:n], r[n:]).astype(BF16)
    ckv_o[...] = pc[:, 1024:1152].astype(F32)
    kpe_o[...] = pc[:, 384:416].astype(F32)
    for j in range(2):
        qp = pd[:, 128 * j:128 * j + 128]
        q = jnp.concatenate([_keep_lanes(qp, (0, 64)), _keep_lanes(qp, (64, 64))], axis=0)
        r = _attend(q, [pd[:, 256 + 128 * j:384 + 128 * j]], [_with_ones(pd[:, 512 + 128 * j:640 + 128 * j])])
        o_ref[:, 768 + 128 * j:896 + 128 * j] = jnp.where(low, r[0:n], r[n:]).astype(BF16)
    for h in range(4):
        kd_o[h] = pd[:, 256 + 64 * h:320 + 64 * h].astype(F32)
        vd_o[h] = pd[:, 512 + 64 * h:576 + 64 * h].astype(F32)


def _attn_ctx(l, sink, lam, pa, pb, pc, pd, subln, carried):
    row = lambda b: (b, 0)
    smem = pl.BlockSpec(memory_space=pltpu.SMEM)
    hd4 = lambda n: pl.BlockSpec((None, None, n, L_CTX, HD), lambda b: (b, l, 0, 0, 0))
    hshape = lambda n: jax.ShapeDtypeStruct((N_CTX_B, DEPTH, n, L_CTX, HD), F32)
    n_in = 7
    return pl.pallas_call(
        functools.partial(_attn_ctx_kernel, len(carried)),
        grid=(N_CTX_B,),
        in_specs=[smem, smem,
                  pl.BlockSpec((L_CTX, 512), row), pl.BlockSpec((L_CTX, 768), row),
                  pl.BlockSpec((L_CTX, PC_W), row), pl.BlockSpec((L_CTX, 768), row),
                  pl.BlockSpec((1, 128), lambda b: (0, 0))] + [pl.BlockSpec(memory_space=pl.ANY)] * len(carried),
        out_specs=[pl.BlockSpec((L_CTX, D), row),
                   hd4(2), hd4(2), hd4(4), hd4(4),
                   pl.BlockSpec((None, None, L_CTX, 128), lambda b: (b, l, 0, 0)),
                   pl.BlockSpec((None, None, L_CTX, 32), lambda b: (b, l, 0, 0)),
                   hd4(4), hd4(4)],
        out_shape=[jax.ShapeDtypeStruct((T_CTX, D), BF16),
                   hshape(2), hshape(2), hshape(4), hshape(4),
                   jax.ShapeDtypeStruct((N_CTX_B, DEPTH, L_CTX, 128), F32),
                   jax.ShapeDtypeStruct((N_CTX_B, DEPTH, L_CTX, 32), F32),
                   hshape(4), hshape(4)],
        input_output_aliases={n_in + i: 1 + i for i in range(len(carried))},
        compiler_params=_cparams(("arbitrary",)),
        name="attn_ctx",
    )(sink, lam, pa, pb, pc, pd, subln, *carried)


_LAT_ROW = lambda b: (T_CTX // N_LAT + b, 0)


def _cache_spec(l, n):
    return pl.BlockSpec((None, None, n, L_CTX, HD), lambda b: (b, l, 0, 0, 0))


def _tab_spec():
    return pl.BlockSpec((N_LAT, 128), lambda b: (0, 0))


def _lat_win_kernel(sink_ref, pa, ck_ref, cv_ref, c_ref, sa_ref, sb_ref, o_ref, qs, kpad, vpad, kcs, vcs):
    c, sa, sb = c_ref[...], sa_ref[...], sb_ref[...]
    for j in range(2):
        qs[:, 128 * j:128 * j + 128] = _rope(pa[:, 128 * j:128 * j + 128].astype(F32), c, sa, sb, 16).astype(BF16)
    zeros = jnp.zeros((WINDOW, 128), BF16)
    kpad[0:WINDOW, :] = zeros
    kpad[WINDOW:WINDOW + N_LAT, :] = _rope(pa[:, 256:384].astype(F32), c, sa, sb, 16).astype(BF16)
    kpad[WINDOW + N_LAT:, :] = zeros
    vpad[0:WINDOW, :] = zeros
    vpad[WINDOW:WINDOW + N_LAT, :] = pa[:, 384:512]
    vpad[WINDOW + N_LAT:, :] = zeros
    for g in range(2):
        kcs[:, 64 * g:64 * g + 64] = ck_ref[g].astype(BF16)
        vcs[:, 64 * g:64 * g + 64] = cv_ref[g].astype(BF16)
    qi = lax.broadcasted_iota(jnp.int32, (2 * WINDOW, 3 * WINDOW), 0) % WINDOW
    kj = lax.broadcasted_iota(jnp.int32, (2 * WINDOW, 3 * WINDOW), 1)
    band = jnp.abs(kj - WINDOW - qi) <= WINDOW
    first_head = lax.broadcasted_iota(jnp.int32, (2 * WINDOW, 1), 0) < WINDOW
    low = _low_half(WINDOW)

    def body(nb, carry):
        r0 = pl.multiple_of(nb * WINDOW, WINDOW)
        kpos = r0 - WINDOW + kj
        valid = band & (kpos >= 0) & (kpos < N_LAT)
        kw = kpad[pl.ds(r0, 3 * WINDOW), :]
        vexts = [_with_ones(vpad[pl.ds(r0, 3 * WINDOW), :]), _with_ones(vcs[...])]
        res = []
        for g in range(2):
            q = jnp.concatenate([_keep_lanes(qs[pl.ds(r0, WINDOW), 0:128], (64 * g, 64)),
                                 _keep_lanes(qs[pl.ds(r0, WINDOW), 128:256], (64 * g, 64))], axis=0)
            s_loc = jnp.where(valid, _dot_t(q, kw), NEG_INF)
            sink = jnp.where(first_head, sink_ref[2 * g], sink_ref[2 * g + 1])
            res.append(_attend_scores([s_loc, _dot_t(q, kcs[...])], vexts, sink))
        o_ref[pl.ds(r0, WINDOW), 0:128] = jnp.where(low, res[0][0:WINDOW], res[1][0:WINDOW]).astype(BF16)
        o_ref[pl.ds(r0, WINDOW), 128:256] = jnp.where(low, res[0][WINDOW:], res[1][WINDOW:]).astype(BF16)
        return carry

    lax.fori_loop(0, N_LAT // WINDOW, body, 0)


def _lat_win(l, sink, pa, ck, cv, tabs):
    return pl.pallas_call(
        _lat_win_kernel,
        grid=(N_LAT_B,),
        in_specs=[pl.BlockSpec(memory_space=pltpu.SMEM),
                  pl.BlockSpec((N_LAT, 512), _LAT_ROW),
                  _cache_spec(l, 2), _cache_spec(l, 2),
                  _tab_spec(), _tab_spec(), _tab_spec()],
        out_specs=pl.BlockSpec((N_LAT, 256), lambda b: (b, 0)),
        out_shape=jax.ShapeDtypeStruct((T_LAT, 256), BF16),
        scratch_shapes=[pltpu.VMEM((N_LAT, 256), BF16),
                        pltpu.VMEM((N_LAT + 2 * WINDOW, 128), BF16),
                        pltpu.VMEM((N_LAT + 2 * WINDOW, 128), BF16),
                        pltpu.VMEM((L_CTX, 128), BF16),
                        pltpu.VMEM((L_CTX, 128), BF16)],
        compiler_params=_cparams(("arbitrary",)),
        name="lat_win",
    )(sink, pa, ck, cv, *tabs)


QT = 256
DIFF_QT = 256
N_KEYS = N_LAT + L_CTX


def _lat_diff_kernel(lam_ref, pb, ck_ref, cv_ref, subln_ref, c_ref, sa_ref, sb_ref, o_ref, qs, kall, vall):
    c, sa, sb = c_ref[...], sa_ref[...], sb_ref[...]
    for j in range(2):
        qs[:, 128 * j:128 * j + 128] = _rope(pb[:, 128 * j:128 * j + 128].astype(F32), c, sa, sb, 8).astype(BF16)
        kall[j, 0:N_LAT, :] = _rope(pb[:, 256 + 128 * j:384 + 128 * j].astype(F32), c, sa, sb, 8).astype(BF16)
        vall[j, 0:N_LAT, :] = pb[:, 512 + 128 * j:640 + 128 * j]
        for hh in range(2):
            kall[j, N_LAT:, 64 * hh:64 * hh + 64] = ck_ref[2 * j + hh].astype(BF16)
            vall[j, N_LAT:, 64 * hh:64 * hh + 64] = cv_ref[2 * j + hh].astype(BF16)
    lam = lam_ref[0]
    out_scale = lam_ref[1]
    low = _low_half(DIFF_QT)
    n = DIFF_QT

    def body(t, carry):
        r0 = pl.multiple_of(t * n, n)
        for j in range(2):
            qp = qs[pl.ds(r0, n), 128 * j:128 * j + 128]
            q = jnp.concatenate([_keep_lanes(qp, (32 * i, 32)) for i in range(4)], axis=0)
            r = _attend(q, [kall[j]], [_with_ones(vall[j])])
            o = jnp.where(low, r[0:n] - lam * r[n:2 * n], r[2 * n:3 * n] - lam * r[3 * n:4 * n])
            o_ref[pl.ds(r0, n), 128 * j:128 * j + 128] = (_rms_halves(o, low, subln_ref[...]) * out_scale).astype(BF16)
        return carry

    lax.fori_loop(0, N_LAT // n, body, 0)


def _lat_diff(l, lam, pb, ck, cv, subln, tabs):
    return pl.pallas_call(
        _lat_diff_kernel,
        grid=(N_LAT_B,),
        in_specs=[pl.BlockSpec(memory_space=pltpu.SMEM),
                  pl.BlockSpec((N_LAT, 768), _LAT_ROW),
                  _cache_spec(l, 4), _cache_spec(l, 4),
                  pl.BlockSpec((1, 128), lambda b: (0, 0)),
                  _tab_spec(), _tab_spec(), _tab_spec()],
        out_specs=pl.BlockSpec((N_LAT, 256), lambda b: (b, 0)),
        out_shape=jax.ShapeDtypeStruct((T_LAT, 256), BF16),
        scratch_shapes=[pltpu.VMEM((N_LAT, 256), BF16),
                        pltpu.VMEM((2, N_KEYS, 128), BF16),
                        pltpu.VMEM((2, N_KEYS, 128), BF16)],
        compiler_params=_cparams(("arbitrary",)),
        name="lat_diff",
    )(lam, pb, ck, cv, subln, *tabs)


def _lat_mla_kernel(pc, ckv_ref, kpe_ref, wukv_ref, c_ref, sa_ref, sb_ref, o_ref, qcat, kcat, vall):
    c, sa, sb = c_ref[...], sa_ref[...], sb_ref[...]
    qr = _rope(pc[:, 256:384].astype(F32), c, sa, sb, 8).astype(BF16)
    kpe_l = _rope(pc[:, 384:512].astype(F32), c, sa, sb, 8).astype(BF16)
    kvc = _dot(ckv_ref[...].astype(BF16), wukv_ref[...]).astype(BF16)
    kpe_c = kpe_ref[...].astype(BF16)
    for j in range(2):
        qcat[j, :, 0:128] = pc[:, 128 * j:128 * j + 128]
        qcat[j, :, 128:256] = qr
        kcat[j, 0:N_LAT, 0:128] = pc[:, 512 + 128 * j:640 + 128 * j]
        kcat[j, 0:N_LAT, 128:256] = kpe_l
        kcat[j, N_LAT:, 0:128] = kvc[:, 128 * j:128 * j + 128]
        for i in range(4):
            kcat[j, N_LAT:, 128 + 32 * i:160 + 32 * i] = kpe_c
        vall[j, 0:N_LAT, :] = pc[:, 768 + 128 * j:896 + 128 * j]
        vall[j, N_LAT:, :] = kvc[:, 256 + 128 * j:384 + 128 * j]
    low = _low_half(QT)

    def body(t, carry):
        r0 = pl.multiple_of(t * QT, QT)
        for j in range(2):
            q2 = qcat[j, pl.ds(r0, QT), :]
            q = jnp.concatenate([_keep_lanes(q2, (64 * hh, 64), (128 + 32 * (2 * j + hh), 32)) for hh in range(2)],
                                axis=0)
            r = _attend(q, [kcat[j]], [_with_ones(vall[j])])
            o_ref[pl.ds(r0, QT), 128 * j:128 * j + 128] = jnp.where(low, r[0:QT], r[QT:]).astype(BF16)
        return carry

    lax.fori_loop(0, N_LAT // QT, body, 0)


def _lat_mla(l, pc, ckv, kpe, wukv_p, tabs):
    return pl.pallas_call(
        _lat_mla_kernel,
        grid=(N_LAT_B,),
        in_specs=[pl.BlockSpec((N_LAT, PC_W), _LAT_ROW),
                  pl.BlockSpec((None, None, L_CTX, 128), lambda b: (b, l, 0, 0)),
                  pl.BlockSpec((None, None, L_CTX, 32), lambda b: (b, l, 0, 0)),
                  pl.BlockSpec((None, 128, 512), lambda b: (l, 0, 0)),
                  _tab_spec(), _tab_spec(), _tab_spec()],
        out_specs=pl.BlockSpec((N_LAT, 256), lambda b: (b, 0)),
        out_shape=jax.ShapeDtypeStruct((T_LAT, 256), BF16),
        scratch_shapes=[pltpu.VMEM((2, N_LAT, 256), BF16),
                        pltpu.VMEM((2, N_KEYS, 256), BF16),
                        pltpu.VMEM((2, N_KEYS, 128), BF16)],
        compiler_params=_cparams(("arbitrary",)),
        name="lat_mla",
    )(pc, ckv, kpe, wukv_p, *tabs)


NA_ROWS = 8
NA_KEYS = NA_ROWS * GRID_W


def _lat_na_kernel(pd, ck_ref, cv_ref, bias_ref, o_ref, kcs, vcs):
    n_rows = N_LAT // GRID_W
    for h in range(4):
        kcs[h // 2, :, 64 * (h % 2):64 * (h % 2) + 64] = ck_ref[h].astype(BF16)
        vcs[h // 2, :, 64 * (h % 2):64 * (h % 2) + 64] = cv_ref[h].astype(BF16)
    low = _low_half(GRID_W)

    def body(r, carry):
        rs = jnp.clip(r - NA_ROWS // 2, 0, n_rows - NA_ROWS)
        q0 = pl.multiple_of(r * GRID_W, GRID_W)
        k0 = pl.multiple_of(rs * GRID_W, GRID_W)
        for j in range(2):
            qp = pd[pl.ds(q0, GRID_W), 128 * j:128 * j + 128]
            q = jnp.concatenate([_keep_lanes(qp, (0, 64)), _keep_lanes(qp, (64, 64))], axis=0)
            s_loc = _dot_t(q, pd[pl.ds(k0, NA_KEYS), 256 + 128 * j:384 + 128 * j]) + bias_ref[j, r - rs]
            s_ctx = _dot_t(q, kcs[j])
            res = _attend_scores([s_loc, s_ctx],
                                 [_with_ones(pd[pl.ds(k0, NA_KEYS), 512 + 128 * j:640 + 128 * j]), _with_ones(vcs[j])])
            o_ref[pl.ds(q0, GRID_W), 128 * j:128 * j + 128] = jnp.where(low, res[0:GRID_W], res[GRID_W:]).astype(BF16)
        return carry

    lax.fori_loop(0, n_rows, body, 0)


def _lat_na(l, pd, ck, cv, bias):
    return pl.pallas_call(
        _lat_na_kernel,
        grid=(N_LAT_B,),
        in_specs=[pl.BlockSpec((N_LAT, 768), _LAT_ROW),
                  _cache_spec(l, 4), _cache_spec(l, 4),
                  pl.BlockSpec((None, 2, NA_ROWS, 2 * GRID_W, NA_KEYS), lambda b: (l, 0, 0, 0, 0))],
        out_specs=pl.BlockSpec((N_LAT, 256), lambda b: (b, 0)),
        out_shape=jax.ShapeDtypeStruct((T_LAT, 256), BF16),
        scratch_shapes=[pltpu.VMEM((2, L_CTX, 128), BF16), pltpu.VMEM((2, L_CTX, 128), BF16)],
        compiler_params=_cparams(("arbitrary",)),
        name="lat_na",
    )(pd, ck, cv, bias)


def _out_proj_kernel(with_router, n_x, *refs):
    x_refs = refs[:n_x]
    refs = refs[n_x:]
    if with_router:
        (oc_ref, oa_ref, ob_ref, om_ref, od_ref, mod_ref, wo_ref, gpost_ref, gffn_ref, r_ref,
         x1_ref, h2_ref, route_ref, cnt_ref) = refs
    else:
        (oc_ref, oa_ref, ob_ref, om_ref, od_ref, mod_ref, wo_ref, gpost_ref, gffn_ref,
         x1_ref, h2_ref) = refs
    i = pl.program_id(0)
    nct = T_CTX // TM

    if with_router:
        @pl.when(i == 0)
        def _():
            cnt_ref[...] = jnp.zeros(cnt_ref.shape, F32)

    def finish(y):
        m = mod_ref[...]
        x1 = _x_tile(x_refs) + m[2:3] * _rms(y, gpost_ref[...])
        x1_ref[...] = x1
        h2 = _rms(x1, gffn_ref[...]) * (1.0 + m[4:5]) + m[3:4]
        h2_ref[...] = h2.astype(h2_ref.dtype)
        if with_router:
            lg = _dot(h2.astype(BF16), r_ref[...])
            lane = lax.broadcasted_iota(jnp.int32, lg.shape, 1)
            lane_f = lane.astype(F32)
            lg = jnp.where(lane < N_EXPERTS, lg, -jnp.inf)
            m1 = jnp.max(lg, axis=-1, keepdims=True)
            i1 = jnp.min(jnp.where(lg == m1, lane_f, 128.0), axis=-1, keepdims=True)
            lg2 = jnp.where(lane_f == i1, -jnp.inf, lg)
            m2 = jnp.max(lg2, axis=-1, keepdims=True)
            i2 = jnp.min(jnp.where(lg2 == m2, lane_f, 128.0), axis=-1, keepdims=True)
            e = jnp.exp(m2 - m1)
            g1 = 1.0 / (1.0 + e)
            g2 = e / (1.0 + e)
            sel1 = lane_f == i1
            sel2 = lane_f == i2
            picked = jnp.where(sel1 | sel2, 1.0, 0.0)
            ri = lax.broadcasted_iota(jnp.int32, (TM, TM), 0)
            ci = lax.broadcasted_iota(jnp.int32, (TM, TM), 1)
            before = _dot(jnp.where(ci < ri, 1.0, 0.0).astype(BF16), picked.astype(BF16)) + cnt_ref[0:1, :]
            r1 = jnp.sum(jnp.where(sel1, before, 0.0), axis=-1, keepdims=True)
            r2 = jnp.sum(jnp.where(sel2, before, 0.0), axis=-1, keepdims=True)
            cnt_ref[...] = cnt_ref[...] + jnp.sum(picked, axis=0, keepdims=True)
            route = jnp.where(lane == 0, i1, jnp.where(lane == 1, i2, jnp.where(lane == 2, g1, 0.0)))
            route_ref[...] = jnp.where(lane == 3, g2, jnp.where(lane == 4, r1, jnp.where(lane == 5, r2, route)))

    @pl.when(i < nct)
    def _():
        finish(_dot(oc_ref[...], wo_ref[...]))

    @pl.when(i >= nct)
    def _():
        y = _dot(oa_ref[...], wo_ref[0:256, :])
        y = y + _dot(ob_ref[...], wo_ref[256:512, :])
        y = y + _dot(om_ref[...], wo_ref[512:768, :])
        y = y + _dot(od_ref[...], wo_ref[768:1024, :])
        finish(y)


def _out_proj(l, o_ctx, o_lat, xs, mods, w_o, g_post, g_ffn, router_p):
    nct = T_CTX // TM
    row = lambda i: (i, 0)
    lyr2 = lambda i: (l, 0, 0)
    lat = lambda i: (jnp.maximum(i - nct, 0), 0)
    with_router = router_p is not None
    in_specs = _x_specs(xs) + [
                pl.BlockSpec((TM, D), lambda i: (jnp.minimum(i, nct - 1), 0)),
                pl.BlockSpec((TM, 256), lat), pl.BlockSpec((TM, 256), lat),
                pl.BlockSpec((TM, 256), lat), pl.BlockSpec((TM, 256), lat),
                pl.BlockSpec((None, None, 6, D), lambda i: (l, _mod_row(i), 0, 0)),
                pl.BlockSpec((None, D, D), lyr2),
                pl.BlockSpec((None, 1, D), lyr2),
                pl.BlockSpec((None, 1, D), lyr2)]
    out_specs = [pl.BlockSpec((TM, D), row), pl.BlockSpec((TM, D), row)]
    out_shape = [jax.ShapeDtypeStruct((T, D), F32), jax.ShapeDtypeStruct((T, D), F32 if with_router else BF16)]
    args = [*xs, o_ctx, *o_lat, mods, w_o, g_post, g_ffn]
    if with_router:
        in_specs.append(pl.BlockSpec((D, 128), lambda i: (0, 0)))
        out_specs += [pl.BlockSpec((TM, 128), row), pl.BlockSpec((8, 128), lambda i: (0, 0))]
        out_shape += [jax.ShapeDtypeStruct((T, 128), F32), jax.ShapeDtypeStruct((8, 128), F32)]
        args.append(router_p)
    return pl.pallas_call(
        functools.partial(_out_proj_kernel, with_router, len(xs)),
        grid=(T // TM,),
        in_specs=in_specs, out_specs=out_specs, out_shape=out_shape,
        compiler_params=_cparams(("arbitrary",)),
        name="out_proj_router" if with_router else "out_proj",
    )(*args)


FFN_TM = 1024


def _ffn_kernel(h_ref, x_ref, mod_ref, g_ref, wg_ref, wu_ref, wd_ref, o_ref, acc_ref):
    c = pl.program_id(1)
    h = h_ref[...]
    t = (_silu(_dot(h, wg_ref[...])) * _dot(h, wu_ref[...])).astype(BF16)
    part = _dot(t, wd_ref[...])

    @pl.when(c == 0)
    def _():
        acc_ref[...] = part

    @pl.when(c > 0)
    def _():
        acc_ref[...] += part

    @pl.when(c == pl.num_programs(1) - 1)
    def _():
        m = mod_ref[...]
        o_ref[...] = x_ref[...] + m[5:6] * _rms(acc_ref[...], g_ref[...])


def _ffn_mod_row(i):
    nct = T_CTX // FFN_TM
    return jnp.where(i >= nct, 1 + (i - nct) // (N_LAT // FFN_TM), 0)


def _ffn(l, i_dense, h2, x1, mods, g_post, wg, wu, wd):
    row = lambda i, c: (i, 0)
    return pl.pallas_call(
        _ffn_kernel,
        grid=(T // FFN_TM, D_FF // FFN_CHUNK),
        in_specs=[pl.BlockSpec((FFN_TM, D), row),
                  pl.BlockSpec((FFN_TM, D), row),
                  pl.BlockSpec((None, None, 6, D), lambda i, c: (l, _ffn_mod_row(i), 0, 0)),
                  pl.BlockSpec((None, 1, D), lambda i, c: (l, 0, 0)),
                  pl.BlockSpec((None, D, FFN_CHUNK), lambda i, c: (i_dense, 0, c)),
                  pl.BlockSpec((None, D, FFN_CHUNK), lambda i, c: (i_dense, 0, c)),
                  pl.BlockSpec((None, FFN_CHUNK, D), lambda i, c: (i_dense, c, 0))],
        out_specs=pl.BlockSpec((FFN_TM, D), row),
        out_shape=jax.ShapeDtypeStruct((T, D), F32),
        scratch_shapes=[pltpu.VMEM((FFN_TM, D), F32)],
        compiler_params=_cparams(("arbitrary", "arbitrary")),
        name="ffn",
    )(h2, x1, mods, g_post, wg, wu, wd)


def _dispatch_kernel(slot_ref, h_ref, xs_hbm, sem):
    def row_copy(g, r, k):
        return pltpu.make_async_copy(h_ref.at[g, pl.ds(r, 1), :],
                                     xs_hbm.at[pl.ds(slot_ref[0, 16 * g + 2 * r + k], 1)], sem)

    def issue(g, carry):
        for r in range(8):
            row_copy(g, r, 0).start()
            row_copy(g, r, 1).start()
        return carry

    def drain(g, carry):
        for r in range(8):
            row_copy(g, r, 0).wait()
            row_copy(g, r, 1).wait()
        return carry

    lax.fori_loop(0, DISPATCH_TM // 8, issue, 0)
    lax.fori_loop(0, DISPATCH_TM // 8, drain, 0)


def _dispatch(slots, h2):
    return pl.pallas_call(
        _dispatch_kernel,
        grid=(T // DISPATCH_TM,),
        in_specs=[pl.BlockSpec((None, 1, 2 * DISPATCH_TM), lambda i: (i, 0, 0), memory_space=pltpu.SMEM),
                  pl.BlockSpec((DISPATCH_TM // 8, 8, D), lambda i: (i, 0, 0))],
        out_specs=pl.BlockSpec(memory_space=pl.ANY),
        out_shape=jax.ShapeDtypeStruct((N_EXPERTS * EXPERT_CAP, D), F32),
        scratch_shapes=[pltpu.SemaphoreType.DMA(())],
        compiler_params=_cparams(("arbitrary",)),
        name="moe_dispatch",
    )(slots.reshape(T // DISPATCH_TM, 1, 2 * DISPATCH_TM), h2.reshape(T // 8, 8, D))


def _moe_kernel(ie_ref, ib_ref, nr_ref, x_ref, wg_ref, wu_ref, wd_ref, o_ref, wgb, wub, wdb):
    i = pl.program_id(0)
    c = pl.program_id(1)
    n_rows = nr_ref[i]

    @pl.when(n_rows > 0)
    def _():
        @pl.when(c == 0)
        def _():
            o_ref[...] = jnp.zeros(o_ref.shape, F32)

        wgb[...] = wg_ref[...].astype(BF16)
        wub[...] = wu_ref[...].astype(BF16)
        wdb[...] = wd_ref[...].astype(BF16)

        def step(r0, rows):
            live = r0 + lax.broadcasted_iota(jnp.int32, (rows, 1), 0) < n_rows
            xs = jnp.where(live, x_ref[pl.ds(r0, rows), :], 0.0).astype(BF16)
            t = (_silu(_dot(xs, wgb[...])) * _dot(xs, wub[...])).astype(BF16)
            o_ref[pl.ds(r0, rows), :] += _dot(t, wdb[...])

        whole = lax.shift_right_logical(n_rows, SUB_ROWS.bit_length() - 1)
        left = jnp.bitwise_and(n_rows, SUB_ROWS - 1)
        n_full = whole + (left > TAIL_ROWS).astype(jnp.int32)

        def body(s, carry):
            step(pl.multiple_of(s * SUB_ROWS, SUB_ROWS), SUB_ROWS)
            return carry

        lax.fori_loop(0, n_full, body, 0)

        @pl.when((left > 0) & (left <= TAIL_ROWS))
        def _():
            step(pl.multiple_of(whole * SUB_ROWS, SUB_ROWS), TAIL_ROWS)


def _moe(i_moe, item_e, item_blk, item_rows, x_sorted, wg, wu, wd):
    nc = D_FF_EXPERT // FF_CHUNK
    chunk = lambda i, c, nr: jnp.where(nr[i] > 0, c, nc - 1)
    grid_spec = pltpu.PrefetchScalarGridSpec(
        num_scalar_prefetch=3,
        grid=(N_ITEMS, nc),
        in_specs=[pl.BlockSpec((ITEM_ROWS, D), lambda i, c, ie, ib, nr: (ib[i], 0)),
                  pl.BlockSpec((None, None, D, FF_CHUNK),
                               lambda i, c, ie, ib, nr: (i_moe, ie[i], 0, chunk(i, c, nr))),
                  pl.BlockSpec((None, None, D, FF_CHUNK),
                               lambda i, c, ie, ib, nr: (i_moe, ie[i], 0, chunk(i, c, nr))),
                  pl.BlockSpec((None, None, FF_CHUNK, D),
                               lambda i, c, ie, ib, nr: (i_moe, ie[i], chunk(i, c, nr), 0))],
        out_specs=pl.BlockSpec((ITEM_ROWS, D), lambda i, c, ie, ib, nr: (ib[i], 0)),
        scratch_shapes=[pltpu.VMEM((D, FF_CHUNK), BF16), pltpu.VMEM((D, FF_CHUNK), BF16),
                        pltpu.VMEM((FF_CHUNK, D), BF16)],
    )
    return pl.pallas_call(
        _moe_kernel,
        grid_spec=grid_spec,
        out_shape=jax.ShapeDtypeStruct((N_EXPERTS * EXPERT_CAP, D), F32),
        compiler_params=_cparams(("arbitrary", "arbitrary")),
        name="moe_experts",
    )(item_e, item_blk, item_rows, x_sorted, wg, wu, wd)


def _post_kernel(final, slot_ref, ys_hbm, route_ref, x_ref, mod_ref, g_ref, *refs):
    buf, sem = refs[-2:]

    def row_copy(g, r, k):
        return pltpu.make_async_copy(ys_hbm.at[pl.ds(slot_ref[0, 16 * g + 2 * r + k], 1)],
                                     buf.at[k, g, pl.ds(r, 1), :], sem)

    def issue(g, carry):
        for r in range(8):
            row_copy(g, r, 0).start()
            row_copy(g, r, 1).start()
        return carry

    def drain(g, carry):
        for r in range(8):
            row_copy(g, r, 0).wait()
            row_copy(g, r, 1).wait()
        return carry

    lax.fori_loop(0, POST_TM // 8, issue, 0)
    lax.fori_loop(0, POST_TM // 8, drain, 0)
    r = route_ref[...]
    y = r[:, 2:3] * buf[0].reshape(POST_TM, D) + r[:, 3:4] * buf[1].reshape(POST_TM, D)
    m = mod_ref[...]
    out = x_ref[...] + m[5:6] * _rms(y, g_ref[...])
    if not final:
        refs[0][...] = out
    else:
        i = pl.program_id(0)

        @pl.when(i < T_CTX // POST_TM)
        def _():
            refs[0][...] = out

        @pl.when(i >= T_CTX // POST_TM)
        def _():
            refs[1][...] = out


def _post(l, slots, y_sorted, route, x1, mods, g_post, final):
    row = lambda i: (i, 0)
    nct = T_CTX // POST_TM
    if final:
        out_specs = [pl.BlockSpec((POST_TM, D), lambda i: (jnp.minimum(i, nct - 1), 0)),
                     pl.BlockSpec((POST_TM, D), lambda i: (jnp.maximum(i - nct, 0), 0))]
        out_shape = [jax.ShapeDtypeStruct((T_CTX, D), F32), jax.ShapeDtypeStruct((T_LAT, D), F32)]
    else:
        out_specs = pl.BlockSpec((POST_TM, D), row)
        out_shape = jax.ShapeDtypeStruct((T, D), F32)
    return pl.pallas_call(
        functools.partial(_post_kernel, final),
        grid=(T // POST_TM,),
        in_specs=[pl.BlockSpec((None, 1, 2 * POST_TM), lambda i: (i, 0, 0), memory_space=pltpu.SMEM),
                  pl.BlockSpec(memory_space=pl.ANY),
                  pl.BlockSpec((POST_TM, 128), row), pl.BlockSpec((POST_TM, D), row),
                  pl.BlockSpec((None, None, 6, D), lambda i: (l, _ffn_mod_row(i), 0, 0)),
                  pl.BlockSpec((None, 1, D), lambda i: (l, 0, 0))],
        out_specs=out_specs,
        out_shape=out_shape,
        scratch_shapes=[pltpu.VMEM((2, POST_TM // 8, 8, D), F32), pltpu.SemaphoreType.DMA(())],
        compiler_params=_cparams(("arbitrary",)),
        name="moe_post_final" if final else "moe_post",
    )(slots.reshape(T // POST_TM, 1, 2 * POST_TM), y_sorted, route, x1, mods, g_post)


def _route_plan(route, counts):
    i32 = jnp.int32
    slots = (route[:, 0:2].astype(i32) * EXPERT_CAP + route[:, 4:6].astype(i32)).reshape(-1)
    n = counts[0, :N_EXPERTS].astype(i32)
    n_items = (n + ITEM_ROWS - 1) // ITEM_ROWS
    ends = jnp.cumsum(n_items)
    base = ends - n_items
    total = ends[-1]
    ii = jnp.arange(N_ITEMS, dtype=i32)
    ii_c = jnp.minimum(ii, total - 1)
    item_e = jnp.sum((ends[None, :] <= ii_c[:, None]).astype(i32), axis=1)
    j = ii_c - base[item_e]
    item_blk = item_e * (EXPERT_CAP // ITEM_ROWS) + j
    rows = jnp.where(ii < total, jnp.clip(n[item_e] - j * ITEM_ROWS, 0, ITEM_ROWS), 0)
    return slots, item_e.astype(i32), item_blk.astype(i32), rows.astype(i32)


def _rope_tables(d):
    half, quarter = d // 2, d // 4
    t = jnp.arange(N_LAT)
    pr, pc = t // GRID_W, t % GRID_W
    i = jnp.arange(128) % d
    j = (i % half) % quarter
    inv = ROPE_THETA ** (-(2 * j).astype(F32) / half)
    pos = jnp.where((i < half)[None, :], pr[:, None], pc[:, None]).astype(F32)
    ang = pos * inv[None, :]
    low = ((i % half) < quarter)[None, :]
    sin = jnp.sin(ang)
    return jnp.cos(ang), jnp.where(low, -sin, 0.0), jnp.where(low, 0.0, sin)


def _na_bias_table(rpb):
    q = jnp.arange(GRID_W)
    k = jnp.arange(GRID_W)
    rel_c = jnp.clip(k[None, :] - q[:, None], -15, 15) + 15
    cs = jnp.clip(q - 8, 0, GRID_W - 16)
    valid = (k[None, :] >= cs[:, None]) & (k[None, :] < cs[:, None] + 16)
    pick = rel_c[:, :, None] == jnp.arange(31)[None, None, :]
    band = jnp.sum(jnp.where(pick[None, None, None], rpb[:, :, :, None, None, :], 0.0), axis=-1)
    band = jnp.where(valid[None, None, None], band, NEG_INF)
    tab = jnp.stack([band[:, :, NA_ROWS - 1 - dd:2 * NA_ROWS - 1 - dd] for dd in range(NA_ROWS)], axis=2)
    tab = tab.transpose(0, 1, 2, 4, 3, 5)
    tab = tab.reshape(DEPTH, 2, 2, NA_ROWS, GRID_W, NA_KEYS).transpose(0, 1, 3, 2, 4, 5)
    return tab.reshape(DEPTH, 2, NA_ROWS, 2 * GRID_W, NA_KEYS)


def _pad_in_proj(w_in):
    wt = jnp.swapaxes(w_in, 1, 2)
    z = lambda n: jnp.zeros((DEPTH, n, D), w_in.dtype)
    s64, s32 = HD ** -0.5, (HD // 2) ** -0.5
    qa = wt[:, 0:256] * s64
    kpe = wt[:, 1600:1632]
    return jnp.concatenate([qa[:, 0:64], qa[:, 128:192], qa[:, 64:128], qa[:, 192:256],
                            wt[:, 256:512],
                            wt[:, 512:768] * s32, wt[:, 768:1280],
                            wt[:, 1472:1600], kpe, kpe, kpe, kpe,
                            wt[:, 1280:1472], z(64),
                            wt[:, 1632:1888] * s64, wt[:, 1888:2400]], axis=1).astype(BF16)


def _split_heads_cols(w, first, second):
    lead = w.shape[:-1]
    w4 = w.reshape(*lead, 4, first + second)
    return jnp.concatenate([w4[..., :first].reshape(*lead, 4 * first),
                            w4[..., first:].reshape(*lead, 4 * second)], axis=-1)


def kernel(x_prompt, x_sample, cache_win_k, cache_win_v, cache_diff_k, cache_diff_v, cache_mla_ckv, cache_mla_kpe, cache_na_k, cache_na_v, c, c_ctx, w_ada, b_ada, norm_mix_pre, norm_mix_post, norm_ffn_pre, norm_ffn_post, w_in, w_o, win_sink, diff_lq1, diff_lk1, diff_lq2, diff_lk2, diff_subln, mla_q_norm, mla_kv_norm, mla_wuq, mla_wukv, na_rpb, ffn_w_gate, ffn_w_up, ffn_w_down, moe_router, moe_w_gate, moe_w_up, moe_w_down):
    w_in_p = _pad_in_proj(w_in)
    wuq_p = jnp.pad(_split_heads_cols(mla_wuq, 64, 32) * (HD + HD // 2) ** -0.5,
                    ((0, 0), (0, 256 - MLA_Q_RANK), (0, 0))).astype(BF16)
    wukv_p = _split_heads_cols(mla_wukv, 64, 64).astype(BF16)
    qn_p = jnp.pad(mla_q_norm, ((0, 0), (0, 256 - MLA_Q_RANK))).reshape(DEPTH, 1, 256)
    kvn = mla_kv_norm.reshape(DEPTH, 1, 128)
    w_o_b = jnp.concatenate([w_o[:, 0:64], w_o[:, 128:192], w_o[:, 64:128], w_o[:, 192:]], axis=1).astype(BF16)
    ffn_wg, ffn_wu, ffn_wd = ffn_w_gate.astype(BF16), ffn_w_up.astype(BF16), ffn_w_down.astype(BF16)
    router_p = jnp.pad(moe_router, ((0, 0), (0, 0), (0, 128 - N_EXPERTS))).astype(BF16)
    g_mix_pre = norm_mix_pre.reshape(DEPTH, 1, D)
    g_mix_post = norm_mix_post.reshape(DEPTH, 1, D)
    g_ffn_pre = norm_ffn_pre.reshape(DEPTH, 1, D)
    g_ffn_post = norm_ffn_post.reshape(DEPTH, 1, D)
    tabs64 = _rope_tables(64)
    tabs32 = _rope_tables(32)
    na_bias = _na_bias_table(na_rpb)
    lam_init = jnp.asarray([0.8 - 0.6 * math.exp(-0.3 * l) for l in range(DEPTH)], F32)
    lam = (jnp.exp(jnp.sum(diff_lq1 * diff_lk1, axis=-1)) - jnp.exp(jnp.sum(diff_lq2 * diff_lk2, axis=-1))
           + lam_init)
    lam_tab = jnp.stack([lam, 1.0 - lam_init], axis=-1)
    subln = jnp.tile(diff_subln, (1, 2)).reshape(DEPTH, 1, 2 * HD)

    cond = jnp.concatenate([c_ctx[None], c, jnp.zeros((3, D), F32)], axis=0)
    mods = _mods(cond, w_ada, b_ada).reshape(DEPTH, 8, 6, D)

    xs = (x_prompt.reshape(T_CTX, D), x_sample.reshape(T_LAT, D))
    caches = ()
    for l in range(DEPTH):
        pa, pb, pc, pd = _in_proj(l, xs, mods, g_mix_pre, w_in_p, qn_p, kvn, wuq_p, wukv_p)
        o_ctx, *caches = _attn_ctx(l, win_sink[l], lam_tab[l], pa, pb, pc, pd, subln[l], caches)
        o_lat = [_lat_win(l, win_sink[l], pa, cache_win_k, cache_win_v, tabs64),
                 _lat_diff(l, lam_tab[l], pb, cache_diff_k, cache_diff_v, subln[l], tabs32),
                 _lat_mla(l, pc, cache_mla_ckv, cache_mla_kpe, wukv_p, tabs32),
                 _lat_na(l, pd, cache_na_k, cache_na_v, na_bias)]
        if l % 2 == 0:
            x1, h2 = _out_proj(l, o_ctx, o_lat, xs, mods, w_o_b, g_mix_post, g_ffn_pre, None)
            xs = (_ffn(l, l // 2, h2, x1, mods, g_ffn_post, ffn_wg, ffn_wu, ffn_wd),)
        else:
            x1, h2, route, counts = _out_proj(l, o_ctx, o_lat, xs, mods, w_o_b, g_mix_post, g_ffn_pre,
                                              router_p[l // 2])
            slots, item_e, item_blk, item_rows = _route_plan(route, counts)
            x_sorted = _dispatch(slots, h2)
            y_sorted = _moe(l // 2, item_e, item_blk, item_rows, x_sorted, moe_w_gate, moe_w_up, moe_w_down)
            xs = _post(l, slots, y_sorted, route, x1, mods, g_ffn_post, final=l == DEPTH - 1)
            xs = tuple(xs) if l == DEPTH - 1 else (xs,)

    y_prompt, y_sample = xs
    return (y_prompt.reshape(N_CTX_B, L_CTX, D), y_sample.reshape(N_LAT_B, N_LAT, D)) + tuple(caches)
```

```python
import functools
import math

import jax
import jax.numpy as jnp
from jax import lax
from jax.experimental import pallas as pl
from jax.experimental.pallas import tpu as pltpu

F32 = jnp.float32
BF16 = jnp.bfloat16

D = 1024
DEPTH = 4
N_CTX_B, L_CTX = 16, 256
N_LAT_B, N_LAT = 4, 1024
T_CTX = N_CTX_B * L_CTX
T_LAT = N_LAT_B * N_LAT
T = T_CTX + T_LAT
HD = 64
GRID_W = 64
WINDOW = 128
EPS = 1e-6
NEG_INF = -1e30
ROPE_THETA = 10000.0
MLA_Q_RANK = 192
D_FF = 2816
N_EXPERTS = 8
D_FF_EXPERT = 3584

PW = 2560
C_CKV, C_KPE, C_CQ, C_D = 1280, 1408, 1536, 1792
PC_W = 1152

TM = 512
EXPERT_CAP = T
ITEM_ROWS = 1024
SUB_ROWS = 512
TAIL_ROWS = 256
N_ITEMS = N_EXPERTS + (2 * T) // ITEM_ROWS
FF_CHUNK = 512
DISPATCH_TM = 1024
POST_TM = 1024
FFN_SLICES = (1024, 1024, 768)
VMEM_LIMIT = 56 * 1024 * 1024


def _cparams(sem):
    return pltpu.CompilerParams(dimension_semantics=sem, vmem_limit_bytes=VMEM_LIMIT)


def _rms(x, g):
    return x * lax.rsqrt(jnp.mean(x * x, axis=-1, keepdims=True) + EPS) * g


def _silu(a):
    return a / (1.0 + jnp.exp(-a))


def _dot(a, b):
    return jnp.dot(a, b, preferred_element_type=F32)


def _dot_t(a, b):
    return lax.dot_general(a, b, (((1,), (1,)), ((), ())), preferred_element_type=F32)


def _with_ones(v):
    return jnp.concatenate([v, jnp.ones(v.shape, v.dtype)], axis=1)


def _with_ones_t(vt):
    return jnp.concatenate([vt, jnp.ones(vt.shape, vt.dtype)], axis=0)


class _T:
    def __init__(self, a):
        self.a = a


def _scores(q, k):
    return _dot(q, k.a) if isinstance(k, _T) else _dot_t(q, k)


def _attend_scores(scores, vexts, extra_logit=None):
    mx = jnp.max(scores[0], axis=-1, keepdims=True)
    for s in scores[1:]:
        mx = jnp.maximum(mx, jnp.max(s, axis=-1, keepdims=True))
    if extra_logit is not None:
        mx = jnp.maximum(mx, extra_logit)
    r = None
    for s, v in zip(scores, vexts):
        p = jnp.exp(s - mx).astype(BF16)
        part = _dot_t(p, v.a) if isinstance(v, _T) else _dot(p, v)
        r = part if r is None else r + part
    den = r[:, 128:256]
    if extra_logit is not None:
        den = den + jnp.exp(extra_logit - mx)
    return r[:, 0:128] / den


def _attend(q, keys, vexts, extra_logit=None):
    return _attend_scores([_scores(q, k) for k in keys], vexts, extra_logit)


def _attend_diff(qp, keys, vexts, lam, low):
    n = qp.shape[0]
    q = jnp.concatenate([_keep_lanes(qp, (32 * i, 32)) for i in range(4)], axis=0)
    r = _attend(q, keys, vexts)
    return jnp.where(low, r[0:n] - lam * r[n:2 * n], r[2 * n:3 * n] - lam * r[3 * n:4 * n])


def _keep_lanes(x, *ranges):
    lane = lax.broadcasted_iota(jnp.int32, x.shape, 1)
    keep = None
    for lo, width in ranges:
        k = (lane >= lo) & (lane < lo + width)
        keep = k if keep is None else keep | k
    return jnp.where(keep, x, jnp.zeros_like(x))


def _low_half(n):
    return lax.broadcasted_iota(jnp.int32, (n, 128), 1) < HD


def _rms_halves(o, low, g):
    sq = o * o
    ms = jnp.where(low, jnp.sum(jnp.where(low, sq, 0.0), axis=-1, keepdims=True),
                   jnp.sum(jnp.where(low, 0.0, sq), axis=-1, keepdims=True)) / HD
    return o * lax.rsqrt(ms + EPS) * g


def _rope(x, c, sa, sb, shift):
    return x * c + pltpu.roll(x, 128 - shift, 1) * sa + pltpu.roll(x, shift, 1) * sb


def _mod_row(i):
    nct = T_CTX // TM
    per_b = N_LAT // TM
    return jnp.where(i >= nct, 1 + (i - nct) // per_b, 0)


def _mods_kernel(cond_ref, w_ref, b_ref, o_ref):
    a = _silu(cond_ref[...]).astype(BF16)
    o_ref[...] = _dot(a, w_ref[...].astype(BF16)) + b_ref[...]


def _mods(cond, w_ada, b_ada):
    tn = 1536
    n = 6 * D
    return pl.pallas_call(
        _mods_kernel,
        grid=(DEPTH, n // tn),
        in_specs=[pl.BlockSpec((8, D), lambda l, j: (0, 0)),
                  pl.BlockSpec((None, D, tn), lambda l, j: (l, 0, j)),
                  pl.BlockSpec((None, 1, tn), lambda l, j: (l, 0, j))],
        out_specs=pl.BlockSpec((None, 8, tn), lambda l, j: (l, 0, j)),
        out_shape=jax.ShapeDtypeStruct((DEPTH, 8, n), F32),
        compiler_params=_cparams(("arbitrary", "arbitrary")),
        name="mods",
    )(cond, w_ada, b_ada.reshape(DEPTH, 1, n))


def _x_specs(xs):
    if len(xs) == 1:
        return [pl.BlockSpec((TM, D), lambda i: (i, 0))]
    nct = T_CTX // TM
    return [pl.BlockSpec((TM, D), lambda i: (jnp.minimum(i, nct - 1), 0)),
            pl.BlockSpec((TM, D), lambda i: (jnp.maximum(i - nct, 0), 0))]


def _x_tile(x_refs):
    if len(x_refs) == 1:
        return x_refs[0][...]
    return jnp.where(pl.program_id(0) < T_CTX // TM, x_refs[0][...], x_refs[1][...])


def _in_proj_kernel(n_x, *refs):
    x_refs = refs[:n_x]
    mod_ref, g_ref, w_ref, qn_ref, kvn_ref, wuq_ref, wukv_ref, pa_ref, pb_ref, pc_ref, pd_ref = refs[n_x:]
    m = mod_ref[...]
    h = _rms(_x_tile(x_refs), g_ref[...]) * (1.0 + m[1:2]) + m[0:1]
    p = _dot_t(h.astype(BF16), w_ref[...])
    pa_ref[...] = p[:, 0:512].astype(BF16)
    pb_ref[...] = p[:, 512:1280].astype(BF16)
    pd_ref[...] = p[:, C_D:PW].astype(BF16)
    ckv_n = _rms(p[:, C_CKV:C_CKV + 128], kvn_ref[...])
    z = p[:, C_CQ:C_CQ + 256]
    cqn = z * lax.rsqrt(jnp.sum(z * z, axis=-1, keepdims=True) / MLA_Q_RANK + EPS) * qn_ref[...]
    qc = _dot(cqn.astype(BF16), wuq_ref[...])
    kv = _dot(ckv_n.astype(BF16), wukv_ref[...])
    pc_ref[:, 0:384] = qc.astype(BF16)
    pc_ref[:, 384:512] = p[:, C_KPE:C_KPE + 128].astype(BF16)
    pc_ref[:, 512:1024] = kv.astype(BF16)
    pc_ref[:, 1024:1152] = ckv_n.astype(BF16)


def _in_proj(l, xs, mods, g_pre, w_in_p, qn_p, kvn, wuq_p, wukv_p):
    row = lambda i: (i, 0)
    lyr2 = lambda i: (l, 0, 0)
    return pl.pallas_call(
        functools.partial(_in_proj_kernel, len(xs)),
        grid=(T // TM,),
        in_specs=_x_specs(xs) + [
                  pl.BlockSpec((None, None, 6, D), lambda i: (l, _mod_row(i), 0, 0)),
                  pl.BlockSpec((None, 1, D), lyr2),
                  pl.BlockSpec((None, PW, D), lyr2),
                  pl.BlockSpec((None, 1, 256), lyr2),
                  pl.BlockSpec((None, 1, 128), lyr2),
                  pl.BlockSpec((None, 256, 384), lyr2),
                  pl.BlockSpec((None, 128, 512), lyr2)],
        out_specs=[pl.BlockSpec((TM, 512), row), pl.BlockSpec((TM, 768), row),
                   pl.BlockSpec((TM, PC_W), row), pl.BlockSpec((TM, 768), row)],
        out_shape=[jax.ShapeDtypeStruct((T, 512), BF16), jax.ShapeDtypeStruct((T, 768), BF16),
                   jax.ShapeDtypeStruct((T, PC_W), BF16), jax.ShapeDtypeStruct((T, 768), BF16)],
        compiler_params=_cparams(("arbitrary",)),
        name="in_proj",
    )(*xs, mods, g_pre, w_in_p, qn_p, kvn, wuq_p, wukv_p)


def _attn_ctx_kernel(n_carried, sink_ref, lam_ref, pa, pb, pc, pd, subln_ref, *refs):
    o_ref, ka_o, va_o, kb_o, vb_o, ckv_o, kpe_o, kd_o, vd_o = refs[n_carried:]
    n = L_CTX
    low = _low_half(n)
    first = lax.broadcasted_iota(jnp.int32, (2 * n, 1), 0) < n
    ka, va = pa[:, 256:384], _with_ones(pa[:, 384:512])
    res = []
    for g in range(2):
        q = jnp.concatenate([_keep_lanes(pa[:, 0:128], (64 * g, 64)), _keep_lanes(pa[:, 128:256], (64 * g, 64))], axis=0)
        sink = jnp.where(first, sink_ref[2 * g], sink_ref[2 * g + 1])
        res.append(_attend(q, [ka], [va], sink))
    o_ref[:, 0:128] = jnp.where(low, res[0][0:n], res[1][0:n]).astype(BF16)
    o_ref[:, 128:256] = jnp.where(low, res[0][n:], res[1][n:]).astype(BF16)
    for g in range(2):
        ka_o[g] = pa[:, 256 + 64 * g:320 + 64 * g].astype(F32)
        va_o[g] = pa[:, 384 + 64 * g:448 + 64 * g].astype(F32)
    lam = lam_ref[0]
    out_scale = lam_ref[1]
    for j in range(2):
        o = _attend_diff(pb[:, 128 * j:128 * j + 128], [pb[:, 256 + 128 * j:384 + 128 * j]],
                         [_with_ones(pb[:, 512 + 128 * j:640 + 128 * j])], lam, low)
        o_ref[:, 256 + 128 * j:384 + 128 * j] = (_rms_halves(o, low, subln_ref[...]) * out_scale).astype(BF16)
    for h in range(4):
        kb_o[h] = pb[:, 256 + 64 * h:320 + 64 * h].astype(F32)
        vb_o[h] = pb[:, 512 + 64 * h:576 + 64 * h].astype(F32)
    kpe4 = pc[:, 384:512]
    for j in range(2):
        qn, kcat = pc[:, 128 * j:128 * j + 128], jnp.concatenate([pc[:, 512 + 128 * j:640 + 128 * j], kpe4], axis=1)
        qcat = jnp.concatenate([qn, pc[:, 256:384]], axis=1)
        q = jnp.concatenate([_keep_lanes(qcat, (64 * hh, 64), (128 + 32 * (2 * j + hh), 32)) for hh in range(2)], axis=0)
        r = _attend(q, [kcat], [_with_ones(pc[:, 768 + 128 * j:896 + 128 * j])])
        o_ref[:, 512 + 128 * j:640 + 128 * j] = jnp.where(low, r[0:n], r[n:]).astype(BF16)
    ckv_o[...] = pc[:, 1024:1152].astype(F32)
    kpe_o[...] = pc[:, 384:416].astype(F32)
    for j in range(2):
        qp = pd[:, 128 * j:128 * j + 128]
        q = jnp.concatenate([_keep_lanes(qp, (0, 64)), _keep_lanes(qp, (64, 64))], axis=0)
        r = _attend(q, [pd[:, 256 + 128 * j:384 + 128 * j]], [_with_ones(pd[:, 512 + 128 * j:640 + 128 * j])])
        o_ref[:, 768 + 128 * j:896 + 128 * j] = jnp.where(low, r[0:n], r[n:]).astype(BF16)
    for h in range(4):
        kd_o[h] = pd[:, 256 + 64 * h:320 + 64 * h].astype(F32)
        vd_o[h] = pd[:, 512 + 64 * h:576 + 64 * h].astype(F32)


def _attn_ctx(l, sink, lam, pa, pb, pc, pd, subln, carried):
    row = lambda b: (b, 0)
    smem = pl.BlockSpec(memory_space=pltpu.SMEM)
    hd4 = lambda n: pl.BlockSpec((None, None, n, L_CTX, HD), lambda b: (b, l, 0, 0, 0))
    hshape = lambda n: jax.ShapeDtypeStruct((N_CTX_B, DEPTH, n, L_CTX, HD), F32)
    n_in = 7
    return pl.pallas_call(
        functools.partial(_attn_ctx_kernel, len(carried)),
        grid=(N_CTX_B,),
        in_specs=[smem, smem,
                  pl.BlockSpec((L_CTX, 512), row), pl.BlockSpec((L_CTX, 768), row),
                  pl.BlockSpec((L_CTX, PC_W), row), pl.BlockSpec((L_CTX, 768), row),
                  pl.BlockSpec((1, 128), lambda b: (0, 0))] + [pl.BlockSpec(memory_space=pl.ANY)] * len(carried),
        out_specs=[pl.BlockSpec((L_CTX, D), row),
                   hd4(2), hd4(2), hd4(4), hd4(4),
                   pl.BlockSpec((None, None, L_CTX, 128), lambda b: (b, l, 0, 0)),
                   pl.BlockSpec((None, None, L_CTX, 32), lambda b: (b, l, 0, 0)),
                   hd4(4), hd4(4)],
        out_shape=[jax.ShapeDtypeStruct((T_CTX, D), BF16),
                   hshape(2), hshape(2), hshape(4), hshape(4),
                   jax.ShapeDtypeStruct((N_CTX_B, DEPTH, L_CTX, 128), F32),
                   jax.ShapeDtypeStruct((N_CTX_B, DEPTH, L_CTX, 32), F32),
                   hshape(4), hshape(4)],
        input_output_aliases={n_in + i: 1 + i for i in range(len(carried))},
        compiler_params=_cparams(("arbitrary",)),
        name="attn_ctx",
    )(sink, lam, pa, pb, pc, pd, subln, *carried)


_LAT_ROW = lambda b: (T_CTX // N_LAT + b, 0)


def _cache_spec(l, n):
    return pl.BlockSpec((None, None, n, HD, L_CTX), lambda b: (b, l, 0, 0, 0))


def _load_cache_pair(ck_ref, cv_ref, j, kcs, vcs):
    for hh in range(2):
        kcs[64 * hh:64 * hh + 64, :] = ck_ref[2 * j + hh].astype(BF16)
        vcs[64 * hh:64 * hh + 64, :] = cv_ref[2 * j + hh].astype(BF16)
    vcs[128:256, :] = jnp.ones((128, L_CTX), BF16)


def _tab_spec():
    return pl.BlockSpec((N_LAT, 128), lambda b: (0, 0))


def _lat_win_kernel(sink_ref, pa, ck_ref, cv_ref, c_ref, sa_ref, sb_ref, o_ref, qs, kpad, vpad, kcs, vcs):
    c, sa, sb = c_ref[...], sa_ref[...], sb_ref[...]
    for j in range(2):
        qs[:, 128 * j:128 * j + 128] = _rope(pa[:, 128 * j:128 * j + 128].astype(F32), c, sa, sb, 16).astype(BF16)
    zeros = jnp.zeros((WINDOW, 128), BF16)
    kpad[0:WINDOW, :] = zeros
    kpad[WINDOW:WINDOW + N_LAT, :] = _rope(pa[:, 256:384].astype(F32), c, sa, sb, 16).astype(BF16)
    kpad[WINDOW + N_LAT:, :] = zeros
    vpad[0:WINDOW, :] = zeros
    vpad[WINDOW:WINDOW + N_LAT, :] = pa[:, 384:512]
    vpad[WINDOW + N_LAT:, :] = zeros
    _load_cache_pair(ck_ref, cv_ref, 0, kcs, vcs)
    qi = lax.broadcasted_iota(jnp.int32, (2 * WINDOW, 3 * WINDOW), 0) % WINDOW
    kj = lax.broadcasted_iota(jnp.int32, (2 * WINDOW, 3 * WINDOW), 1)
    band = jnp.abs(kj - WINDOW - qi) <= WINDOW
    first_head = lax.broadcasted_iota(jnp.int32, (2 * WINDOW, 1), 0) < WINDOW
    low = _low_half(WINDOW)

    def body(nb, carry):
        r0 = pl.multiple_of(nb * WINDOW, WINDOW)
        kpos = r0 - WINDOW + kj
        valid = band & (kpos >= 0) & (kpos < N_LAT)
        kw = kpad[pl.ds(r0, 3 * WINDOW), :]
        vexts = [_with_ones(vpad[pl.ds(r0, 3 * WINDOW), :]), _T(vcs[...])]
        res = []
        for g in range(2):
            q = jnp.concatenate([_keep_lanes(qs[pl.ds(r0, WINDOW), 0:128], (64 * g, 64)),
                                 _keep_lanes(qs[pl.ds(r0, WINDOW), 128:256], (64 * g, 64))], axis=0)
            s_loc = jnp.where(valid, _dot_t(q, kw), NEG_INF)
            sink = jnp.where(first_head, sink_ref[2 * g], sink_ref[2 * g + 1])
            res.append(_attend_scores([s_loc, _dot(q, kcs[...])], vexts, sink))
        o_ref[pl.ds(r0, WINDOW), 0:128] = jnp.where(low, res[0][0:WINDOW], res[1][0:WINDOW]).astype(BF16)
        o_ref[pl.ds(r0, WINDOW), 128:256] = jnp.where(low, res[0][WINDOW:], res[1][WINDOW:]).astype(BF16)
        return carry

    lax.fori_loop(0, N_LAT // WINDOW, body, 0)


def _lat_win(l, sink, pa, ck, cv, tabs):
    return pl.pallas_call(
        _lat_win_kernel,
        grid=(N_LAT_B,),
        in_specs=[pl.BlockSpec(memory_space=pltpu.SMEM),
                  pl.BlockSpec((N_LAT, 512), _LAT_ROW),
                  _cache_spec(l, 2), _cache_spec(l, 2),
                  _tab_spec(), _tab_spec(), _tab_spec()],
        out_specs=pl.BlockSpec((N_LAT, 256), lambda b: (b, 0)),
        out_shape=jax.ShapeDtypeStruct((T_LAT, 256), BF16),
        scratch_shapes=[pltpu.VMEM((N_LAT, 256), BF16),
                        pltpu.VMEM((N_LAT + 2 * WINDOW, 128), BF16),
                        pltpu.VMEM((N_LAT + 2 * WINDOW, 128), BF16),
                        pltpu.VMEM((128, L_CTX), BF16),
                        pltpu.VMEM((256, L_CTX), BF16)],
        compiler_params=_cparams(("arbitrary",)),
        name="lat_win",
    )(sink, pa, ck, cv, *tabs)


QT = 256
DIFF_QT = 256
N_KEYS = N_LAT + L_CTX


def _lat_diff_kernel(lam_ref, pb, ck_ref, cv_ref, subln_ref, c_ref, sa_ref, sb_ref, o_ref, qs, klat, kcs, vcs):
    c, sa, sb = c_ref[...], sa_ref[...], sb_ref[...]
    for j in range(2):
        qs[:, 128 * j:128 * j + 128] = _rope(pb[:, 128 * j:128 * j + 128].astype(F32), c, sa, sb, 8).astype(BF16)
        klat[j] = _rope(pb[:, 256 + 128 * j:384 + 128 * j].astype(F32), c, sa, sb, 8).astype(BF16)
        _load_cache_pair(ck_ref, cv_ref, j, kcs.at[j], vcs.at[j])
    lam = lam_ref[0]
    out_scale = lam_ref[1]
    low = _low_half(DIFF_QT)
    n = DIFF_QT

    def body(t, carry):
        r0 = pl.multiple_of(t * n, n)
        for j in range(2):
            o = _attend_diff(qs[pl.ds(r0, n), 128 * j:128 * j + 128], [klat[j], _T(kcs[j])],
                             [_with_ones(pb[:, 512 + 128 * j:640 + 128 * j]), _T(vcs[j])], lam, low)
            o_ref[pl.ds(r0, n), 128 * j:128 * j + 128] = (_rms_halves(o, low, subln_ref[...]) * out_scale).astype(BF16)
        return carry

    lax.fori_loop(0, N_LAT // n, body, 0)


def _lat_diff(l, lam, pb, ck, cv, subln, tabs):
    return pl.pallas_call(
        _lat_diff_kernel,
        grid=(N_LAT_B,),
        in_specs=[pl.BlockSpec(memory_space=pltpu.SMEM),
                  pl.BlockSpec((N_LAT, 768), _LAT_ROW),
                  _cache_spec(l, 4), _cache_spec(l, 4),
                  pl.BlockSpec((1, 128), lambda b: (0, 0)),
                  _tab_spec(), _tab_spec(), _tab_spec()],
        out_specs=pl.BlockSpec((N_LAT, 256), lambda b: (b, 0)),
        out_shape=jax.ShapeDtypeStruct((T_LAT, 256), BF16),
        scratch_shapes=[pltpu.VMEM((N_LAT, 256), BF16),
                        pltpu.VMEM((2, N_LAT, 128), BF16),
                        pltpu.VMEM((2, 128, L_CTX), BF16),
                        pltpu.VMEM((2, 256, L_CTX), BF16)],
        compiler_params=_cparams(("arbitrary",)),
        name="lat_diff",
    )(lam, pb, ck, cv, subln, *tabs)


def _lat_mla_kernel(pc, ckv_ref, kpe_ref, wukv_ref, c_ref, sa_ref, sb_ref, o_ref, qcat, kcat, kvc, kpe4t):
    c, sa, sb = c_ref[...], sa_ref[...], sb_ref[...]
    qr = _rope(pc[:, 256:384].astype(F32), c, sa, sb, 8).astype(BF16)
    kpe_l = _rope(pc[:, 384:512].astype(F32), c, sa, sb, 8).astype(BF16)
    kvc[...] = _dot(ckv_ref[...].astype(BF16), wukv_ref[...]).astype(BF16)
    for i in range(4):
        kpe4t[32 * i:32 * i + 32, :] = kpe_ref[...].astype(BF16)
    for j in range(2):
        qcat[j, :, 0:128] = pc[:, 128 * j:128 * j + 128]
        qcat[j, :, 128:256] = qr
        kcat[j, :, 0:128] = pc[:, 512 + 128 * j:640 + 128 * j]
        kcat[j, :, 128:256] = kpe_l
    low = _low_half(QT)

    def body(t, carry):
        r0 = pl.multiple_of(t * QT, QT)
        for j in range(2):
            q2 = qcat[j, pl.ds(r0, QT), :]
            q = jnp.concatenate([_keep_lanes(q2, (64 * hh, 64), (128 + 32 * (2 * j + hh), 32)) for hh in range(2)],
                                axis=0)
            s_ctx = _dot_t(q[:, 0:128], kvc[:, 128 * j:128 * j + 128]) + _dot(q[:, 128:256], kpe4t[...])
            r = _attend_scores([_dot_t(q, kcat[j]), s_ctx],
                               [_with_ones(pc[:, 768 + 128 * j:896 + 128 * j]),
                                _with_ones(kvc[:, 256 + 128 * j:384 + 128 * j])])
            o_ref[pl.ds(r0, QT), 128 * j:128 * j + 128] = jnp.where(low, r[0:QT], r[QT:]).astype(BF16)
        return carry

    lax.fori_loop(0, N_LAT // QT, body, 0)


def _lat_mla(l, pc, ckv, kpe, wukv_p, tabs):
    return pl.pallas_call(
        _lat_mla_kernel,
        grid=(N_LAT_B,),
        in_specs=[pl.BlockSpec((N_LAT, PC_W), _LAT_ROW),
                  pl.BlockSpec((None, None, L_CTX, 128), lambda b: (b, l, 0, 0)),
                  pl.BlockSpec((None, None, 32, L_CTX), lambda b: (b, l, 0, 0)),
                  pl.BlockSpec((None, 128, 512), lambda b: (l, 0, 0)),
                  _tab_spec(), _tab_spec(), _tab_spec()],
        out_specs=pl.BlockSpec((N_LAT, 256), lambda b: (b, 0)),
        out_shape=jax.ShapeDtypeStruct((T_LAT, 256), BF16),
        scratch_shapes=[pltpu.VMEM((2, N_LAT, 256), BF16),
                        pltpu.VMEM((2, N_LAT, 256), BF16),
                        pltpu.VMEM((L_CTX, 512), BF16),
                        pltpu.VMEM((128, L_CTX), BF16)],
        compiler_params=_cparams(("arbitrary",)),
        name="lat_mla",
    )(pc, ckv, kpe, wukv_p, *tabs)


NA_ROWS = 8
NA_KEYS = NA_ROWS * GRID_W


def _lat_na_kernel(pd, ck_ref, cv_ref, bias_ref, o_ref, kcs, vcs):
    n_rows = N_LAT // GRID_W
    for j in range(2):
        _load_cache_pair(ck_ref, cv_ref, j, kcs.at[j], vcs.at[j])
    low = _low_half(GRID_W)

    def body(r, carry):
        rs = jnp.clip(r - NA_ROWS // 2, 0, n_rows - NA_ROWS)
        q0 = pl.multiple_of(r * GRID_W, GRID_W)
        k0 = pl.multiple_of(rs * GRID_W, GRID_W)
        for j in range(2):
            qp = pd[pl.ds(q0, GRID_W), 128 * j:128 * j + 128]
            q = jnp.concatenate([_keep_lanes(qp, (0, 64)), _keep_lanes(qp, (64, 64))], axis=0)
            s_loc = _dot_t(q, pd[pl.ds(k0, NA_KEYS), 256 + 128 * j:384 + 128 * j]) + bias_ref[j, r - rs]
            s_ctx = _dot(q, kcs[j])
            res = _attend_scores([s_loc, s_ctx],
                                 [_with_ones(pd[pl.ds(k0, NA_KEYS), 512 + 128 * j:640 + 128 * j]), _T(vcs[j])])
            o_ref[pl.ds(q0, GRID_W), 128 * j:128 * j + 128] = jnp.where(low, res[0:GRID_W], res[GRID_W:]).astype(BF16)
        return carry

    lax.fori_loop(0, n_rows, body, 0)


def _lat_na(l, pd, ck, cv, bias):
    return pl.pallas_call(
        _lat_na_kernel,
        grid=(N_LAT_B,),
        in_specs=[pl.BlockSpec((N_LAT, 768), _LAT_ROW),
                  _cache_spec(l, 4), _cache_spec(l, 4),
                  pl.BlockSpec((None, 2, NA_ROWS, 2 * GRID_W, NA_KEYS), lambda b: (l, 0, 0, 0, 0))],
        out_specs=pl.BlockSpec((N_LAT, 256), lambda b: (b, 0)),
        out_shape=jax.ShapeDtypeStruct((T_LAT, 256), BF16),
        scratch_shapes=[pltpu.VMEM((2, 128, L_CTX), BF16), pltpu.VMEM((2, 256, L_CTX), BF16)],
        compiler_params=_cparams(("arbitrary",)),
        name="lat_na",
    )(pd, ck, cv, bias)


def _out_proj_kernel(with_router, n_x, *refs):
    x_refs = refs[:n_x]
    refs = refs[n_x:]
    if with_router:
        (oc_ref, oa_ref, ob_ref, om_ref, od_ref, mod_ref, wo_ref, gpost_ref, gffn_ref, r_ref,
         x1_ref, h2_ref, route_ref, cnt_ref) = refs
    else:
        (oc_ref, oa_ref, ob_ref, om_ref, od_ref, mod_ref, wo_ref, gpost_ref, gffn_ref,
         x1_ref, h2_ref) = refs
    i = pl.program_id(0)
    nct = T_CTX // TM

    if with_router:
        @pl.when(i == 0)
        def _():
            cnt_ref[...] = jnp.zeros(cnt_ref.shape, F32)

    def finish(y):
        m = mod_ref[...]
        x1 = _x_tile(x_refs) + m[2:3] * _rms(y, gpost_ref[...])
        x1_ref[...] = x1
        h2 = _rms(x1, gffn_ref[...]) * (1.0 + m[4:5]) + m[3:4]
        h2_ref[...] = h2.astype(h2_ref.dtype)
        if with_router:
            lg = _dot(h2.astype(BF16), r_ref[...])
            lane = lax.broadcasted_iota(jnp.int32, lg.shape, 1)
            lane_f = lane.astype(F32)
            lg = jnp.where(lane < N_EXPERTS, lg, -jnp.inf)
            m1 = jnp.max(lg, axis=-1, keepdims=True)
            i1 = jnp.min(jnp.where(lg == m1, lane_f, 128.0), axis=-1, keepdims=True)
            lg2 = jnp.where(lane_f == i1, -jnp.inf, lg)
            m2 = jnp.max(lg2, axis=-1, keepdims=True)
            i2 = jnp.min(jnp.where(lg2 == m2, lane_f, 128.0), axis=-1, keepdims=True)
            e = jnp.exp(m2 - m1)
            g1 = 1.0 / (1.0 + e)
            g2 = e / (1.0 + e)
            sel1 = lane_f == i1
            sel2 = lane_f == i2
            picked = jnp.where(sel1 | sel2, 1.0, 0.0)
            ri = lax.broadcasted_iota(jnp.int32, (TM, TM), 0)
            ci = lax.broadcasted_iota(jnp.int32, (TM, TM), 1)
            before = _dot(jnp.where(ci < ri, 1.0, 0.0).astype(BF16), picked.astype(BF16)) + cnt_ref[0:1, :]
            r1 = jnp.sum(jnp.where(sel1, before, 0.0), axis=-1, keepdims=True)
            r2 = jnp.sum(jnp.where(sel2, before, 0.0), axis=-1, keepdims=True)
            cnt_ref[...] = cnt_ref[...] + jnp.sum(picked, axis=0, keepdims=True)
            route = jnp.where(lane == 0, i1, jnp.where(lane == 1, i2, jnp.where(lane == 2, g1, 0.0)))
            route_ref[...] = jnp.where(lane == 3, g2, jnp.where(lane == 4, r1, jnp.where(lane == 5, r2, route)))

    @pl.when(i < nct)
    def _():
        finish(_dot(oc_ref[...], wo_ref[...]))

    @pl.when(i >= nct)
    def _():
        y = _dot(oa_ref[...], wo_ref[0:256, :])
        y = y + _dot(ob_ref[...], wo_ref[256:512, :])
        y = y + _dot(om_ref[...], wo_ref[512:768, :])
        y = y + _dot(od_ref[...], wo_ref[768:1024, :])
        finish(y)


def _out_proj(l, o_ctx, o_lat, xs, mods, w_o, g_post, g_ffn, router_p):
    nct = T_CTX // TM
    row = lambda i: (i, 0)
    lyr2 = lambda i: (l, 0, 0)
    lat = lambda i: (jnp.maximum(i - nct, 0), 0)
    with_router = router_p is not None
    in_specs = _x_specs(xs) + [
                pl.BlockSpec((TM, D), lambda i: (jnp.minimum(i, nct - 1), 0)),
                pl.BlockSpec((TM, 256), lat), pl.BlockSpec((TM, 256), lat),
                pl.BlockSpec((TM, 256), lat), pl.BlockSpec((TM, 256), lat),
                pl.BlockSpec((None, None, 6, D), lambda i: (l, _mod_row(i), 0, 0)),
                pl.BlockSpec((None, D, D), lyr2),
                pl.BlockSpec((None, 1, D), lyr2),
                pl.BlockSpec((None, 1, D), lyr2)]
    out_specs = [pl.BlockSpec((TM, D), row), pl.BlockSpec((TM, D), row)]
    out_shape = [jax.ShapeDtypeStruct((T, D), F32), jax.ShapeDtypeStruct((T, D), F32 if with_router else BF16)]
    args = [*xs, o_ctx, *o_lat, mods, w_o, g_post, g_ffn]
    if with_router:
        in_specs.append(pl.BlockSpec((D, 128), lambda i: (0, 0)))
        out_specs += [pl.BlockSpec((TM, 128), row), pl.BlockSpec((8, 128), lambda i: (0, 0))]
        out_shape += [jax.ShapeDtypeStruct((T, 128), F32), jax.ShapeDtypeStruct((8, 128), F32)]
        args.append(router_p)
    return pl.pallas_call(
        functools.partial(_out_proj_kernel, with_router, len(xs)),
        grid=(T // TM,),
        in_specs=in_specs, out_specs=out_specs, out_shape=out_shape,
        compiler_params=_cparams(("arbitrary",)),
        name="out_proj_router" if with_router else "out_proj",
    )(*args)


FFN_TM = 512


def _ffn_kernel(h_ref, x_ref, mod_ref, g_ref, wg_ref, wu_ref, wd_ref, o_ref):
    h = h_ref[...]
    acc = None
    lo = 0
    for width in FFN_SLICES:
        t = (_silu(_dot(h, wg_ref[:, lo:lo + width])) * _dot(h, wu_ref[:, lo:lo + width])).astype(BF16)
        part = _dot(t, wd_ref[lo:lo + width, :])
        acc = part if acc is None else acc + part
        lo += width
    m = mod_ref[...]
    o_ref[...] = x_ref[...] + m[5:6] * _rms(acc, g_ref[...])


def _ffn_mod_row(i):
    nct = T_CTX // FFN_TM
    return jnp.where(i >= nct, 1 + (i - nct) // (N_LAT // FFN_TM), 0)


def _post_mod_row(i):
    nct = T_CTX // POST_TM
    return jnp.where(i >= nct, 1 + (i - nct) // (N_LAT // POST_TM), 0)


def _ffn(l, i_dense, h2, x1, mods, g_post, wg, wu, wd):
    row = lambda i: (i, 0)
    resident = lambda shape: pl.BlockSpec((None,) + shape, lambda i: (i_dense, 0, 0), pipeline_mode=pl.Buffered(1))
    return pl.pallas_call(
        _ffn_kernel,
        grid=(T // FFN_TM,),
        in_specs=[pl.BlockSpec((FFN_TM, D), row),
                  pl.BlockSpec((FFN_TM, D), row),
                  pl.BlockSpec((None, None, 6, D), lambda i: (l, _ffn_mod_row(i), 0, 0)),
                  pl.BlockSpec((None, 1, D), lambda i: (l, 0, 0)),
                  resident((D, D_FF)), resident((D, D_FF)), resident((D_FF, D))],
        out_specs=pl.BlockSpec((FFN_TM, D), row),
        out_shape=jax.ShapeDtypeStruct((T, D), F32),
        compiler_params=_cparams(("arbitrary",)),
        name="ffn",
    )(h2, x1, mods, g_post, wg, wu, wd)


def _dispatch_kernel(slot_ref, h_ref, xs_hbm, sem):
    def row_copy(g, r, k):
        return pltpu.make_async_copy(h_ref.at[g, pl.ds(r, 1), :],
                                     xs_hbm.at[pl.ds(slot_ref[0, 16 * g + 2 * r + k], 1)], sem)

    def issue(g, carry):
        for r in range(8):
            row_copy(g, r, 0).start()
            row_copy(g, r, 1).start()
        return carry

    def drain(g, carry):
        for r in range(8):
            row_copy(g, r, 0).wait()
            row_copy(g, r, 1).wait()
        return carry

    lax.fori_loop(0, DISPATCH_TM // 8, issue, 0)
    lax.fori_loop(0, DISPATCH_TM // 8, drain, 0)


def _dispatch(slots, h2):
    return pl.pallas_call(
        _dispatch_kernel,
        grid=(T // DISPATCH_TM,),
        in_specs=[pl.BlockSpec((None, 1, 2 * DISPATCH_TM), lambda i: (i, 0, 0), memory_space=pltpu.SMEM),
                  pl.BlockSpec((DISPATCH_TM // 8, 8, D), lambda i: (i, 0, 0))],
        out_specs=pl.BlockSpec(memory_space=pl.ANY),
        out_shape=jax.ShapeDtypeStruct((N_EXPERTS * EXPERT_CAP, D), F32),
        scratch_shapes=[pltpu.SemaphoreType.DMA(())],
        compiler_params=_cparams(("arbitrary",)),
        name="moe_dispatch",
    )(slots.reshape(T // DISPATCH_TM, 1, 2 * DISPATCH_TM), h2.reshape(T // 8, 8, D))


def _moe_kernel(ie_ref, ib_ref, nr_ref, x_ref, wg_ref, wu_ref, wd_ref, o_ref, wgb, wub, wdb):
    i = pl.program_id(0)
    c = pl.program_id(1)
    n_rows = nr_ref[i]

    @pl.when(n_rows > 0)
    def _():
        @pl.when(c == 0)
        def _():
            o_ref[...] = jnp.zeros(o_ref.shape, F32)

        wgb[...] = wg_ref[...].astype(BF16)
        wub[...] = wu_ref[...].astype(BF16)
        wdb[...] = wd_ref[...].astype(BF16)

        def step(r0, rows):
            live = r0 + lax.broadcasted_iota(jnp.int32, (rows, 1), 0) < n_rows
            xs = jnp.where(live, x_ref[pl.ds(r0, rows), :], 0.0).astype(BF16)
            t = (_silu(_dot(xs, wgb[...])) * _dot(xs, wub[...])).astype(BF16)
            o_ref[pl.ds(r0, rows), :] += _dot(t, wdb[...])

        whole = lax.shift_right_logical(n_rows, SUB_ROWS.bit_length() - 1)
        left = jnp.bitwise_and(n_rows, SUB_ROWS - 1)
        n_full = whole + (left > TAIL_ROWS).astype(jnp.int32)

        def body(s, carry):
            step(pl.multiple_of(s * SUB_ROWS, SUB_ROWS), SUB_ROWS)
            return carry

        lax.fori_loop(0, n_full, body, 0)

        @pl.when((left > 0) & (left <= TAIL_ROWS))
        def _():
            step(pl.multiple_of(whole * SUB_ROWS, SUB_ROWS), TAIL_ROWS)


def _moe(i_moe, item_e, item_blk, item_rows, x_sorted, wg, wu, wd):
    nc = D_FF_EXPERT // FF_CHUNK
    chunk = lambda i, c, nr: jnp.where(nr[i] > 0, c, nc - 1)
    grid_spec = pltpu.PrefetchScalarGridSpec(
        num_scalar_prefetch=3,
        grid=(N_ITEMS, nc),
        in_specs=[pl.BlockSpec((ITEM_ROWS, D), lambda i, c, ie, ib, nr: (ib[i], 0)),
                  pl.BlockSpec((None, None, D, FF_CHUNK),
                               lambda i, c, ie, ib, nr: (i_moe, ie[i], 0, chunk(i, c, nr))),
                  pl.BlockSpec((None, None, D, FF_CHUNK),
                               lambda i, c, ie, ib, nr: (i_moe, ie[i], 0, chunk(i, c, nr))),
                  pl.BlockSpec((None, None, FF_CHUNK, D),
                               lambda i, c, ie, ib, nr: (i_moe, ie[i], chunk(i, c, nr), 0))],
        out_specs=pl.BlockSpec((ITEM_ROWS, D), lambda i, c, ie, ib, nr: (ib[i], 0)),
        scratch_shapes=[pltpu.VMEM((D, FF_CHUNK), BF16), pltpu.VMEM((D, FF_CHUNK), BF16),
                        pltpu.VMEM((FF_CHUNK, D), BF16)],
    )
    return pl.pallas_call(
        _moe_kernel,
        grid_spec=grid_spec,
        out_shape=jax.ShapeDtypeStruct((N_EXPERTS * EXPERT_CAP, D), F32),
        compiler_params=_cparams(("arbitrary", "arbitrary")),
        name="moe_experts",
    )(item_e, item_blk, item_rows, x_sorted, wg, wu, wd)


def _post_kernel(final, slot_ref, ys_hbm, route_ref, x_ref, mod_ref, g_ref, *refs):
    buf, sem = refs[-2:]

    def row_copy(g, r, k):
        return pltpu.make_async_copy(ys_hbm.at[pl.ds(slot_ref[0, 16 * g + 2 * r + k], 1)],
                                     buf.at[k, g, pl.ds(r, 1), :], sem)

    def issue(g, carry):
        for r in range(8):
            row_copy(g, r, 0).start()
            row_copy(g, r, 1).start()
        return carry

    def drain(g, carry):
        for r in range(8):
            row_copy(g, r, 0).wait()
            row_copy(g, r, 1).wait()
        return carry

    lax.fori_loop(0, POST_TM // 8, issue, 0)
    lax.fori_loop(0, POST_TM // 8, drain, 0)
    r = route_ref[...]
    y = r[:, 2:3] * buf[0].reshape(POST_TM, D) + r[:, 3:4] * buf[1].reshape(POST_TM, D)
    m = mod_ref[...]
    out = x_ref[...] + m[5:6] * _rms(y, g_ref[...])
    if not final:
        refs[0][...] = out
    else:
        i = pl.program_id(0)

        @pl.when(i < T_CTX // POST_TM)
        def _():
            refs[0][...] = out

        @pl.when(i >= T_CTX // POST_TM)
        def _():
            refs[1][...] = out


def _post(l, slots, y_sorted, route, x1, mods, g_post, final):
    row = lambda i: (i, 0)
    nct = T_CTX // POST_TM
    if final:
        out_specs = [pl.BlockSpec((POST_TM, D), lambda i: (jnp.minimum(i, nct - 1), 0)),
                     pl.BlockSpec((POST_TM, D), lambda i: (jnp.maximum(i - nct, 0), 0))]
        out_shape = [jax.ShapeDtypeStruct((T_CTX, D), F32), jax.ShapeDtypeStruct((T_LAT, D), F32)]
    else:
        out_specs = pl.BlockSpec((POST_TM, D), row)
        out_shape = jax.ShapeDtypeStruct((T, D), F32)
    return pl.pallas_call(
        functools.partial(_post_kernel, final),
        grid=(T // POST_TM,),
        in_specs=[pl.BlockSpec((None, 1, 2 * POST_TM), lambda i: (i, 0, 0), memory_space=pltpu.SMEM),
                  pl.BlockSpec(memory_space=pl.ANY),
                  pl.BlockSpec((POST_TM, 128), row), pl.BlockSpec((POST_TM, D), row),
                  pl.BlockSpec((None, None, 6, D), lambda i: (l, _post_mod_row(i), 0, 0)),
                  pl.BlockSpec((None, 1, D), lambda i: (l, 0, 0))],
        out_specs=out_specs,
        out_shape=out_shape,
        scratch_shapes=[pltpu.VMEM((2, POST_TM // 8, 8, D), F32), pltpu.SemaphoreType.DMA(())],
        compiler_params=_cparams(("arbitrary",)),
        name="moe_post_final" if final else "moe_post",
    )(slots.reshape(T // POST_TM, 1, 2 * POST_TM), y_sorted, route, x1, mods, g_post)


def _route_plan(route, counts):
    i32 = jnp.int32
    slots = (route[:, 0:2].astype(i32) * EXPERT_CAP + route[:, 4:6].astype(i32)).reshape(-1)
    n = counts[0, :N_EXPERTS].astype(i32)
    n_items = (n + ITEM_ROWS - 1) // ITEM_ROWS
    ends = jnp.cumsum(n_items)
    base = ends - n_items
    total = ends[-1]
    ii = jnp.arange(N_ITEMS, dtype=i32)
    ii_c = jnp.minimum(ii, total - 1)
    item_e = jnp.sum((ends[None, :] <= ii_c[:, None]).astype(i32), axis=1)
    j = ii_c - base[item_e]
    item_blk = item_e * (EXPERT_CAP // ITEM_ROWS) + j
    rows = jnp.where(ii < total, jnp.clip(n[item_e] - j * ITEM_ROWS, 0, ITEM_ROWS), 0)
    return slots, item_e.astype(i32), item_blk.astype(i32), rows.astype(i32)


def _rope_tables(d):
    half, quarter = d // 2, d // 4
    t = jnp.arange(N_LAT)
    pr, pc = t // GRID_W, t % GRID_W
    i = jnp.arange(128) % d
    j = (i % half) % quarter
    inv = ROPE_THETA ** (-(2 * j).astype(F32) / half)
    pos = jnp.where((i < half)[None, :], pr[:, None], pc[:, None]).astype(F32)
    ang = pos * inv[None, :]
    low = ((i % half) < quarter)[None, :]
    sin = jnp.sin(ang)
    return jnp.cos(ang), jnp.where(low, -sin, 0.0), jnp.where(low, 0.0, sin)


def _na_bias_table(rpb):
    q = jnp.arange(GRID_W)
    k = jnp.arange(GRID_W)
    rel_c = jnp.clip(k[None, :] - q[:, None], -15, 15) + 15
    cs = jnp.clip(q - 8, 0, GRID_W - 16)
    valid = (k[None, :] >= cs[:, None]) & (k[None, :] < cs[:, None] + 16)
    pick = rel_c[:, :, None] == jnp.arange(31)[None, None, :]
    band = jnp.sum(jnp.where(pick[None, None, None], rpb[:, :, :, None, None, :], 0.0), axis=-1)
    band = jnp.where(valid[None, None, None], band, NEG_INF)
    tab = jnp.stack([band[:, :, NA_ROWS - 1 - dd:2 * NA_ROWS - 1 - dd] for dd in range(NA_ROWS)], axis=2)
    tab = tab.transpose(0, 1, 2, 4, 3, 5)
    tab = tab.reshape(DEPTH, 2, 2, NA_ROWS, GRID_W, NA_KEYS).transpose(0, 1, 3, 2, 4, 5)
    return tab.reshape(DEPTH, 2, NA_ROWS, 2 * GRID_W, NA_KEYS)


def _pad_in_proj(w_in):
    wt = jnp.swapaxes(w_in, 1, 2)
    z = lambda n: jnp.zeros((DEPTH, n, D), w_in.dtype)
    s64, s32 = HD ** -0.5, (HD // 2) ** -0.5
    qa = wt[:, 0:256] * s64
    kpe = wt[:, 1600:1632]
    return jnp.concatenate([qa[:, 0:64], qa[:, 128:192], qa[:, 64:128], qa[:, 192:256],
                            wt[:, 256:512],
                            wt[:, 512:768] * s32, wt[:, 768:1280],
                            wt[:, 1472:1600], kpe, kpe, kpe, kpe,
                            wt[:, 1280:1472], z(64),
                            wt[:, 1632:1888] * s64, wt[:, 1888:2400]], axis=1).astype(BF16)


def _split_heads_cols(w, first, second):
    lead = w.shape[:-1]
    w4 = w.reshape(*lead, 4, first + second)
    return jnp.concatenate([w4[..., :first].reshape(*lead, 4 * first),
                            w4[..., first:].reshape(*lead, 4 * second)], axis=-1)


def kernel(x_prompt, x_sample, cache_win_k, cache_win_v, cache_diff_k, cache_diff_v, cache_mla_ckv, cache_mla_kpe, cache_na_k, cache_na_v, c, c_ctx, w_ada, b_ada, norm_mix_pre, norm_mix_post, norm_ffn_pre, norm_ffn_post, w_in, w_o, win_sink, diff_lq1, diff_lk1, diff_lq2, diff_lk2, diff_subln, mla_q_norm, mla_kv_norm, mla_wuq, mla_wukv, na_rpb, ffn_w_gate, ffn_w_up, ffn_w_down, moe_router, moe_w_gate, moe_w_up, moe_w_down):
    w_in_p = _pad_in_proj(w_in)
    wuq_p = jnp.pad(_split_heads_cols(mla_wuq, 64, 32) * (HD + HD // 2) ** -0.5,
                    ((0, 0), (0, 256 - MLA_Q_RANK), (0, 0))).astype(BF16)
    wukv_p = _split_heads_cols(mla_wukv, 64, 64).astype(BF16)
    qn_p = jnp.pad(mla_q_norm, ((0, 0), (0, 256 - MLA_Q_RANK))).reshape(DEPTH, 1, 256)
    kvn = mla_kv_norm.reshape(DEPTH, 1, 128)
    w_o_b = jnp.concatenate([w_o[:, 0:64], w_o[:, 128:192], w_o[:, 64:128], w_o[:, 192:]], axis=1).astype(BF16)
    ffn_wg, ffn_wu, ffn_wd = ffn_w_gate.astype(BF16), ffn_w_up.astype(BF16), ffn_w_down.astype(BF16)
    router_p = jnp.pad(moe_router, ((0, 0), (0, 0), (0, 128 - N_EXPERTS))).astype(BF16)
    g_mix_pre = norm_mix_pre.reshape(DEPTH, 1, D)
    g_mix_post = norm_mix_post.reshape(DEPTH, 1, D)
    g_ffn_pre = norm_ffn_pre.reshape(DEPTH, 1, D)
    g_ffn_post = norm_ffn_post.reshape(DEPTH, 1, D)
    tabs64 = _rope_tables(64)
    tabs32 = _rope_tables(32)
    na_bias = _na_bias_table(na_rpb)
    lam_init = jnp.asarray([0.8 - 0.6 * math.exp(-0.3 * l) for l in range(DEPTH)], F32)
    lam = (jnp.exp(jnp.sum(diff_lq1 * diff_lk1, axis=-1)) - jnp.exp(jnp.sum(diff_lq2 * diff_lk2, axis=-1))
           + lam_init)
    lam_tab = jnp.stack([lam, 1.0 - lam_init], axis=-1)
    subln = jnp.tile(diff_subln, (1, 2)).reshape(DEPTH, 1, 2 * HD)

    cond = jnp.concatenate([c_ctx[None], c, jnp.zeros((3, D), F32)], axis=0)
    mods = _mods(cond, w_ada, b_ada).reshape(DEPTH, 8, 6, D)

    tr = lambda a: jnp.swapaxes(a, -1, -2)
    cache_win_k, cache_win_v, cache_diff_k, cache_diff_v = map(tr, (cache_win_k, cache_win_v, cache_diff_k, cache_diff_v))
    cache_na_k, cache_na_v, cache_mla_kpe = map(tr, (cache_na_k, cache_na_v, cache_mla_kpe))

    xs = (x_prompt.reshape(T_CTX, D), x_sample.reshape(T_LAT, D))
    caches = ()
    for l in range(DEPTH):
        pa, pb, pc, pd = _in_proj(l, xs, mods, g_mix_pre, w_in_p, qn_p, kvn, wuq_p, wukv_p)
        o_ctx, *caches = _attn_ctx(l, win_sink[l], lam_tab[l], pa, pb, pc, pd, subln[l], caches)
        o_lat = [_lat_win(l, win_sink[l], pa, cache_win_k, cache_win_v, tabs64),
                 _lat_diff(l, lam_tab[l], pb, cache_diff_k, cache_diff_v, subln[l], tabs32),
                 _lat_mla(l, pc, cache_mla_ckv, cache_mla_kpe, wukv_p, tabs32),
                 _lat_na(l, pd, cache_na_k, cache_na_v, na_bias)]
        if l % 2 == 0:
            x1, h2 = _out_proj(l, o_ctx, o_lat, xs, mods, w_o_b, g_mix_post, g_ffn_pre, None)
            xs = (_ffn(l, l // 2, h2, x1, mods, g_ffn_post, ffn_wg, ffn_wu, ffn_wd),)
        else:
            x1, h2, route, counts = _out_proj(l, o_ctx, o_lat, xs, mods, w_o_b, g_mix_post, g_ffn_pre,
                                              router_p[l // 2])
            slots, item_e, item_blk, item_rows = _route_plan(route, counts)
            x_sorted = _dispatch(slots, h2)
            y_sorted = _moe(l // 2, item_e, item_blk, item_rows, x_sorted, moe_w_gate, moe_w_up, moe_w_down)
            xs = _post(l, slots, y_sorted, route, x1, mods, g_ffn_post, final=l == DEPTH - 1)
            xs = tuple(xs) if l == DEPTH - 1 else (xs,)

    y_prompt, y_sample = xs
    return (y_prompt.reshape(N_CTX_B, L_CTX, D), y_sample.reshape(N_LAT_B, N_LAT, D)) + tuple(caches)
```

```python
import functools
import math

import jax
import jax.numpy as jnp
from jax import lax
from jax.experimental import pallas as pl
from jax.experimental.pallas import tpu as pltpu

F32 = jnp.float32
BF16 = jnp.bfloat16

D = 1024
DEPTH = 4
N_CTX_B, L_CTX = 16, 256
N_LAT_B, N_LAT = 4, 1024
T_CTX = N_CTX_B * L_CTX
T_LAT = N_LAT_B * N_LAT
T = T_CTX + T_LAT
HD = 64
GRID_W = 64
WINDOW = 128
EPS = 1e-6
NEG_INF = -1e30
ROPE_THETA = 10000.0
MLA_Q_RANK = 192
D_FF = 2816
N_EXPERTS = 8
D_FF_EXPERT = 3584

PW = 2560
C_CKV, C_KPE, C_CQ, C_D = 1280, 1408, 1536, 1792
PC_W = 1152

TM = 512
EXPERT_CAP = T
ITEM_ROWS = 1024
SUB_ROWS = 512
TAIL_ROWS = 256
N_ITEMS = N_EXPERTS + (2 * T) // ITEM_ROWS
FF_CHUNK = 512
DISPATCH_TM = 1024
POST_TM = 1024
FFN_SLICES = (1024, 1024, 768)
VMEM_LIMIT = 56 * 1024 * 1024


def _cparams(sem):
    return pltpu.CompilerParams(dimension_semantics=sem, vmem_limit_bytes=VMEM_LIMIT)


def _rms(x, g):
    return x * lax.rsqrt(jnp.mean(x * x, axis=-1, keepdims=True) + EPS) * g


def _silu(a):
    return a / (1.0 + jnp.exp(-a))


def _dot(a, b):
    return jnp.dot(a, b, preferred_element_type=F32)


def _dot_t(a, b):
    return lax.dot_general(a, b, (((1,), (1,)), ((), ())), preferred_element_type=F32)


def _with_ones(v):
    return jnp.concatenate([v, jnp.ones(v.shape, v.dtype)], axis=1)


def _with_ones_t(vt):
    return jnp.concatenate([vt, jnp.ones(vt.shape, vt.dtype)], axis=0)


class _T:
    def __init__(self, a):
        self.a = a


def _scores(q, k):
    return _dot(q, k.a) if isinstance(k, _T) else _dot_t(q, k)


def _attend_scores(scores, vexts, extra_logit=None):
    mx = jnp.max(scores[0], axis=-1, keepdims=True)
    for s in scores[1:]:
        mx = jnp.maximum(mx, jnp.max(s, axis=-1, keepdims=True))
    if extra_logit is not None:
        mx = jnp.maximum(mx, extra_logit)
    r = None
    for s, v in zip(scores, vexts):
        p = jnp.exp(s - mx).astype(BF16)
        part = _dot_t(p, v.a) if isinstance(v, _T) else _dot(p, v)
        r = part if r is None else r + part
    den = r[:, 128:256]
    if extra_logit is not None:
        den = den + jnp.exp(extra_logit - mx)
    return r[:, 0:128] / den


def _attend(q, keys, vexts, extra_logit=None):
    return _attend_scores([_scores(q, k) for k in keys], vexts, extra_logit)


def _attend_diff(qp, keys, vexts, lam, low):
    n = qp.shape[0]
    q = jnp.concatenate([_keep_lanes(qp, (32 * i, 32)) for i in range(4)], axis=0)
    r = _attend(q, keys, vexts)
    return jnp.where(low, r[0:n] - lam * r[n:2 * n], r[2 * n:3 * n] - lam * r[3 * n:4 * n])


def _keep_lanes(x, *ranges):
    lane = lax.broadcasted_iota(jnp.int32, x.shape, 1)
    keep = None
    for lo, width in ranges:
        k = (lane >= lo) & (lane < lo + width)
        keep = k if keep is None else keep | k
    return jnp.where(keep, x, jnp.zeros_like(x))


def _low_half(n):
    return lax.broadcasted_iota(jnp.int32, (n, 128), 1) < HD


def _rms_halves(o, low, g):
    sq = o * o
    ms = jnp.where(low, jnp.sum(jnp.where(low, sq, 0.0), axis=-1, keepdims=True),
                   jnp.sum(jnp.where(low, 0.0, sq), axis=-1, keepdims=True)) / HD
    return o * lax.rsqrt(ms + EPS) * g


def _rope(x, c, sa, sb, shift):
    return x * c + pltpu.roll(x, 128 - shift, 1) * sa + pltpu.roll(x, shift, 1) * sb


def _mod_row(i):
    nct = T_CTX // TM
    per_b = N_LAT // TM
    return jnp.where(i >= nct, 1 + (i - nct) // per_b, 0)


def _mods_kernel(cond_ref, w_ref, b_ref, o_ref):
    a = _silu(cond_ref[...]).astype(BF16)
    o_ref[...] = _dot(a, w_ref[...].astype(BF16)) + b_ref[...]


def _mods(cond, w_ada, b_ada):
    tn = 1536
    n = 6 * D
    return pl.pallas_call(
        _mods_kernel,
        grid=(DEPTH, n // tn),
        in_specs=[pl.BlockSpec((8, D), lambda l, j: (0, 0)),
                  pl.BlockSpec((None, D, tn), lambda l, j: (l, 0, j)),
                  pl.BlockSpec((None, 1, tn), lambda l, j: (l, 0, j))],
        out_specs=pl.BlockSpec((None, 8, tn), lambda l, j: (l, 0, j)),
        out_shape=jax.ShapeDtypeStruct((DEPTH, 8, n), F32),
        compiler_params=_cparams(("arbitrary", "arbitrary")),
        name="mods",
    )(cond, w_ada, b_ada.reshape(DEPTH, 1, n))


def _x_specs(xs):
    if len(xs) == 1:
        return [pl.BlockSpec((TM, D), lambda i: (i, 0))]
    nct = T_CTX // TM
    return [pl.BlockSpec((TM, D), lambda i: (jnp.minimum(i, nct - 1), 0)),
            pl.BlockSpec((TM, D), lambda i: (jnp.maximum(i - nct, 0), 0))]


def _x_tile(x_refs):
    if len(x_refs) == 1:
        return x_refs[0][...]
    return jnp.where(pl.program_id(0) < T_CTX // TM, x_refs[0][...], x_refs[1][...])


def _in_proj_kernel(n_x, *refs):
    x_refs = refs[:n_x]
    mod_ref, g_ref, w_ref, qn_ref, kvn_ref, wuq_ref, wukv_ref, pa_ref, pb_ref, pc_ref, pd_ref = refs[n_x:]
    m = mod_ref[...]
    h = _rms(_x_tile(x_refs), g_ref[...]) * (1.0 + m[1:2]) + m[0:1]
    p = _dot_t(h.astype(BF16), w_ref[...])
    pa_ref[...] = p[:, 0:512].astype(BF16)
    pb_ref[...] = p[:, 512:1280].astype(BF16)
    pd_ref[...] = p[:, C_D:PW].astype(BF16)
    ckv_n = _rms(p[:, C_CKV:C_CKV + 128], kvn_ref[...])
    z = p[:, C_CQ:C_CQ + 256]
    cqn = z * lax.rsqrt(jnp.sum(z * z, axis=-1, keepdims=True) / MLA_Q_RANK + EPS) * qn_ref[...]
    qc = _dot(cqn.astype(BF16), wuq_ref[...])
    kv = _dot(ckv_n.astype(BF16), wukv_ref[...])
    pc_ref[:, 0:384] = qc.astype(BF16)
    pc_ref[:, 384:512] = p[:, C_KPE:C_KPE + 128].astype(BF16)
    pc_ref[:, 512:1024] = kv.astype(BF16)
    pc_ref[:, 1024:1152] = ckv_n.astype(BF16)


def _in_proj(l, xs, mods, g_pre, w_in_p, qn_p, kvn, wuq_p, wukv_p):
    row = lambda i: (i, 0)
    lyr2 = lambda i: (l, 0, 0)
    return pl.pallas_call(
        functools.partial(_in_proj_kernel, len(xs)),
        grid=(T // TM,),
        in_specs=_x_specs(xs) + [
                  pl.BlockSpec((None, None, 6, D), lambda i: (l, _mod_row(i), 0, 0)),
                  pl.BlockSpec((None, 1, D), lyr2),
                  pl.BlockSpec((None, PW, D), lyr2),
                  pl.BlockSpec((None, 1, 256), lyr2),
                  pl.BlockSpec((None, 1, 128), lyr2),
                  pl.BlockSpec((None, 256, 384), lyr2),
                  pl.BlockSpec((None, 128, 512), lyr2)],
        out_specs=[pl.BlockSpec((TM, 512), row), pl.BlockSpec((TM, 768), row),
                   pl.BlockSpec((TM, PC_W), row), pl.BlockSpec((TM, 768), row)],
        out_shape=[jax.ShapeDtypeStruct((T, 512), BF16), jax.ShapeDtypeStruct((T, 768), BF16),
                   jax.ShapeDtypeStruct((T, PC_W), BF16), jax.ShapeDtypeStruct((T, 768), BF16)],
        compiler_params=_cparams(("arbitrary",)),
        name="in_proj",
    )(*xs, mods, g_pre, w_in_p, qn_p, kvn, wuq_p, wukv_p)


def _attn_ctx_kernel(n_carried, sink_ref, lam_ref, pa, pb, pc, pd, subln_ref, *refs):
    o_ref, ka_o, va_o, kb_o, vb_o, ckv_o, kpe_o, kd_o, vd_o = refs[n_carried:]
    n = L_CTX
    low = _low_half(n)
    first = lax.broadcasted_iota(jnp.int32, (2 * n, 1), 0) < n
    ka, va = pa[:, 256:384], _with_ones(pa[:, 384:512])
    res = []
    for g in range(2):
        q = jnp.concatenate([_keep_lanes(pa[:, 0:128], (64 * g, 64)), _keep_lanes(pa[:, 128:256], (64 * g, 64))], axis=0)
        sink = jnp.where(first, sink_ref[2 * g], sink_ref[2 * g + 1])
        res.append(_attend(q, [ka], [va], sink))
    o_ref[:, 0:128] = jnp.where(low, res[0][0:n], res[1][0:n]).astype(BF16)
    o_ref[:, 128:256] = jnp.where(low, res[0][n:], res[1][n:]).astype(BF16)
    for g in range(2):
        ka_o[g] = pa[:, 256 + 64 * g:320 + 64 * g].astype(F32)
        va_o[g] = pa[:, 384 + 64 * g:448 + 64 * g].astype(F32)
    lam = lam_ref[0]
    out_scale = lam_ref[1]
    for j in range(2):
        o = _attend_diff(pb[:, 128 * j:128 * j + 128], [pb[:, 256 + 128 * j:384 + 128 * j]],
                         [_with_ones(pb[:, 512 + 128 * j:640 + 128 * j])], lam, low)
        o_ref[:, 256 + 128 * j:384 + 128 * j] = (_rms_halves(o, low, subln_ref[...]) * out_scale).astype(BF16)
    for h in range(4):
        kb_o[h] = pb[:, 256 + 64 * h:320 + 64 * h].astype(F32)
        vb_o[h] = pb[:, 512 + 64 * h:576 + 64 * h].astype(F32)
    kpe4 = pc[:, 384:512]
    for j in range(2):
        qn, kcat = pc[:, 128 * j:128 * j + 128], jnp.concatenate([pc[:, 512 + 128 * j:640 + 128 * j], kpe4], axis=1)
        qcat = jnp.concatenate([qn, pc[:, 256:384]], axis=1)
        q = jnp.concatenate([_keep_lanes(qcat, (64 * hh, 64), (128 + 32 * (2 * j + hh), 32)) for hh in range(2)], axis=0)
        r = _attend(q, [kcat], [_with_ones(pc[:, 768 + 128 * j:896 + 128 * j])])
        o_ref[:, 512 + 128 * j:640 + 128 * j] = jnp.where(low, r[0:n], r[n:]).astype(BF16)
    ckv_o[...] = pc[:, 1024:1152].astype(F32)
    kpe_o[...] = pc[:, 384:416].astype(F32)
    for j in range(2):
        qp = pd[:, 128 * j:128 * j + 128]
        q = jnp.concatenate([_keep_lanes(qp, (0, 64)), _keep_lanes(qp, (64, 64))], axis=0)
        r = _attend(q, [pd[:, 256 + 128 * j:384 + 128 * j]], [_with_ones(pd[:, 512 + 128 * j:640 + 128 * j])])
        o_ref[:, 768 + 128 * j:896 + 128 * j] = jnp.where(low, r[0:n], r[n:]).astype(BF16)
    for h in range(4):
        kd_o[h] = pd[:, 256 + 64 * h:320 + 64 * h].astype(F32)
        vd_o[h] = pd[:, 512 + 64 * h:576 + 64 * h].astype(F32)


def _attn_ctx(l, sink, lam, pa, pb, pc, pd, subln, carried):
    row = lambda b: (b, 0)
    smem = pl.BlockSpec(memory_space=pltpu.SMEM)
    hd4 = lambda n: pl.BlockSpec((None, None, n, L_CTX, HD), lambda b: (b, l, 0, 0, 0))
    hshape = lambda n: jax.ShapeDtypeStruct((N_CTX_B, DEPTH, n, L_CTX, HD), F32)
    n_in = 7
    return pl.pallas_call(
        functools.partial(_attn_ctx_kernel, len(carried)),
        grid=(N_CTX_B,),
        in_specs=[smem, smem,
                  pl.BlockSpec((L_CTX, 512), row), pl.BlockSpec((L_CTX, 768), row),
                  pl.BlockSpec((L_CTX, PC_W), row), pl.BlockSpec((L_CTX, 768), row),
                  pl.BlockSpec((1, 128), lambda b: (0, 0))] + [pl.BlockSpec(memory_space=pl.ANY)] * len(carried),
        out_specs=[pl.BlockSpec((L_CTX, D), row),
                   hd4(2), hd4(2), hd4(4), hd4(4),
                   pl.BlockSpec((None, None, L_CTX, 128), lambda b: (b, l, 0, 0)),
                   pl.BlockSpec((None, None, L_CTX, 32), lambda b: (b, l, 0, 0)),
                   hd4(4), hd4(4)],
        out_shape=[jax.ShapeDtypeStruct((T_CTX, D), BF16),
                   hshape(2), hshape(2), hshape(4), hshape(4),
                   jax.ShapeDtypeStruct((N_CTX_B, DEPTH, L_CTX, 128), F32),
                   jax.ShapeDtypeStruct((N_CTX_B, DEPTH, L_CTX, 32), F32),
                   hshape(4), hshape(4)],
        input_output_aliases={n_in + i: 1 + i for i in range(len(carried))},
        compiler_params=_cparams(("arbitrary",)),
        name="attn_ctx",
    )(sink, lam, pa, pb, pc, pd, subln, *carried)


_LAT_ROW = lambda b: (T_CTX // N_LAT + b, 0)


def _cache_spec(l, n):
    return pl.BlockSpec((None, None, n, HD, L_CTX), lambda b: (b, l, 0, 0, 0))


def _load_cache_pair(ck_ref, cv_ref, j, kcs, vcs):
    for hh in range(2):
        kcs[64 * hh:64 * hh + 64, :] = ck_ref[2 * j + hh].astype(BF16)
        vcs[64 * hh:64 * hh + 64, :] = cv_ref[2 * j + hh].astype(BF16)
    vcs[128:256, :] = jnp.ones((128, L_CTX), BF16)


def _tab_spec():
    return pl.BlockSpec((N_LAT, 128), lambda b: (0, 0))


def _lat_win_kernel(sink_ref, pa, ck_ref, cv_ref, c_ref, sa_ref, sb_ref, o_ref, qs, kpad, vpad, kcs, vcs):
    c, sa, sb = c_ref[...], sa_ref[...], sb_ref[...]
    for j in range(2):
        qs[:, 128 * j:128 * j + 128] = _rope(pa[:, 128 * j:128 * j + 128].astype(F32), c, sa, sb, 16).astype(BF16)
    zeros = jnp.zeros((WINDOW, 128), BF16)
    kpad[0:WINDOW, :] = zeros
    kpad[WINDOW:WINDOW + N_LAT, :] = _rope(pa[:, 256:384].astype(F32), c, sa, sb, 16).astype(BF16)
    kpad[WINDOW + N_LAT:, :] = zeros
    vpad[0:WINDOW, :] = zeros
    vpad[WINDOW:WINDOW + N_LAT, :] = pa[:, 384:512]
    vpad[WINDOW + N_LAT:, :] = zeros
    _load_cache_pair(ck_ref, cv_ref, 0, kcs, vcs)
    qi = lax.broadcasted_iota(jnp.int32, (2 * WINDOW, 3 * WINDOW), 0) % WINDOW
    kj = lax.broadcasted_iota(jnp.int32, (2 * WINDOW, 3 * WINDOW), 1)
    band = jnp.abs(kj - WINDOW - qi) <= WINDOW
    first_head = lax.broadcasted_iota(jnp.int32, (2 * WINDOW, 1), 0) < WINDOW
    low = _low_half(WINDOW)

    def body(nb, carry):
        r0 = pl.multiple_of(nb * WINDOW, WINDOW)
        kpos = r0 - WINDOW + kj
        valid = band & (kpos >= 0) & (kpos < N_LAT)
        kw = kpad[pl.ds(r0, 3 * WINDOW), :]
        vexts = [_with_ones(vpad[pl.ds(r0, 3 * WINDOW), :]), _T(vcs[...])]
        res = []
        for g in range(2):
            q = jnp.concatenate([_keep_lanes(qs[pl.ds(r0, WINDOW), 0:128], (64 * g, 64)),
                                 _keep_lanes(qs[pl.ds(r0, WINDOW), 128:256], (64 * g, 64))], axis=0)
            s_loc = jnp.where(valid, _dot_t(q, kw), NEG_INF)
            sink = jnp.where(first_head, sink_ref[2 * g], sink_ref[2 * g + 1])
            res.append(_attend_scores([s_loc, _dot(q, kcs[...])], vexts, sink))
        o_ref[pl.ds(r0, WINDOW), 0:128] = jnp.where(low, res[0][0:WINDOW], res[1][0:WINDOW]).astype(BF16)
        o_ref[pl.ds(r0, WINDOW), 128:256] = jnp.where(low, res[0][WINDOW:], res[1][WINDOW:]).astype(BF16)
        return carry

    lax.fori_loop(0, N_LAT // WINDOW, body, 0, unroll=2)


def _lat_win(l, sink, pa, ck, cv, tabs):
    return pl.pallas_call(
        _lat_win_kernel,
        grid=(N_LAT_B,),
        in_specs=[pl.BlockSpec(memory_space=pltpu.SMEM),
                  pl.BlockSpec((N_LAT, 512), _LAT_ROW),
                  _cache_spec(l, 2), _cache_spec(l, 2),
                  _tab_spec(), _tab_spec(), _tab_spec()],
        out_specs=pl.BlockSpec((N_LAT, 256), lambda b: (b, 0)),
        out_shape=jax.ShapeDtypeStruct((T_LAT, 256), BF16),
        scratch_shapes=[pltpu.VMEM((N_LAT, 256), BF16),
                        pltpu.VMEM((N_LAT + 2 * WINDOW, 128), BF16),
                        pltpu.VMEM((N_LAT + 2 * WINDOW, 128), BF16),
                        pltpu.VMEM((128, L_CTX), BF16),
                        pltpu.VMEM((256, L_CTX), BF16)],
        compiler_params=_cparams(("arbitrary",)),
        name="lat_win",
    )(sink, pa, ck, cv, *tabs)


QT = 256
DIFF_QT = 256
N_KEYS = N_LAT + L_CTX


def _lat_diff_kernel(lam_ref, pb, ck_ref, cv_ref, subln_ref, c_ref, sa_ref, sb_ref, o_ref, qs, klat, kcs, vcs):
    c, sa, sb = c_ref[...], sa_ref[...], sb_ref[...]
    for j in range(2):
        qs[:, 128 * j:128 * j + 128] = _rope(pb[:, 128 * j:128 * j + 128].astype(F32), c, sa, sb, 8).astype(BF16)
        klat[j] = _rope(pb[:, 256 + 128 * j:384 + 128 * j].astype(F32), c, sa, sb, 8).astype(BF16)
        _load_cache_pair(ck_ref, cv_ref, j, kcs.at[j], vcs.at[j])
    lam = lam_ref[0]
    out_scale = lam_ref[1]
    low = _low_half(DIFF_QT)
    n = DIFF_QT

    def body(t, carry):
        r0 = pl.multiple_of(t * n, n)
        for j in range(2):
            o = _attend_diff(qs[pl.ds(r0, n), 128 * j:128 * j + 128], [klat[j], _T(kcs[j])],
                             [_with_ones(pb[:, 512 + 128 * j:640 + 128 * j]), _T(vcs[j])], lam, low)
            o_ref[pl.ds(r0, n), 128 * j:128 * j + 128] = (_rms_halves(o, low, subln_ref[...]) * out_scale).astype(BF16)
        return carry

    lax.fori_loop(0, N_LAT // n, body, 0)


def _lat_diff(l, lam, pb, ck, cv, subln, tabs):
    return pl.pallas_call(
        _lat_diff_kernel,
        grid=(N_LAT_B,),
        in_specs=[pl.BlockSpec(memory_space=pltpu.SMEM),
                  pl.BlockSpec((N_LAT, 768), _LAT_ROW),
                  _cache_spec(l, 4), _cache_spec(l, 4),
                  pl.BlockSpec((1, 128), lambda b: (0, 0)),
                  _tab_spec(), _tab_spec(), _tab_spec()],
        out_specs=pl.BlockSpec((N_LAT, 256), lambda b: (b, 0)),
        out_shape=jax.ShapeDtypeStruct((T_LAT, 256), BF16),
        scratch_shapes=[pltpu.VMEM((N_LAT, 256), BF16),
                        pltpu.VMEM((2, N_LAT, 128), BF16),
                        pltpu.VMEM((2, 128, L_CTX), BF16),
                        pltpu.VMEM((2, 256, L_CTX), BF16)],
        compiler_params=_cparams(("arbitrary",)),
        name="lat_diff",
    )(lam, pb, ck, cv, subln, *tabs)


def _lat_mla_kernel(pc, ckv_ref, kpe_ref, wukv_ref, c_ref, sa_ref, sb_ref, o_ref, qcat, kcat, kvc, kpe4t):
    c, sa, sb = c_ref[...], sa_ref[...], sb_ref[...]
    qr = _rope(pc[:, 256:384].astype(F32), c, sa, sb, 8).astype(BF16)
    kpe_l = _rope(pc[:, 384:512].astype(F32), c, sa, sb, 8).astype(BF16)
    kvc[...] = _dot(ckv_ref[...].astype(BF16), wukv_ref[...]).astype(BF16)
    for i in range(4):
        kpe4t[32 * i:32 * i + 32, :] = kpe_ref[...].astype(BF16)
    for j in range(2):
        qcat[j, :, 0:128] = pc[:, 128 * j:128 * j + 128]
        qcat[j, :, 128:256] = qr
        kcat[j, :, 0:128] = pc[:, 512 + 128 * j:640 + 128 * j]
        kcat[j, :, 128:256] = kpe_l
    low = _low_half(QT)

    def body(t, carry):
        r0 = pl.multiple_of(t * QT, QT)
        for j in range(2):
            q2 = qcat[j, pl.ds(r0, QT), :]
            q = jnp.concatenate([_keep_lanes(q2, (64 * hh, 64), (128 + 32 * (2 * j + hh), 32)) for hh in range(2)],
                                axis=0)
            s_ctx = _dot_t(q[:, 0:128], kvc[:, 128 * j:128 * j + 128]) + _dot(q[:, 128:256], kpe4t[...])
            r = _attend_scores([_dot_t(q, kcat[j]), s_ctx],
                               [_with_ones(pc[:, 768 + 128 * j:896 + 128 * j]),
                                _with_ones(kvc[:, 256 + 128 * j:384 + 128 * j])])
            o_ref[pl.ds(r0, QT), 128 * j:128 * j + 128] = jnp.where(low, r[0:QT], r[QT:]).astype(BF16)
        return carry

    lax.fori_loop(0, N_LAT // QT, body, 0)


def _lat_mla(l, pc, ckv, kpe, wukv_p, tabs):
    return pl.pallas_call(
        _lat_mla_kernel,
        grid=(N_LAT_B,),
        in_specs=[pl.BlockSpec((N_LAT, PC_W), _LAT_ROW),
                  pl.BlockSpec((None, None, L_CTX, 128), lambda b: (b, l, 0, 0)),
                  pl.BlockSpec((None, None, 32, L_CTX), lambda b: (b, l, 0, 0)),
                  pl.BlockSpec((None, 128, 512), lambda b: (l, 0, 0)),
                  _tab_spec(), _tab_spec(), _tab_spec()],
        out_specs=pl.BlockSpec((N_LAT, 256), lambda b: (b, 0)),
        out_shape=jax.ShapeDtypeStruct((T_LAT, 256), BF16),
        scratch_shapes=[pltpu.VMEM((2, N_LAT, 256), BF16),
                        pltpu.VMEM((2, N_LAT, 256), BF16),
                        pltpu.VMEM((L_CTX, 512), BF16),
                        pltpu.VMEM((128, L_CTX), BF16)],
        compiler_params=_cparams(("arbitrary",)),
        name="lat_mla",
    )(pc, ckv, kpe, wukv_p, *tabs)


NA_ROWS = 8
NA_KEYS = NA_ROWS * GRID_W


def _lat_na_kernel(pd, ck_ref, cv_ref, bias_ref, o_ref, kcs, vcs):
    n_rows = N_LAT // GRID_W
    for j in range(2):
        _load_cache_pair(ck_ref, cv_ref, j, kcs.at[j], vcs.at[j])
    low = _low_half(GRID_W)

    def body(r, carry):
        rs = jnp.clip(r - NA_ROWS // 2, 0, n_rows - NA_ROWS)
        q0 = pl.multiple_of(r * GRID_W, GRID_W)
        k0 = pl.multiple_of(rs * GRID_W, GRID_W)
        first = NA_ROWS - 1 - (r - rs)
        for j in range(2):
            qp = pd[pl.ds(q0, GRID_W), 128 * j:128 * j + 128]
            q = jnp.concatenate([_keep_lanes(qp, (0, 64)), _keep_lanes(qp, (64, 64))], axis=0)
            bias = jnp.concatenate(
                [jnp.concatenate([bias_ref[2 * j + hh, first + 2 * m] for m in range(NA_ROWS // 2)], axis=1)
                 for hh in range(2)], axis=0)
            s_loc = _dot_t(q, pd[pl.ds(k0, NA_KEYS), 256 + 128 * j:384 + 128 * j]) + bias
            s_ctx = _dot(q, kcs[j])
            res = _attend_scores([s_loc, s_ctx],
                                 [_with_ones(pd[pl.ds(k0, NA_KEYS), 512 + 128 * j:640 + 128 * j]), _T(vcs[j])])
            o_ref[pl.ds(q0, GRID_W), 128 * j:128 * j + 128] = jnp.where(low, res[0:GRID_W], res[GRID_W:]).astype(BF16)
        return carry

    lax.fori_loop(0, n_rows, body, 0, unroll=2)


def _lat_na(l, pd, ck, cv, bias):
    return pl.pallas_call(
        _lat_na_kernel,
        grid=(N_LAT_B,),
        in_specs=[pl.BlockSpec((N_LAT, 768), _LAT_ROW),
                  _cache_spec(l, 4), _cache_spec(l, 4),
                  pl.BlockSpec((None, 4, 2 * NA_ROWS - 2, GRID_W, 2 * GRID_W), lambda b: (l, 0, 0, 0, 0))],
        out_specs=pl.BlockSpec((N_LAT, 256), lambda b: (b, 0)),
        out_shape=jax.ShapeDtypeStruct((T_LAT, 256), BF16),
        scratch_shapes=[pltpu.VMEM((2, 128, L_CTX), BF16), pltpu.VMEM((2, 256, L_CTX), BF16)],
        compiler_params=_cparams(("arbitrary",)),
        name="lat_na",
    )(pd, ck, cv, bias)


def _out_proj_kernel(with_router, n_x, *refs):
    x_refs = refs[:n_x]
    refs = refs[n_x:]
    if with_router:
        (oc_ref, oa_ref, ob_ref, om_ref, od_ref, mod_ref, wo_ref, gpost_ref, gffn_ref, r_ref,
         x1_ref, h2_ref, route_ref, cnt_ref) = refs
    else:
        (oc_ref, oa_ref, ob_ref, om_ref, od_ref, mod_ref, wo_ref, gpost_ref, gffn_ref,
         x1_ref, h2_ref) = refs
    i = pl.program_id(0)
    nct = T_CTX // TM

    if with_router:
        @pl.when(i == 0)
        def _():
            cnt_ref[...] = jnp.zeros(cnt_ref.shape, F32)

    def finish(y):
        m = mod_ref[...]
        x1 = _x_tile(x_refs) + m[2:3] * _rms(y, gpost_ref[...])
        x1_ref[...] = x1
        h2 = _rms(x1, gffn_ref[...]) * (1.0 + m[4:5]) + m[3:4]
        h2_ref[...] = h2.astype(h2_ref.dtype)
        if with_router:
            lg = _dot(h2.astype(BF16), r_ref[...])
            lane = lax.broadcasted_iota(jnp.int32, lg.shape, 1)
            lane_f = lane.astype(F32)
            lg = jnp.where(lane < N_EXPERTS, lg, -jnp.inf)
            m1 = jnp.max(lg, axis=-1, keepdims=True)
            i1 = jnp.min(jnp.where(lg == m1, lane_f, 128.0), axis=-1, keepdims=True)
            lg2 = jnp.where(lane_f == i1, -jnp.inf, lg)
            m2 = jnp.max(lg2, axis=-1, keepdims=True)
            i2 = jnp.min(jnp.where(lg2 == m2, lane_f, 128.0), axis=-1, keepdims=True)
            e = jnp.exp(m2 - m1)
            g1 = 1.0 / (1.0 + e)
            g2 = e / (1.0 + e)
            sel1 = lane_f == i1
            sel2 = lane_f == i2
            picked = jnp.where(sel1 | sel2, 1.0, 0.0)
            ri = lax.broadcasted_iota(jnp.int32, (TM, TM), 0)
            ci = lax.broadcasted_iota(jnp.int32, (TM, TM), 1)
            before = _dot(jnp.where(ci < ri, 1.0, 0.0).astype(BF16), picked.astype(BF16)) + cnt_ref[0:1, :]
            r1 = jnp.sum(jnp.where(sel1, before, 0.0), axis=-1, keepdims=True)
            r2 = jnp.sum(jnp.where(sel2, before, 0.0), axis=-1, keepdims=True)
            cnt_ref[...] = cnt_ref[...] + jnp.sum(picked, axis=0, keepdims=True)
            route = jnp.where(lane == 0, i1, jnp.where(lane == 1, i2, jnp.where(lane == 2, g1, 0.0)))
            route_ref[...] = jnp.where(lane == 3, g2, jnp.where(lane == 4, r1, jnp.where(lane == 5, r2, route)))

    @pl.when(i < nct)
    def _():
        finish(_dot(oc_ref[...], wo_ref[...]))

    @pl.when(i >= nct)
    def _():
        y = _dot(oa_ref[...], wo_ref[0:256, :])
        y = y + _dot(ob_ref[...], wo_ref[256:512, :])
        y = y + _dot(om_ref[...], wo_ref[512:768, :])
        y = y + _dot(od_ref[...], wo_ref[768:1024, :])
        finish(y)


def _out_proj(l, o_ctx, o_lat, xs, mods, w_o, g_post, g_ffn, router_p):
    nct = T_CTX // TM
    row = lambda i: (i, 0)
    lyr2 = lambda i: (l, 0, 0)
    lat = lambda i: (jnp.maximum(i - nct, 0), 0)
    with_router = router_p is not None
    in_specs = _x_specs(xs) + [
                pl.BlockSpec((TM, D), lambda i: (jnp.minimum(i, nct - 1), 0)),
                pl.BlockSpec((TM, 256), lat), pl.BlockSpec((TM, 256), lat),
                pl.BlockSpec((TM, 256), lat), pl.BlockSpec((TM, 256), lat),
                pl.BlockSpec((None, None, 6, D), lambda i: (l, _mod_row(i), 0, 0)),
                pl.BlockSpec((None, D, D), lyr2),
                pl.BlockSpec((None, 1, D), lyr2),
                pl.BlockSpec((None, 1, D), lyr2)]
    out_specs = [pl.BlockSpec((TM, D), row), pl.BlockSpec((TM, D), row)]
    out_shape = [jax.ShapeDtypeStruct((T, D), F32), jax.ShapeDtypeStruct((T, D), F32 if with_router else BF16)]
    args = [*xs, o_ctx, *o_lat, mods, w_o, g_post, g_ffn]
    if with_router:
        in_specs.append(pl.BlockSpec((D, 128), lambda i: (0, 0)))
        out_specs += [pl.BlockSpec((TM, 128), row), pl.BlockSpec((8, 128), lambda i: (0, 0))]
        out_shape += [jax.ShapeDtypeStruct((T, 128), F32), jax.ShapeDtypeStruct((8, 128), F32)]
        args.append(router_p)
    return pl.pallas_call(
        functools.partial(_out_proj_kernel, with_router, len(xs)),
        grid=(T // TM,),
        in_specs=in_specs, out_specs=out_specs, out_shape=out_shape,
        compiler_params=_cparams(("arbitrary",)),
        name="out_proj_router" if with_router else "out_proj",
    )(*args)


FFN_TM = 512


def _ffn_kernel(h_ref, x_ref, mod_ref, g_ref, wg_ref, wu_ref, wd_ref, o_ref):
    h = h_ref[...]
    acc = None
    lo = 0
    for width in FFN_SLICES:
        t = (_silu(_dot(h, wg_ref[:, lo:lo + width])) * _dot(h, wu_ref[:, lo:lo + width])).astype(BF16)
        part = _dot(t, wd_ref[lo:lo + width, :])
        acc = part if acc is None else acc + part
        lo += width
    m = mod_ref[...]
    o_ref[...] = x_ref[...] + m[5:6] * _rms(acc, g_ref[...])


def _ffn_mod_row(i):
    nct = T_CTX // FFN_TM
    return jnp.where(i >= nct, 1 + (i - nct) // (N_LAT // FFN_TM), 0)


def _post_mod_row(i):
    nct = T_CTX // POST_TM
    return jnp.where(i >= nct, 1 + (i - nct) // (N_LAT // POST_TM), 0)


def _ffn(l, i_dense, h2, x1, mods, g_post, wg, wu, wd):
    row = lambda i: (i, 0)
    resident = lambda shape: pl.BlockSpec((None,) + shape, lambda i: (i_dense, 0, 0), pipeline_mode=pl.Buffered(1))
    return pl.pallas_call(
        _ffn_kernel,
        grid=(T // FFN_TM,),
        in_specs=[pl.BlockSpec((FFN_TM, D), row),
                  pl.BlockSpec((FFN_TM, D), row),
                  pl.BlockSpec((None, None, 6, D), lambda i: (l, _ffn_mod_row(i), 0, 0)),
                  pl.BlockSpec((None, 1, D), lambda i: (l, 0, 0)),
                  resident((D, D_FF)), resident((D, D_FF)), resident((D_FF, D))],
        out_specs=pl.BlockSpec((FFN_TM, D), row),
        out_shape=jax.ShapeDtypeStruct((T, D), F32),
        compiler_params=_cparams(("arbitrary",)),
        name="ffn",
    )(h2, x1, mods, g_post, wg, wu, wd)


def _dispatch_kernel(slot_ref, h_ref, xs_hbm, sem):
    def row_copy(g, r, k):
        return pltpu.make_async_copy(h_ref.at[g, pl.ds(r, 1), :],
                                     xs_hbm.at[pl.ds(slot_ref[0, 16 * g + 2 * r + k], 1)], sem)

    def issue(g, carry):
        for r in range(8):
            row_copy(g, r, 0).start()
            row_copy(g, r, 1).start()
        return carry

    def drain(g, carry):
        for r in range(8):
            row_copy(g, r, 0).wait()
            row_copy(g, r, 1).wait()
        return carry

    lax.fori_loop(0, DISPATCH_TM // 8, issue, 0)
    lax.fori_loop(0, DISPATCH_TM // 8, drain, 0)


def _dispatch(slots, h2):
    return pl.pallas_call(
        _dispatch_kernel,
        grid=(T // DISPATCH_TM,),
        in_specs=[pl.BlockSpec((None, 1, 2 * DISPATCH_TM), lambda i: (i, 0, 0), memory_space=pltpu.SMEM),
                  pl.BlockSpec((DISPATCH_TM // 8, 8, D), lambda i: (i, 0, 0))],
        out_specs=pl.BlockSpec(memory_space=pl.ANY),
        out_shape=jax.ShapeDtypeStruct((N_EXPERTS * EXPERT_CAP, D), F32),
        scratch_shapes=[pltpu.SemaphoreType.DMA(())],
        compiler_params=_cparams(("arbitrary",)),
        name="moe_dispatch",
    )(slots.reshape(T // DISPATCH_TM, 1, 2 * DISPATCH_TM), h2.reshape(T // 8, 8, D))


def _moe_kernel(ie_ref, ib_ref, nr_ref, x_ref, wg_ref, wu_ref, wd_ref, o_ref, wgb, wub, wdb):
    i = pl.program_id(0)
    c = pl.program_id(1)
    n_rows = nr_ref[i]

    @pl.when(n_rows > 0)
    def _():
        @pl.when(c == 0)
        def _():
            o_ref[...] = jnp.zeros(o_ref.shape, F32)

        wgb[...] = wg_ref[...].astype(BF16)
        wub[...] = wu_ref[...].astype(BF16)
        wdb[...] = wd_ref[...].astype(BF16)

        def step(r0, rows):
            live = r0 + lax.broadcasted_iota(jnp.int32, (rows, 1), 0) < n_rows
            xs = jnp.where(live, x_ref[pl.ds(r0, rows), :], 0.0).astype(BF16)
            t = (_silu(_dot(xs, wgb[...])) * _dot(xs, wub[...])).astype(BF16)
            o_ref[pl.ds(r0, rows), :] += _dot(t, wdb[...])

        @pl.when(n_rows == ITEM_ROWS)
        def _():
            step(0, ITEM_ROWS)

        @pl.when(n_rows < ITEM_ROWS)
        def _():
            whole = lax.shift_right_logical(n_rows, SUB_ROWS.bit_length() - 1)
            left = jnp.bitwise_and(n_rows, SUB_ROWS - 1)
            n_full = whole + (left > TAIL_ROWS).astype(jnp.int32)

            def body(s, carry):
                step(pl.multiple_of(s * SUB_ROWS, SUB_ROWS), SUB_ROWS)
                return carry

            lax.fori_loop(0, n_full, body, 0)

            @pl.when((left > 0) & (left <= TAIL_ROWS))
            def _():
                step(pl.multiple_of(whole * SUB_ROWS, SUB_ROWS), TAIL_ROWS)


def _moe(i_moe, item_e, item_blk, item_rows, x_sorted, wg, wu, wd):
    nc = D_FF_EXPERT // FF_CHUNK
    chunk = lambda i, c, nr: jnp.where(nr[i] > 0, c, nc - 1)
    grid_spec = pltpu.PrefetchScalarGridSpec(
        num_scalar_prefetch=3,
        grid=(N_ITEMS, nc),
        in_specs=[pl.BlockSpec((ITEM_ROWS, D), lambda i, c, ie, ib, nr: (ib[i], 0)),
                  pl.BlockSpec((None, None, D, FF_CHUNK),
                               lambda i, c, ie, ib, nr: (i_moe, ie[i], 0, chunk(i, c, nr))),
                  pl.BlockSpec((None, None, D, FF_CHUNK),
                               lambda i, c, ie, ib, nr: (i_moe, ie[i], 0, chunk(i, c, nr))),
                  pl.BlockSpec((None, None, FF_CHUNK, D),
                               lambda i, c, ie, ib, nr: (i_moe, ie[i], chunk(i, c, nr), 0))],
        out_specs=pl.BlockSpec((ITEM_ROWS, D), lambda i, c, ie, ib, nr: (ib[i], 0)),
        scratch_shapes=[pltpu.VMEM((D, FF_CHUNK), BF16), pltpu.VMEM((D, FF_CHUNK), BF16),
                        pltpu.VMEM((FF_CHUNK, D), BF16)],
    )
    return pl.pallas_call(
        _moe_kernel,
        grid_spec=grid_spec,
        out_shape=jax.ShapeDtypeStruct((N_EXPERTS * EXPERT_CAP, D), F32),
        compiler_params=_cparams(("arbitrary", "arbitrary")),
        name="moe_experts",
    )(item_e, item_blk, item_rows, x_sorted, wg, wu, wd)


def _post_kernel(final, slot_ref, ys_hbm, route_ref, x_ref, mod_ref, g_ref, *refs):
    buf, sem = refs[-2:]

    def row_copy(g, r, k):
        return pltpu.make_async_copy(ys_hbm.at[pl.ds(slot_ref[0, 16 * g + 2 * r + k], 1)],
                                     buf.at[k, g, pl.ds(r, 1), :], sem)

    def issue(g, carry):
        for r in range(8):
            row_copy(g, r, 0).start()
            row_copy(g, r, 1).start()
        return carry

    def drain(g, carry):
        for r in range(8):
            row_copy(g, r, 0).wait()
            row_copy(g, r, 1).wait()
        return carry

    lax.fori_loop(0, POST_TM // 8, issue, 0)
    lax.fori_loop(0, POST_TM // 8, drain, 0)
    r = route_ref[...]
    y = r[:, 2:3] * buf[0].reshape(POST_TM, D) + r[:, 3:4] * buf[1].reshape(POST_TM, D)
    m = mod_ref[...]
    out = x_ref[...] + m[5:6] * _rms(y, g_ref[...])
    if not final:
        refs[0][...] = out
    else:
        i = pl.program_id(0)

        @pl.when(i < T_CTX // POST_TM)
        def _():
            refs[0][...] = out

        @pl.when(i >= T_CTX // POST_TM)
        def _():
            refs[1][...] = out


def _post(l, slots, y_sorted, route, x1, mods, g_post, final):
    row = lambda i: (i, 0)
    nct = T_CTX // POST_TM
    if final:
        out_specs = [pl.BlockSpec((POST_TM, D), lambda i: (jnp.minimum(i, nct - 1), 0)),
                     pl.BlockSpec((POST_TM, D), lambda i: (jnp.maximum(i - nct, 0), 0))]
        out_shape = [jax.ShapeDtypeStruct((T_CTX, D), F32), jax.ShapeDtypeStruct((T_LAT, D), F32)]
    else:
        out_specs = pl.BlockSpec((POST_TM, D), row)
        out_shape = jax.ShapeDtypeStruct((T, D), F32)
    return pl.pallas_call(
        functools.partial(_post_kernel, final),
        grid=(T // POST_TM,),
        in_specs=[pl.BlockSpec((None, 1, 2 * POST_TM), lambda i: (i, 0, 0), memory_space=pltpu.SMEM),
                  pl.BlockSpec(memory_space=pl.ANY),
                  pl.BlockSpec((POST_TM, 128), row), pl.BlockSpec((POST_TM, D), row),
                  pl.BlockSpec((None, None, 6, D), lambda i: (l, _post_mod_row(i), 0, 0)),
                  pl.BlockSpec((None, 1, D), lambda i: (l, 0, 0))],
        out_specs=out_specs,
        out_shape=out_shape,
        scratch_shapes=[pltpu.VMEM((2, POST_TM // 8, 8, D), F32), pltpu.SemaphoreType.DMA(())],
        compiler_params=_cparams(("arbitrary",)),
        name="moe_post_final" if final else "moe_post",
    )(slots.reshape(T // POST_TM, 1, 2 * POST_TM), y_sorted, route, x1, mods, g_post)


def _route_plan(route, counts):
    i32 = jnp.int32
    slots = (route[:, 0:2].astype(i32) * EXPERT_CAP + route[:, 4:6].astype(i32)).reshape(-1)
    n = counts[0, :N_EXPERTS].astype(i32)
    n_items = (n + ITEM_ROWS - 1) // ITEM_ROWS
    ends = jnp.cumsum(n_items)
    base = ends - n_items
    total = ends[-1]
    ii = jnp.arange(N_ITEMS, dtype=i32)
    ii_c = jnp.minimum(ii, total - 1)
    item_e = jnp.sum((ends[None, :] <= ii_c[:, None]).astype(i32), axis=1)
    j = ii_c - base[item_e]
    item_blk = item_e * (EXPERT_CAP // ITEM_ROWS) + j
    rows = jnp.where(ii < total, jnp.clip(n[item_e] - j * ITEM_ROWS, 0, ITEM_ROWS), 0)
    return slots, item_e.astype(i32), item_blk.astype(i32), rows.astype(i32)


def _rope_tables(d):
    half, quarter = d // 2, d // 4
    t = jnp.arange(N_LAT)
    pr, pc = t // GRID_W, t % GRID_W
    i = jnp.arange(128) % d
    j = (i % half) % quarter
    inv = ROPE_THETA ** (-(2 * j).astype(F32) / half)
    pos = jnp.where((i < half)[None, :], pr[:, None], pc[:, None]).astype(F32)
    ang = pos * inv[None, :]
    low = ((i % half) < quarter)[None, :]
    sin = jnp.sin(ang)
    return jnp.cos(ang), jnp.where(low, -sin, 0.0), jnp.where(low, 0.0, sin)


def _na_bias_table(rpb):
    q = jnp.arange(GRID_W)
    k = jnp.arange(GRID_W)
    rel_c = jnp.clip(k[None, :] - q[:, None], -15, 15) + 15
    cs = jnp.clip(q - 8, 0, GRID_W - 16)
    valid = (k[None, :] >= cs[:, None]) & (k[None, :] < cs[:, None] + 16)
    pick = rel_c[:, :, None] == jnp.arange(31)[None, None, :]
    band = jnp.sum(jnp.where(pick[None, None, None], rpb[:, :, :, None, None, :], 0.0), axis=-1)
    band = jnp.where(valid[None, None, None], band, NEG_INF)
    return jnp.concatenate([band[:, :, :-1], band[:, :, 1:]], axis=-1)


def _pad_in_proj(w_in):
    wt = jnp.swapaxes(w_in, 1, 2)
    z = lambda n: jnp.zeros((DEPTH, n, D), w_in.dtype)
    s64, s32 = HD ** -0.5, (HD // 2) ** -0.5
    qa = wt[:, 0:256] * s64
    kpe = wt[:, 1600:1632]
    return jnp.concatenate([qa[:, 0:64], qa[:, 128:192], qa[:, 64:128], qa[:, 192:256],
                            wt[:, 256:512],
                            wt[:, 512:768] * s32, wt[:, 768:1280],
                            wt[:, 1472:1600], kpe, kpe, kpe, kpe,
                            wt[:, 1280:1472], z(64),
                            wt[:, 1632:1888] * s64, wt[:, 1888:2400]], axis=1).astype(BF16)


def _split_heads_cols(w, first, second):
    lead = w.shape[:-1]
    w4 = w.reshape(*lead, 4, first + second)
    return jnp.concatenate([w4[..., :first].reshape(*lead, 4 * first),
                            w4[..., first:].reshape(*lead, 4 * second)], axis=-1)


def kernel(x_prompt, x_sample, cache_win_k, cache_win_v, cache_diff_k, cache_diff_v, cache_mla_ckv, cache_mla_kpe, cache_na_k, cache_na_v, c, c_ctx, w_ada, b_ada, norm_mix_pre, norm_mix_post, norm_ffn_pre, norm_ffn_post, w_in, w_o, win_sink, diff_lq1, diff_lk1, diff_lq2, diff_lk2, diff_subln, mla_q_norm, mla_kv_norm, mla_wuq, mla_wukv, na_rpb, ffn_w_gate, ffn_w_up, ffn_w_down, moe_router, moe_w_gate, moe_w_up, moe_w_down):
    w_in_p = _pad_in_proj(w_in)
    wuq_p = jnp.pad(_split_heads_cols(mla_wuq, 64, 32) * (HD + HD // 2) ** -0.5,
                    ((0, 0), (0, 256 - MLA_Q_RANK), (0, 0))).astype(BF16)
    wukv_p = _split_heads_cols(mla_wukv, 64, 64).astype(BF16)
    qn_p = jnp.pad(mla_q_norm, ((0, 0), (0, 256 - MLA_Q_RANK))).reshape(DEPTH, 1, 256)
    kvn = mla_kv_norm.reshape(DEPTH, 1, 128)
    w_o_b = jnp.concatenate([w_o[:, 0:64], w_o[:, 128:192], w_o[:, 64:128], w_o[:, 192:]], axis=1).astype(BF16)
    ffn_wg, ffn_wu, ffn_wd = ffn_w_gate.astype(BF16), ffn_w_up.astype(BF16), ffn_w_down.astype(BF16)
    router_p = jnp.pad(moe_router, ((0, 0), (0, 0), (0, 128 - N_EXPERTS))).astype(BF16)
    g_mix_pre = norm_mix_pre.reshape(DEPTH, 1, D)
    g_mix_post = norm_mix_post.reshape(DEPTH, 1, D)
    g_ffn_pre = norm_ffn_pre.reshape(DEPTH, 1, D)
    g_ffn_post = norm_ffn_post.reshape(DEPTH, 1, D)
    tabs64 = _rope_tables(64)
    tabs32 = _rope_tables(32)
    na_bias = _na_bias_table(na_rpb)
    lam_init = jnp.asarray([0.8 - 0.6 * math.exp(-0.3 * l) for l in range(DEPTH)], F32)
    lam = (jnp.exp(jnp.sum(diff_lq1 * diff_lk1, axis=-1)) - jnp.exp(jnp.sum(diff_lq2 * diff_lk2, axis=-1))
           + lam_init)
    lam_tab = jnp.stack([lam, 1.0 - lam_init], axis=-1)
    subln = jnp.tile(diff_subln, (1, 2)).reshape(DEPTH, 1, 2 * HD)

    cond = jnp.concatenate([c_ctx[None], c, jnp.zeros((3, D), F32)], axis=0)
    mods = _mods(cond, w_ada, b_ada).reshape(DEPTH, 8, 6, D)

    tr = lambda a: jnp.swapaxes(a, -1, -2)
    cache_win_k, cache_win_v, cache_diff_k, cache_diff_v = map(tr, (cache_win_k, cache_win_v, cache_diff_k, cache_diff_v))
    cache_na_k, cache_na_v, cache_mla_kpe = map(tr, (cache_na_k, cache_na_v, cache_mla_kpe))

    xs = (x_prompt.reshape(T_CTX, D), x_sample.reshape(T_LAT, D))
    caches = ()
    for l in range(DEPTH):
        pa, pb, pc, pd = _in_proj(l, xs, mods, g_mix_pre, w_in_p, qn_p, kvn, wuq_p, wukv_p)
        o_ctx, *caches = _attn_ctx(l, win_sink[l], lam_tab[l], pa, pb, pc, pd, subln[l], caches)
        o_lat = [_lat_win(l, win_sink[l], pa, cache_win_k, cache_win_v, tabs64),
                 _lat_diff(l, lam_tab[l], pb, cache_diff_k, cache_diff_v, subln[l], tabs32),
                 _lat_mla(l, pc, cache_mla_ckv, cache_mla_kpe, wukv_p, tabs32),
                 _lat_na(l, pd, cache_na_k, cache_na_v, na_bias)]
        if l % 2 == 0:
            x1, h2 = _out_proj(l, o_ctx, o_lat, xs, mods, w_o_b, g_mix_post, g_ffn_pre, None)
            xs = (_ffn(l, l // 2, h2, x1, mods, g_ffn_post, ffn_wg, ffn_wu, ffn_wd),)
        else:
            x1, h2, route, counts = _out_proj(l, o_ctx, o_lat, xs, mods, w_o_b, g_mix_post, g_ffn_pre,
                                              router_p[l // 2])
            slots, item_e, item_blk, item_rows = _route_plan(route, counts)
            x_sorted = _dispatch(slots, h2)
            y_sorted = _moe(l // 2, item_e, item_blk, item_rows, x_sorted, moe_w_gate, moe_w_up, moe_w_down)
            xs = _post(l, slots, y_sorted, route, x1, mods, g_ffn_post, final=l == DEPTH - 1)
            xs = tuple(xs) if l == DEPTH - 1 else (xs,)

    y_prompt, y_sample = xs
    return (y_prompt.reshape(N_CTX_B, L_CTX, D), y_sample.reshape(N_LAT_B, N_LAT, D)) + tuple(caches)
```

```python
import functools
import math

import jax
import jax.numpy as jnp
from jax import lax
from jax.experimental import pallas as pl
from jax.experimental.pallas import tpu as pltpu

F32 = jnp.float32
BF16 = jnp.bfloat16

D = 1024
DEPTH = 4
N_CTX_B, L_CTX = 16, 256
N_LAT_B, N_LAT = 4, 1024
T_CTX = N_CTX_B * L_CTX
T_LAT = N_LAT_B * N_LAT
T = T_CTX + T_LAT
HD = 64
GRID_W = 64
WINDOW = 128
EPS = 1e-6
NEG_INF = -1e30
ROPE_THETA = 10000.0
MLA_Q_RANK = 192
D_FF = 2816
N_EXPERTS = 8
D_FF_EXPERT = 3584

PW = 2560
C_CKV, C_KPE, C_CQ, C_D = 1280, 1408, 1536, 1792
PC_W = 1152

TM = 512
EXPERT_CAP = T
ITEM_ROWS = 1024
SUB_ROWS = 512
TAIL_ROWS = 256
N_ITEMS = N_EXPERTS + (2 * T) // ITEM_ROWS
FF_CHUNK = 512
DISPATCH_TM = 1024
POST_TM = 1024
FFN_SLICES = (1024, 1024, 768)
VMEM_LIMIT = 56 * 1024 * 1024


def _cparams(sem):
    return pltpu.CompilerParams(dimension_semantics=sem, vmem_limit_bytes=VMEM_LIMIT)


def _rms(x, g):
    return x * lax.rsqrt(jnp.mean(x * x, axis=-1, keepdims=True) + EPS) * g


def _silu(a):
    return a / (1.0 + jnp.exp(-a))


def _dot(a, b):
    return jnp.dot(a, b, preferred_element_type=F32)


def _dot_t(a, b):
    return lax.dot_general(a, b, (((1,), (1,)), ((), ())), preferred_element_type=F32)


def _with_ones(v):
    return jnp.concatenate([v, jnp.ones(v.shape, v.dtype)], axis=1)


def _with_ones_t(vt):
    return jnp.concatenate([vt, jnp.ones(vt.shape, vt.dtype)], axis=0)


class _T:
    def __init__(self, a):
        self.a = a


def _scores(q, k):
    return _dot(q, k.a) if isinstance(k, _T) else _dot_t(q, k)


def _attend_scores(scores, vexts, extra_logit=None):
    mx = jnp.max(scores[0], axis=-1, keepdims=True)
    for s in scores[1:]:
        mx = jnp.maximum(mx, jnp.max(s, axis=-1, keepdims=True))
    if extra_logit is not None:
        mx = jnp.maximum(mx, extra_logit)
    r = None
    for s, v in zip(scores, vexts):
        p = jnp.exp(s - mx).astype(BF16)
        part = _dot_t(p, v.a) if isinstance(v, _T) else _dot(p, v)
        r = part if r is None else r + part
    den = r[:, 128:256]
    if extra_logit is not None:
        den = den + jnp.exp(extra_logit - mx)
    return r[:, 0:128] / den


def _attend(q, keys, vexts, extra_logit=None):
    return _attend_scores([_scores(q, k) for k in keys], vexts, extra_logit)


def _attend_diff(qp, keys, vexts, lam, low):
    n = qp.shape[0]
    q = jnp.concatenate([_keep_lanes(qp, (32 * i, 32)) for i in range(4)], axis=0)
    r = _attend(q, keys, vexts)
    return jnp.where(low, r[0:n] - lam * r[n:2 * n], r[2 * n:3 * n] - lam * r[3 * n:4 * n])


def _keep_lanes(x, *ranges):
    lane = lax.broadcasted_iota(jnp.int32, x.shape, 1)
    keep = None
    for lo, width in ranges:
        k = (lane >= lo) & (lane < lo + width)
        keep = k if keep is None else keep | k
    return jnp.where(keep, x, jnp.zeros_like(x))


def _low_half(n):
    return lax.broadcasted_iota(jnp.int32, (n, 128), 1) < HD


def _rms_halves(o, low, g):
    sq = o * o
    ms = jnp.where(low, jnp.sum(jnp.where(low, sq, 0.0), axis=-1, keepdims=True),
                   jnp.sum(jnp.where(low, 0.0, sq), axis=-1, keepdims=True)) / HD
    return o * lax.rsqrt(ms + EPS) * g


def _rope(x, c, sa, sb, shift):
    return x * c + pltpu.roll(x, 128 - shift, 1) * sa + pltpu.roll(x, shift, 1) * sb


def _mod_row(i):
    nct = T_CTX // TM
    per_b = N_LAT // TM
    return jnp.where(i >= nct, 1 + (i - nct) // per_b, 0)


def _mods_kernel(cond_ref, w_ref, b_ref, o_ref):
    a = _silu(cond_ref[...]).astype(BF16)
    o_ref[...] = _dot(a, w_ref[...].astype(BF16)) + b_ref[...]


def _mods(cond, w_ada, b_ada):
    tn = 1536
    n = 6 * D
    return pl.pallas_call(
        _mods_kernel,
        grid=(DEPTH, n // tn),
        in_specs=[pl.BlockSpec((8, D), lambda l, j: (0, 0)),
                  pl.BlockSpec((None, D, tn), lambda l, j: (l, 0, j)),
                  pl.BlockSpec((None, 1, tn), lambda l, j: (l, 0, j))],
        out_specs=pl.BlockSpec((None, 8, tn), lambda l, j: (l, 0, j)),
        out_shape=jax.ShapeDtypeStruct((DEPTH, 8, n), F32),
        compiler_params=_cparams(("arbitrary", "arbitrary")),
        name="mods",
    )(cond, w_ada, b_ada.reshape(DEPTH, 1, n))


def _x_specs(xs):
    if len(xs) == 1:
        return [pl.BlockSpec((TM, D), lambda i: (i, 0))]
    nct = T_CTX // TM
    return [pl.BlockSpec((TM, D), lambda i: (jnp.minimum(i, nct - 1), 0)),
            pl.BlockSpec((TM, D), lambda i: (jnp.maximum(i - nct, 0), 0))]


def _x_tile(x_refs):
    if len(x_refs) == 1:
        return x_refs[0][...]
    return jnp.where(pl.program_id(0) < T_CTX // TM, x_refs[0][...], x_refs[1][...])


def _in_proj_kernel(n_x, *refs):
    x_refs = refs[:n_x]
    mod_ref, g_ref, w_ref, qn_ref, kvn_ref, wuq_ref, wukv_ref, pa_ref, pb_ref, pc_ref, pd_ref = refs[n_x:]
    m = mod_ref[...]
    h = _rms(_x_tile(x_refs), g_ref[...]) * (1.0 + m[1:2]) + m[0:1]
    p = _dot_t(h.astype(BF16), w_ref[...])
    pa_ref[...] = p[:, 0:512].astype(BF16)
    pb_ref[...] = p[:, 512:1280].astype(BF16)
    pd_ref[...] = p[:, C_D:PW].astype(BF16)
    ckv_n = _rms(p[:, C_CKV:C_CKV + 128], kvn_ref[...])
    z = p[:, C_CQ:C_CQ + 256]
    cqn = z * lax.rsqrt(jnp.sum(z * z, axis=-1, keepdims=True) / MLA_Q_RANK + EPS) * qn_ref[...]
    qc = _dot(cqn.astype(BF16), wuq_ref[...])
    kv = _dot(ckv_n.astype(BF16), wukv_ref[...])
    pc_ref[:, 0:384] = qc.astype(BF16)
    pc_ref[:, 384:512] = p[:, C_KPE:C_KPE + 128].astype(BF16)
    pc_ref[:, 512:1024] = kv.astype(BF16)
    pc_ref[:, 1024:1152] = ckv_n.astype(BF16)


def _in_proj(l, xs, mods, g_pre, w_in_p, qn_p, kvn, wuq_p, wukv_p):
    row = lambda i: (i, 0)
    lyr2 = lambda i: (l, 0, 0)
    return pl.pallas_call(
        functools.partial(_in_proj_kernel, len(xs)),
        grid=(T // TM,),
        in_specs=_x_specs(xs) + [
                  pl.BlockSpec((None, None, 6, D), lambda i: (l, _mod_row(i), 0, 0)),
                  pl.BlockSpec((None, 1, D), lyr2),
                  pl.BlockSpec((None, PW, D), lyr2),
                  pl.BlockSpec((None, 1, 256), lyr2),
                  pl.BlockSpec((None, 1, 128), lyr2),
                  pl.BlockSpec((None, 256, 384), lyr2),
                  pl.BlockSpec((None, 128, 512), lyr2)],
        out_specs=[pl.BlockSpec((TM, 512), row), pl.BlockSpec((TM, 768), row),
                   pl.BlockSpec((TM, PC_W), row), pl.BlockSpec((TM, 768), row)],
        out_shape=[jax.ShapeDtypeStruct((T, 512), BF16), jax.ShapeDtypeStruct((T, 768), BF16),
                   jax.ShapeDtypeStruct((T, PC_W), BF16), jax.ShapeDtypeStruct((T, 768), BF16)],
        compiler_params=_cparams(("arbitrary",)),
        name="in_proj",
    )(*xs, mods, g_pre, w_in_p, qn_p, kvn, wuq_p, wukv_p)


def _attn_ctx_kernel(n_carried, sink_ref, lam_ref, pa, pb, pc, pd, subln_ref, *refs):
    o_ref, ka_o, va_o, kb_o, vb_o, ckv_o, kpe_o, kd_o, vd_o = refs[n_carried:]
    n = L_CTX
    low = _low_half(n)
    first = lax.broadcasted_iota(jnp.int32, (2 * n, 1), 0) < n
    ka, va = pa[:, 256:384], _with_ones(pa[:, 384:512])
    res = []
    for g in range(2):
        q = jnp.concatenate([_keep_lanes(pa[:, 0:128], (64 * g, 64)), _keep_lanes(pa[:, 128:256], (64 * g, 64))], axis=0)
        sink = jnp.where(first, sink_ref[2 * g], sink_ref[2 * g + 1])
        res.append(_attend(q, [ka], [va], sink))
    o_ref[:, 0:128] = jnp.where(low, res[0][0:n], res[1][0:n]).astype(BF16)
    o_ref[:, 128:256] = jnp.where(low, res[0][n:], res[1][n:]).astype(BF16)
    for g in range(2):
        ka_o[g] = pa[:, 256 + 64 * g:320 + 64 * g].astype(F32)
        va_o[g] = pa[:, 384 + 64 * g:448 + 64 * g].astype(F32)
    lam = lam_ref[0]
    out_scale = lam_ref[1]
    for j in range(2):
        o = _attend_diff(pb[:, 128 * j:128 * j + 128], [pb[:, 256 + 128 * j:384 + 128 * j]],
                         [_with_ones(pb[:, 512 + 128 * j:640 + 128 * j])], lam, low)
        o_ref[:, 256 + 128 * j:384 + 128 * j] = (_rms_halves(o, low, subln_ref[...]) * out_scale).astype(BF16)
    for h in range(4):
        kb_o[h] = pb[:, 256 + 64 * h:320 + 64 * h].astype(F32)
        vb_o[h] = pb[:, 512 + 64 * h:576 + 64 * h].astype(F32)
    kpe4 = pc[:, 384:512]
    for j in range(2):
        qn, kcat = pc[:, 128 * j:128 * j + 128], jnp.concatenate([pc[:, 512 + 128 * j:640 + 128 * j], kpe4], axis=1)
        qcat = jnp.concatenate([qn, pc[:, 256:384]], axis=1)
        q = jnp.concatenate([_keep_lanes(qcat, (64 * hh, 64), (128 + 32 * (2 * j + hh), 32)) for hh in range(2)], axis=0)
        r = _attend(q, [kcat], [_with_ones(pc[:, 768 + 128 * j:896 + 128 * j])])
        o_ref[:, 512 + 128 * j:640 + 128 * j] = jnp.where(low, r[0:n], r[n:]).astype(BF16)
    ckv_o[...] = pc[:, 1024:1152].astype(F32)
    kpe_o[...] = pc[:, 384:416].astype(F32)
    for j in range(2):
        qp = pd[:, 128 * j:128 * j + 128]
        q = jnp.concatenate([_keep_lanes(qp, (0, 64)), _keep_lanes(qp, (64, 64))], axis=0)
        r = _attend(q, [pd[:, 256 + 128 * j:384 + 128 * j]], [_with_ones(pd[:, 512 + 128 * j:640 + 128 * j])])
        o_ref[:, 768 + 128 * j:896 + 128 * j] = jnp.where(low, r[0:n], r[n:]).astype(BF16)
    for h in range(4):
        kd_o[h] = pd[:, 256 + 64 * h:320 + 64 * h].astype(F32)
        vd_o[h] = pd[:, 512 + 64 * h:576 + 64 * h].astype(F32)


def _attn_ctx(l, sink, lam, pa, pb, pc, pd, subln, carried):
    row = lambda b: (b, 0)
    smem = pl.BlockSpec(memory_space=pltpu.SMEM)
    hd4 = lambda n: pl.BlockSpec((None, None, n, L_CTX, HD), lambda b: (b, l, 0, 0, 0))
    hshape = lambda n: jax.ShapeDtypeStruct((N_CTX_B, DEPTH, n, L_CTX, HD), F32)
    n_in = 7
    return pl.pallas_call(
        functools.partial(_attn_ctx_kernel, len(carried)),
        grid=(N_CTX_B,),
        in_specs=[smem, smem,
                  pl.BlockSpec((L_CTX, 512), row), pl.BlockSpec((L_CTX, 768), row),
                  pl.BlockSpec((L_CTX, PC_W), row), pl.BlockSpec((L_CTX, 768), row),
                  pl.BlockSpec((1, 128), lambda b: (0, 0))] + [pl.BlockSpec(memory_space=pl.ANY)] * len(carried),
        out_specs=[pl.BlockSpec((L_CTX, D), row),
                   hd4(2), hd4(2), hd4(4), hd4(4),
                   pl.BlockSpec((None, None, L_CTX, 128), lambda b: (b, l, 0, 0)),
                   pl.BlockSpec((None, None, L_CTX, 32), lambda b: (b, l, 0, 0)),
                   hd4(4), hd4(4)],
        out_shape=[jax.ShapeDtypeStruct((T_CTX, D), BF16),
                   hshape(2), hshape(2), hshape(4), hshape(4),
                   jax.ShapeDtypeStruct((N_CTX_B, DEPTH, L_CTX, 128), F32),
                   jax.ShapeDtypeStruct((N_CTX_B, DEPTH, L_CTX, 32), F32),
                   hshape(4), hshape(4)],
        input_output_aliases={n_in + i: 1 + i for i in range(len(carried))},
        compiler_params=_cparams(("arbitrary",)),
        name="attn_ctx",
    )(sink, lam, pa, pb, pc, pd, subln, *carried)


_LAT_ROW = lambda b: (T_CTX // N_LAT + b, 0)


def _cache_spec(l, n):
    return pl.BlockSpec((None, None, n, HD, L_CTX), lambda b: (b, l, 0, 0, 0))


def _load_cache_pair(ck_ref, cv_ref, j, kcs, vcs):
    for hh in range(2):
        kcs[64 * hh:64 * hh + 64, :] = ck_ref[2 * j + hh].astype(BF16)
        vcs[64 * hh:64 * hh + 64, :] = cv_ref[2 * j + hh].astype(BF16)
    vcs[128:256, :] = jnp.ones((128, L_CTX), BF16)


def _tab_spec():
    return pl.BlockSpec((N_LAT, 128), lambda b: (0, 0))


def _lat_win_kernel(sink_ref, pa, ck_ref, cv_ref, c_ref, sa_ref, sb_ref, o_ref, qs, kpad, vpad, kcs, vcs):
    c, sa, sb = c_ref[...], sa_ref[...], sb_ref[...]
    for j in range(2):
        qs[:, 128 * j:128 * j + 128] = _rope(pa[:, 128 * j:128 * j + 128].astype(F32), c, sa, sb, 16).astype(BF16)
    zeros = jnp.zeros((WINDOW, 128), BF16)
    kpad[0:WINDOW, :] = zeros
    kpad[WINDOW:WINDOW + N_LAT, :] = _rope(pa[:, 256:384].astype(F32), c, sa, sb, 16).astype(BF16)
    kpad[WINDOW + N_LAT:, :] = zeros
    vpad[0:WINDOW, :] = zeros
    vpad[WINDOW:WINDOW + N_LAT, :] = pa[:, 384:512]
    vpad[WINDOW + N_LAT:, :] = zeros
    _load_cache_pair(ck_ref, cv_ref, 0, kcs, vcs)
    qi = lax.broadcasted_iota(jnp.int32, (2 * WINDOW, 3 * WINDOW), 0) % WINDOW
    kj = lax.broadcasted_iota(jnp.int32, (2 * WINDOW, 3 * WINDOW), 1)
    band = jnp.abs(kj - WINDOW - qi) <= WINDOW
    first_head = lax.broadcasted_iota(jnp.int32, (2 * WINDOW, 1), 0) < WINDOW
    low = _low_half(WINDOW)

    def body(nb, carry):
        r0 = pl.multiple_of(nb * WINDOW, WINDOW)
        kpos = r0 - WINDOW + kj
        valid = band & (kpos >= 0) & (kpos < N_LAT)
        kw = kpad[pl.ds(r0, 3 * WINDOW), :]
        vexts = [_with_ones(vpad[pl.ds(r0, 3 * WINDOW), :]), _T(vcs[...])]
        res = []
        for g in range(2):
            q = jnp.concatenate([_keep_lanes(qs[pl.ds(r0, WINDOW), 0:128], (64 * g, 64)),
                                 _keep_lanes(qs[pl.ds(r0, WINDOW), 128:256], (64 * g, 64))], axis=0)
            s_loc = jnp.where(valid, _dot_t(q, kw), NEG_INF)
            sink = jnp.where(first_head, sink_ref[2 * g], sink_ref[2 * g + 1])
            res.append(_attend_scores([s_loc, _dot(q, kcs[...])], vexts, sink))
        o_ref[pl.ds(r0, WINDOW), 0:128] = jnp.where(low, res[0][0:WINDOW], res[1][0:WINDOW]).astype(BF16)
        o_ref[pl.ds(r0, WINDOW), 128:256] = jnp.where(low, res[0][WINDOW:], res[1][WINDOW:]).astype(BF16)
        return carry

    lax.fori_loop(0, N_LAT // WINDOW, body, 0, unroll=2)


def _lat_win(l, sink, pa, ck, cv, tabs):
    return pl.pallas_call(
        _lat_win_kernel,
        grid=(N_LAT_B,),
        in_specs=[pl.BlockSpec(memory_space=pltpu.SMEM),
                  pl.BlockSpec((N_LAT, 512), _LAT_ROW),
                  _cache_spec(l, 2), _cache_spec(l, 2),
                  _tab_spec(), _tab_spec(), _tab_spec()],
        out_specs=pl.BlockSpec((N_LAT, 256), lambda b: (b, 0)),
        out_shape=jax.ShapeDtypeStruct((T_LAT, 256), BF16),
        scratch_shapes=[pltpu.VMEM((N_LAT, 256), BF16),
                        pltpu.VMEM((N_LAT + 2 * WINDOW, 128), BF16),
                        pltpu.VMEM((N_LAT + 2 * WINDOW, 128), BF16),
                        pltpu.VMEM((128, L_CTX), BF16),
                        pltpu.VMEM((256, L_CTX), BF16)],
        compiler_params=_cparams(("arbitrary",)),
        name="lat_win",
    )(sink, pa, ck, cv, *tabs)


QT = 256
DIFF_QT = 256
N_KEYS = N_LAT + L_CTX


def _lat_diff_kernel(lam_ref, pb, ck_ref, cv_ref, subln_ref, c_ref, sa_ref, sb_ref, o_ref, qs, klat, kcs, vcs):
    c, sa, sb = c_ref[...], sa_ref[...], sb_ref[...]
    for j in range(2):
        qs[:, 128 * j:128 * j + 128] = _rope(pb[:, 128 * j:128 * j + 128].astype(F32), c, sa, sb, 8).astype(BF16)
        klat[j] = _rope(pb[:, 256 + 128 * j:384 + 128 * j].astype(F32), c, sa, sb, 8).astype(BF16)
        _load_cache_pair(ck_ref, cv_ref, j, kcs.at[j], vcs.at[j])
    lam = lam_ref[0]
    out_scale = lam_ref[1]
    low = _low_half(DIFF_QT)
    n = DIFF_QT

    def body(t, carry):
        r0 = pl.multiple_of(t * n, n)
        for j in range(2):
            o = _attend_diff(qs[pl.ds(r0, n), 128 * j:128 * j + 128], [klat[j], _T(kcs[j])],
                             [_with_ones(pb[:, 512 + 128 * j:640 + 128 * j]), _T(vcs[j])], lam, low)
            o_ref[pl.ds(r0, n), 128 * j:128 * j + 128] = (_rms_halves(o, low, subln_ref[...]) * out_scale).astype(BF16)
        return carry

    lax.fori_loop(0, N_LAT // n, body, 0, unroll=2)


def _lat_diff(l, lam, pb, ck, cv, subln, tabs):
    return pl.pallas_call(
        _lat_diff_kernel,
        grid=(N_LAT_B,),
        in_specs=[pl.BlockSpec(memory_space=pltpu.SMEM),
                  pl.BlockSpec((N_LAT, 768), _LAT_ROW),
                  _cache_spec(l, 4), _cache_spec(l, 4),
                  pl.BlockSpec((1, 128), lambda b: (0, 0)),
                  _tab_spec(), _tab_spec(), _tab_spec()],
        out_specs=pl.BlockSpec((N_LAT, 256), lambda b: (b, 0)),
        out_shape=jax.ShapeDtypeStruct((T_LAT, 256), BF16),
        scratch_shapes=[pltpu.VMEM((N_LAT, 256), BF16),
                        pltpu.VMEM((2, N_LAT, 128), BF16),
                        pltpu.VMEM((2, 128, L_CTX), BF16),
                        pltpu.VMEM((2, 256, L_CTX), BF16)],
        compiler_params=_cparams(("arbitrary",)),
        name="lat_diff",
    )(lam, pb, ck, cv, subln, *tabs)


def _lat_mla_kernel(pc, ckv_ref, kpe_ref, wukv_ref, c_ref, sa_ref, sb_ref, o_ref, qcat, kcat, kvc, kpe4t):
    c, sa, sb = c_ref[...], sa_ref[...], sb_ref[...]
    qr = _rope(pc[:, 256:384].astype(F32), c, sa, sb, 8).astype(BF16)
    kpe_l = _rope(pc[:, 384:512].astype(F32), c, sa, sb, 8).astype(BF16)
    kvc[...] = _dot(ckv_ref[...].astype(BF16), wukv_ref[...]).astype(BF16)
    for i in range(4):
        kpe4t[32 * i:32 * i + 32, :] = kpe_ref[...].astype(BF16)
    for j in range(2):
        qcat[j, :, 0:128] = pc[:, 128 * j:128 * j + 128]
        qcat[j, :, 128:256] = qr
        kcat[j, :, 0:128] = pc[:, 512 + 128 * j:640 + 128 * j]
        kcat[j, :, 128:256] = kpe_l
    low = _low_half(QT)

    def body(t, carry):
        r0 = pl.multiple_of(t * QT, QT)
        for j in range(2):
            q2 = qcat[j, pl.ds(r0, QT), :]
            q = jnp.concatenate([_keep_lanes(q2, (64 * hh, 64), (128 + 32 * (2 * j + hh), 32)) for hh in range(2)],
                                axis=0)
            s_ctx = _dot_t(q[:, 0:128], kvc[:, 128 * j:128 * j + 128]) + _dot(q[:, 128:256], kpe4t[...])
            r = _attend_scores([_dot_t(q, kcat[j]), s_ctx],
                               [_with_ones(pc[:, 768 + 128 * j:896 + 128 * j]),
                                _with_ones(kvc[:, 256 + 128 * j:384 + 128 * j])])
            o_ref[pl.ds(r0, QT), 128 * j:128 * j + 128] = jnp.where(low, r[0:QT], r[QT:]).astype(BF16)
        return carry

    lax.fori_loop(0, N_LAT // QT, body, 0, unroll=2)


def _lat_mla(l, pc, ckv, kpe, wukv_p, tabs):
    return pl.pallas_call(
        _lat_mla_kernel,
        grid=(N_LAT_B,),
        in_specs=[pl.BlockSpec((N_LAT, PC_W), _LAT_ROW),
                  pl.BlockSpec((None, None, L_CTX, 128), lambda b: (b, l, 0, 0)),
                  pl.BlockSpec((None, None, 32, L_CTX), lambda b: (b, l, 0, 0)),
                  pl.BlockSpec((None, 128, 512), lambda b: (l, 0, 0)),
                  _tab_spec(), _tab_spec(), _tab_spec()],
        out_specs=pl.BlockSpec((N_LAT, 256), lambda b: (b, 0)),
        out_shape=jax.ShapeDtypeStruct((T_LAT, 256), BF16),
        scratch_shapes=[pltpu.VMEM((2, N_LAT, 256), BF16),
                        pltpu.VMEM((2, N_LAT, 256), BF16),
                        pltpu.VMEM((L_CTX, 512), BF16),
                        pltpu.VMEM((128, L_CTX), BF16)],
        compiler_params=_cparams(("arbitrary",)),
        name="lat_mla",
    )(pc, ckv, kpe, wukv_p, *tabs)


NA_ROWS = 8
NA_KEYS = NA_ROWS * GRID_W


def _lat_na_kernel(pd, ck_ref, cv_ref, bias_ref, o_ref, kcs, vcs):
    n_rows = N_LAT // GRID_W
    for j in range(2):
        _load_cache_pair(ck_ref, cv_ref, j, kcs.at[j], vcs.at[j])
    low = _low_half(GRID_W)

    def body(r, carry):
        rs = jnp.clip(r - NA_ROWS // 2, 0, n_rows - NA_ROWS)
        q0 = pl.multiple_of(r * GRID_W, GRID_W)
        k0 = pl.multiple_of(rs * GRID_W, GRID_W)
        first = NA_ROWS - 1 - (r - rs)
        for j in range(2):
            qp = pd[pl.ds(q0, GRID_W), 128 * j:128 * j + 128]
            q = jnp.concatenate([_keep_lanes(qp, (0, 64)), _keep_lanes(qp, (64, 64))], axis=0)
            bias = jnp.concatenate(
                [jnp.concatenate([bias_ref[2 * j + hh, first + 2 * m] for m in range(NA_ROWS // 2)], axis=1)
                 for hh in range(2)], axis=0)
            s_loc = _dot_t(q, pd[pl.ds(k0, NA_KEYS), 256 + 128 * j:384 + 128 * j]) + bias
            s_ctx = _dot(q, kcs[j])
            res = _attend_scores([s_loc, s_ctx],
                                 [_with_ones(pd[pl.ds(k0, NA_KEYS), 512 + 128 * j:640 + 128 * j]), _T(vcs[j])])
            o_ref[pl.ds(q0, GRID_W), 128 * j:128 * j + 128] = jnp.where(low, res[0:GRID_W], res[GRID_W:]).astype(BF16)
        return carry

    lax.fori_loop(0, n_rows, body, 0, unroll=2)


def _lat_na(l, pd, ck, cv, bias):
    return pl.pallas_call(
        _lat_na_kernel,
        grid=(N_LAT_B,),
        in_specs=[pl.BlockSpec((N_LAT, 768), _LAT_ROW),
                  _cache_spec(l, 4), _cache_spec(l, 4),
                  pl.BlockSpec((None, 4, 2 * NA_ROWS - 2, GRID_W, 2 * GRID_W), lambda b: (l, 0, 0, 0, 0))],
        out_specs=pl.BlockSpec((N_LAT, 256), lambda b: (b, 0)),
        out_shape=jax.ShapeDtypeStruct((T_LAT, 256), BF16),
        scratch_shapes=[pltpu.VMEM((2, 128, L_CTX), BF16), pltpu.VMEM((2, 256, L_CTX), BF16)],
        compiler_params=_cparams(("arbitrary",)),
        name="lat_na",
    )(pd, ck, cv, bias)


def _out_proj_kernel(with_router, n_x, *refs):
    x_refs = refs[:n_x]
    refs = refs[n_x:]
    if with_router:
        (oc_ref, oa_ref, ob_ref, om_ref, od_ref, mod_ref, wo_ref, gpost_ref, gffn_ref, r_ref,
         x1_ref, h2_ref, route_ref, cnt_ref) = refs
    else:
        (oc_ref, oa_ref, ob_ref, om_ref, od_ref, mod_ref, wo_ref, gpost_ref, gffn_ref,
         x1_ref, h2_ref) = refs
    i = pl.program_id(0)
    nct = T_CTX // TM

    if with_router:
        @pl.when(i == 0)
        def _():
            cnt_ref[...] = jnp.zeros(cnt_ref.shape, F32)

    def finish(y):
        m = mod_ref[...]
        x1 = _x_tile(x_refs) + m[2:3] * _rms(y, gpost_ref[...])
        x1_ref[...] = x1
        h2 = _rms(x1, gffn_ref[...]) * (1.0 + m[4:5]) + m[3:4]
        h2_ref[...] = h2.astype(h2_ref.dtype)
        if with_router:
            lg = _dot(h2.astype(BF16), r_ref[...])
            lane = lax.broadcasted_iota(jnp.int32, lg.shape, 1)
            lane_f = lane.astype(F32)
            lg = jnp.where(lane < N_EXPERTS, lg, -jnp.inf)
            m1 = jnp.max(lg, axis=-1, keepdims=True)
            i1 = jnp.min(jnp.where(lg == m1, lane_f, 128.0), axis=-1, keepdims=True)
            lg2 = jnp.where(lane_f == i1, -jnp.inf, lg)
            m2 = jnp.max(lg2, axis=-1, keepdims=True)
            i2 = jnp.min(jnp.where(lg2 == m2, lane_f, 128.0), axis=-1, keepdims=True)
            e = jnp.exp(m2 - m1)
            g1 = 1.0 / (1.0 + e)
            g2 = e / (1.0 + e)
            sel1 = lane_f == i1
            sel2 = lane_f == i2
            picked = jnp.where(sel1 | sel2, 1.0, 0.0)
            ri = lax.broadcasted_iota(jnp.int32, (TM, TM), 0)
            ci = lax.broadcasted_iota(jnp.int32, (TM, TM), 1)
            before = _dot(jnp.where(ci < ri, 1.0, 0.0).astype(BF16), picked.astype(BF16)) + cnt_ref[0:1, :]
            r1 = jnp.sum(jnp.where(sel1, before, 0.0), axis=-1, keepdims=True)
            r2 = jnp.sum(jnp.where(sel2, before, 0.0), axis=-1, keepdims=True)
            cnt_ref[...] = cnt_ref[...] + jnp.sum(picked, axis=0, keepdims=True)
            route = jnp.where(lane == 0, i1, jnp.where(lane == 1, i2, jnp.where(lane == 2, g1, 0.0)))
            route_ref[...] = jnp.where(lane == 3, g2, jnp.where(lane == 4, r1, jnp.where(lane == 5, r2, route)))

    @pl.when(i < nct)
    def _():
        finish(_dot(oc_ref[...], wo_ref[...]))

    @pl.when(i >= nct)
    def _():
        y = _dot(oa_ref[...], wo_ref[0:256, :])
        y = y + _dot(ob_ref[...], wo_ref[256:512, :])
        y = y + _dot(om_ref[...], wo_ref[512:768, :])
        y = y + _dot(od_ref[...], wo_ref[768:1024, :])
        finish(y)


def _out_proj(l, o_ctx, o_lat, xs, mods, w_o, g_post, g_ffn, router_p):
    nct = T_CTX // TM
    row = lambda i: (i, 0)
    lyr2 = lambda i: (l, 0, 0)
    lat = lambda i: (jnp.maximum(i - nct, 0), 0)
    with_router = router_p is not None
    in_specs = _x_specs(xs) + [
                pl.BlockSpec((TM, D), lambda i: (jnp.minimum(i, nct - 1), 0)),
                pl.BlockSpec((TM, 256), lat), pl.BlockSpec((TM, 256), lat),
                pl.BlockSpec((TM, 256), lat), pl.BlockSpec((TM, 256), lat),
                pl.BlockSpec((None, None, 6, D), lambda i: (l, _mod_row(i), 0, 0)),
                pl.BlockSpec((None, D, D), lyr2),
                pl.BlockSpec((None, 1, D), lyr2),
                pl.BlockSpec((None, 1, D), lyr2)]
    out_specs = [pl.BlockSpec((TM, D), row), pl.BlockSpec((TM, D), row)]
    out_shape = [jax.ShapeDtypeStruct((T, D), F32), jax.ShapeDtypeStruct((T, D), F32 if with_router else BF16)]
    args = [*xs, o_ctx, *o_lat, mods, w_o, g_post, g_ffn]
    if with_router:
        in_specs.append(pl.BlockSpec((D, 128), lambda i: (0, 0)))
        out_specs += [pl.BlockSpec((TM, 128), row), pl.BlockSpec((8, 128), lambda i: (0, 0))]
        out_shape += [jax.ShapeDtypeStruct((T, 128), F32), jax.ShapeDtypeStruct((8, 128), F32)]
        args.append(router_p)
    return pl.pallas_call(
        functools.partial(_out_proj_kernel, with_router, len(xs)),
        grid=(T // TM,),
        in_specs=in_specs, out_specs=out_specs, out_shape=out_shape,
        compiler_params=_cparams(("arbitrary",)),
        name="out_proj_router" if with_router else "out_proj",
    )(*args)


FFN_TM = 512


def _ffn_kernel(h_ref, x_ref, mod_ref, g_ref, wg_ref, wu_ref, wd_ref, o_ref):
    h = h_ref[...]
    acc = None
    lo = 0
    for width in FFN_SLICES:
        t = (_silu(_dot(h, wg_ref[:, lo:lo + width])) * _dot(h, wu_ref[:, lo:lo + width])).astype(BF16)
        part = _dot(t, wd_ref[lo:lo + width, :])
        acc = part if acc is None else acc + part
        lo += width
    m = mod_ref[...]
    o_ref[...] = x_ref[...] + m[5:6] * _rms(acc, g_ref[...])


def _ffn_mod_row(i):
    nct = T_CTX // FFN_TM
    return jnp.where(i >= nct, 1 + (i - nct) // (N_LAT // FFN_TM), 0)


def _post_mod_row(i):
    nct = T_CTX // POST_TM
    return jnp.where(i >= nct, 1 + (i - nct) // (N_LAT // POST_TM), 0)


def _ffn(l, i_dense, h2, x1, mods, g_post, wg, wu, wd):
    row = lambda i: (i, 0)
    resident = lambda shape: pl.BlockSpec((None,) + shape, lambda i: (i_dense, 0, 0), pipeline_mode=pl.Buffered(1))
    return pl.pallas_call(
        _ffn_kernel,
        grid=(T // FFN_TM,),
        in_specs=[pl.BlockSpec((FFN_TM, D), row),
                  pl.BlockSpec((FFN_TM, D), row),
                  pl.BlockSpec((None, None, 6, D), lambda i: (l, _ffn_mod_row(i), 0, 0)),
                  pl.BlockSpec((None, 1, D), lambda i: (l, 0, 0)),
                  resident((D, D_FF)), resident((D, D_FF)), resident((D_FF, D))],
        out_specs=pl.BlockSpec((FFN_TM, D), row),
        out_shape=jax.ShapeDtypeStruct((T, D), F32),
        compiler_params=_cparams(("arbitrary",)),
        name="ffn",
    )(h2, x1, mods, g_post, wg, wu, wd)


def _dispatch_kernel(slot_ref, h_ref, xs_hbm, sem):
    def row_copy(g, r, k):
        return pltpu.make_async_copy(h_ref.at[g, pl.ds(r, 1), :],
                                     xs_hbm.at[pl.ds(slot_ref[0, 16 * g + 2 * r + k], 1)], sem)

    def issue(g, carry):
        for r in range(8):
            row_copy(g, r, 0).start()
            row_copy(g, r, 1).start()
        return carry

    def drain(g, carry):
        for r in range(8):
            row_copy(g, r, 0).wait()
            row_copy(g, r, 1).wait()
        return carry

    lax.fori_loop(0, DISPATCH_TM // 8, issue, 0)
    lax.fori_loop(0, DISPATCH_TM // 8, drain, 0)


def _dispatch(slots, h2):
    return pl.pallas_call(
        _dispatch_kernel,
        grid=(T // DISPATCH_TM,),
        in_specs=[pl.BlockSpec((None, 1, 2 * DISPATCH_TM), lambda i: (i, 0, 0), memory_space=pltpu.SMEM),
                  pl.BlockSpec((DISPATCH_TM // 8, 8, D), lambda i: (i, 0, 0))],
        out_specs=pl.BlockSpec(memory_space=pl.ANY),
        out_shape=jax.ShapeDtypeStruct((N_EXPERTS * EXPERT_CAP, D), F32),
        scratch_shapes=[pltpu.SemaphoreType.DMA(())],
        compiler_params=_cparams(("arbitrary",)),
        name="moe_dispatch",
    )(slots.reshape(T // DISPATCH_TM, 1, 2 * DISPATCH_TM), h2.reshape(T // 8, 8, D))


def _moe_kernel(ie_ref, ib_ref, nr_ref, x_ref, wg_ref, wu_ref, wd_ref, o_ref, wgb, wub, wdb):
    i = pl.program_id(0)
    c = pl.program_id(1)
    n_rows = nr_ref[i]

    @pl.when(n_rows > 0)
    def _():
        @pl.when(c == 0)
        def _():
            o_ref[...] = jnp.zeros(o_ref.shape, F32)

        def step(r0, rows, wg, wu, wd):
            live = r0 + lax.broadcasted_iota(jnp.int32, (rows, 1), 0) < n_rows
            xs = jnp.where(live, x_ref[pl.ds(r0, rows), :], 0.0).astype(BF16)
            t = (_silu(_dot(xs, wg)) * _dot(xs, wu)).astype(BF16)
            o_ref[pl.ds(r0, rows), :] += _dot(t, wd)

        def first_step(rows, keep):
            w = wg_ref[...].astype(BF16), wu_ref[...].astype(BF16), wd_ref[...].astype(BF16)
            if keep:
                wgb[...], wub[...], wdb[...] = w
            step(0, rows, *w)

        @pl.when(n_rows == ITEM_ROWS)
        def _():
            first_step(ITEM_ROWS, keep=False)

        @pl.when(n_rows < ITEM_ROWS)
        def _():
            whole = lax.shift_right_logical(n_rows, SUB_ROWS.bit_length() - 1)
            left = jnp.bitwise_and(n_rows, SUB_ROWS - 1)
            n_full = whole + (left > TAIL_ROWS).astype(jnp.int32)
            tail = (left > 0) & (left <= TAIL_ROWS)

            @pl.when(n_full > 0)
            def _():
                first_step(SUB_ROWS, keep=True)

            @pl.when(n_full == 0)
            def _():
                first_step(TAIL_ROWS, keep=False)

            def body(s, carry):
                step(pl.multiple_of(s * SUB_ROWS, SUB_ROWS), SUB_ROWS, wgb[...], wub[...], wdb[...])
                return carry

            lax.fori_loop(1, n_full, body, 0)

            @pl.when(tail & (n_full > 0))
            def _():
                step(pl.multiple_of(whole * SUB_ROWS, SUB_ROWS), TAIL_ROWS, wgb[...], wub[...], wdb[...])


def _moe(i_moe, item_e, item_blk, item_rows, x_sorted, wg, wu, wd):
    nc = D_FF_EXPERT // FF_CHUNK
    chunk = lambda i, c, nr: jnp.where(nr[i] > 0, c, nc - 1)
    grid_spec = pltpu.PrefetchScalarGridSpec(
        num_scalar_prefetch=3,
        grid=(N_ITEMS, nc),
        in_specs=[pl.BlockSpec((ITEM_ROWS, D), lambda i, c, ie, ib, nr: (ib[i], 0)),
                  pl.BlockSpec((None, None, D, FF_CHUNK),
                               lambda i, c, ie, ib, nr: (i_moe, ie[i], 0, chunk(i, c, nr))),
                  pl.BlockSpec((None, None, D, FF_CHUNK),
                               lambda i, c, ie, ib, nr: (i_moe, ie[i], 0, chunk(i, c, nr))),
                  pl.BlockSpec((None, None, FF_CHUNK, D),
                               lambda i, c, ie, ib, nr: (i_moe, ie[i], chunk(i, c, nr), 0))],
        out_specs=pl.BlockSpec((ITEM_ROWS, D), lambda i, c, ie, ib, nr: (ib[i], 0)),
        scratch_shapes=[pltpu.VMEM((D, FF_CHUNK), BF16), pltpu.VMEM((D, FF_CHUNK), BF16),
                        pltpu.VMEM((FF_CHUNK, D), BF16)],
    )
    return pl.pallas_call(
        _moe_kernel,
        grid_spec=grid_spec,
        out_shape=jax.ShapeDtypeStruct((N_EXPERTS * EXPERT_CAP, D), F32),
        compiler_params=_cparams(("arbitrary", "arbitrary")),
        name="moe_experts",
    )(item_e, item_blk, item_rows, x_sorted, wg, wu, wd)


def _post_kernel(final, slot_ref, ys_hbm, route_ref, x_ref, mod_ref, g_ref, *refs):
    buf, sem = refs[-2:]

    def row_copy(g, r, k):
        return pltpu.make_async_copy(ys_hbm.at[pl.ds(slot_ref[0, 16 * g + 2 * r + k], 1)],
                                     buf.at[k, g, pl.ds(r, 1), :], sem)

    def issue(g, carry):
        for r in range(8):
            row_copy(g, r, 0).start()
            row_copy(g, r, 1).start()
        return carry

    def drain(g, carry):
        for r in range(8):
            row_copy(g, r, 0).wait()
            row_copy(g, r, 1).wait()
        return carry

    lax.fori_loop(0, POST_TM // 8, issue, 0)
    lax.fori_loop(0, POST_TM // 8, drain, 0)
    r = route_ref[...]
    y = r[:, 2:3] * buf[0].reshape(POST_TM, D) + r[:, 3:4] * buf[1].reshape(POST_TM, D)
    m = mod_ref[...]
    out = x_ref[...] + m[5:6] * _rms(y, g_ref[...])
    if not final:
        refs[0][...] = out
    else:
        i = pl.program_id(0)

        @pl.when(i < T_CTX // POST_TM)
        def _():
            refs[0][...] = out

        @pl.when(i >= T_CTX // POST_TM)
        def _():
            refs[1][...] = out


def _post(l, slots, y_sorted, route, x1, mods, g_post, final):
    row = lambda i: (i, 0)
    nct = T_CTX // POST_TM
    if final:
        out_specs = [pl.BlockSpec((POST_TM, D), lambda i: (jnp.minimum(i, nct - 1), 0)),
                     pl.BlockSpec((POST_TM, D), lambda i: (jnp.maximum(i - nct, 0), 0))]
        out_shape = [jax.ShapeDtypeStruct((T_CTX, D), F32), jax.ShapeDtypeStruct((T_LAT, D), F32)]
    else:
        out_specs = pl.BlockSpec((POST_TM, D), row)
        out_shape = jax.ShapeDtypeStruct((T, D), F32)
    return pl.pallas_call(
        functools.partial(_post_kernel, final),
        grid=(T // POST_TM,),
        in_specs=[pl.BlockSpec((None, 1, 2 * POST_TM), lambda i: (i, 0, 0), memory_space=pltpu.SMEM),
                  pl.BlockSpec(memory_space=pl.ANY),
                  pl.BlockSpec((POST_TM, 128), row), pl.BlockSpec((POST_TM, D), row),
                  pl.BlockSpec((None, None, 6, D), lambda i: (l, _post_mod_row(i), 0, 0)),
                  pl.BlockSpec((None, 1, D), lambda i: (l, 0, 0))],
        out_specs=out_specs,
        out_shape=out_shape,
        scratch_shapes=[pltpu.VMEM((2, POST_TM // 8, 8, D), F32), pltpu.SemaphoreType.DMA(())],
        compiler_params=_cparams(("arbitrary",)),
        name="moe_post_final" if final else "moe_post",
    )(slots.reshape(T // POST_TM, 1, 2 * POST_TM), y_sorted, route, x1, mods, g_post)


def _route_plan(route, counts):
    i32 = jnp.int32
    slots = (route[:, 0:2].astype(i32) * EXPERT_CAP + route[:, 4:6].astype(i32)).reshape(-1)
    n = counts[0, :N_EXPERTS].astype(i32)
    n_items = (n + ITEM_ROWS - 1) // ITEM_ROWS
    ends = jnp.cumsum(n_items)
    base = ends - n_items
    total = ends[-1]
    ii = jnp.arange(N_ITEMS, dtype=i32)
    ii_c = jnp.minimum(ii, total - 1)
    item_e = jnp.sum((ends[None, :] <= ii_c[:, None]).astype(i32), axis=1)
    j = ii_c - base[item_e]
    item_blk = item_e * (EXPERT_CAP // ITEM_ROWS) + j
    rows = jnp.where(ii < total, jnp.clip(n[item_e] - j * ITEM_ROWS, 0, ITEM_ROWS), 0)
    return slots, item_e.astype(i32), item_blk.astype(i32), rows.astype(i32)


def _rope_tables(d):
    half, quarter = d // 2, d // 4
    t = jnp.arange(N_LAT)
    pr, pc = t // GRID_W, t % GRID_W
    i = jnp.arange(128) % d
    j = (i % half) % quarter
    inv = ROPE_THETA ** (-(2 * j).astype(F32) / half)
    pos = jnp.where((i < half)[None, :], pr[:, None], pc[:, None]).astype(F32)
    ang = pos * inv[None, :]
    low = ((i % half) < quarter)[None, :]
    sin = jnp.sin(ang)
    return jnp.cos(ang), jnp.where(low, -sin, 0.0), jnp.where(low, 0.0, sin)


def _na_bias_table(rpb):
    q = jnp.arange(GRID_W)
    k = jnp.arange(GRID_W)
    rel_c = jnp.clip(k[None, :] - q[:, None], -15, 15) + 15
    cs = jnp.clip(q - 8, 0, GRID_W - 16)
    valid = (k[None, :] >= cs[:, None]) & (k[None, :] < cs[:, None] + 16)
    pick = (rel_c[:, :, None] == jnp.arange(31)[None, None, :]).astype(F32)
    band = jnp.einsum('qkc,lhrc->lhrqk', pick, rpb, precision=lax.Precision.HIGHEST)
    band = jnp.where(valid[None, None, None], band, NEG_INF)
    return jnp.concatenate([band[:, :, :-1], band[:, :, 1:]], axis=-1)


def _pad_in_proj(w_in):
    wt = jnp.swapaxes(w_in, 1, 2)
    z = lambda n: jnp.zeros((DEPTH, n, D), w_in.dtype)
    s64, s32 = HD ** -0.5, (HD // 2) ** -0.5
    qa = wt[:, 0:256] * s64
    kpe = wt[:, 1600:1632]
    return jnp.concatenate([qa[:, 0:64], qa[:, 128:192], qa[:, 64:128], qa[:, 192:256],
                            wt[:, 256:512],
                            wt[:, 512:768] * s32, wt[:, 768:1280],
                            wt[:, 1472:1600], kpe, kpe, kpe, kpe,
                            wt[:, 1280:1472], z(64),
                            wt[:, 1632:1888] * s64, wt[:, 1888:2400]], axis=1).astype(BF16)


def _split_heads_cols(w, first, second):
    lead = w.shape[:-1]
    w4 = w.reshape(*lead, 4, first + second)
    return jnp.concatenate([w4[..., :first].reshape(*lead, 4 * first),
                            w4[..., first:].reshape(*lead, 4 * second)], axis=-1)


def kernel(x_prompt, x_sample, cache_win_k, cache_win_v, cache_diff_k, cache_diff_v, cache_mla_ckv, cache_mla_kpe, cache_na_k, cache_na_v, c, c_ctx, w_ada, b_ada, norm_mix_pre, norm_mix_post, norm_ffn_pre, norm_ffn_post, w_in, w_o, win_sink, diff_lq1, diff_lk1, diff_lq2, diff_lk2, diff_subln, mla_q_norm, mla_kv_norm, mla_wuq, mla_wukv, na_rpb, ffn_w_gate, ffn_w_up, ffn_w_down, moe_router, moe_w_gate, moe_w_up, moe_w_down):
    w_in_p = _pad_in_proj(w_in)
    wuq_p = jnp.pad(_split_heads_cols(mla_wuq, 64, 32) * (HD + HD // 2) ** -0.5,
                    ((0, 0), (0, 256 - MLA_Q_RANK), (0, 0))).astype(BF16)
    wukv_p = _split_heads_cols(mla_wukv, 64, 64).astype(BF16)
    qn_p = jnp.pad(mla_q_norm, ((0, 0), (0, 256 - MLA_Q_RANK))).reshape(DEPTH, 1, 256)
    kvn = mla_kv_norm.reshape(DEPTH, 1, 128)
    w_o_b = jnp.concatenate([w_o[:, 0:64], w_o[:, 128:192], w_o[:, 64:128], w_o[:, 192:]], axis=1).astype(BF16)
    ffn_wg, ffn_wu, ffn_wd = ffn_w_gate.astype(BF16), ffn_w_up.astype(BF16), ffn_w_down.astype(BF16)
    router_p = jnp.pad(moe_router, ((0, 0), (0, 0), (0, 128 - N_EXPERTS))).astype(BF16)
    g_mix_pre = norm_mix_pre.reshape(DEPTH, 1, D)
    g_mix_post = norm_mix_post.reshape(DEPTH, 1, D)
    g_ffn_pre = norm_ffn_pre.reshape(DEPTH, 1, D)
    g_ffn_post = norm_ffn_post.reshape(DEPTH, 1, D)
    tabs64 = _rope_tables(64)
    tabs32 = _rope_tables(32)
    na_bias = _na_bias_table(na_rpb)
    lam_init = jnp.asarray([0.8 - 0.6 * math.exp(-0.3 * l) for l in range(DEPTH)], F32)
    lam = (jnp.exp(jnp.sum(diff_lq1 * diff_lk1, axis=-1)) - jnp.exp(jnp.sum(diff_lq2 * diff_lk2, axis=-1))
           + lam_init)
    lam_tab = jnp.stack([lam, 1.0 - lam_init], axis=-1)
    subln = jnp.tile(diff_subln, (1, 2)).reshape(DEPTH, 1, 2 * HD)

    cond = jnp.concatenate([c_ctx[None], c, jnp.zeros((3, D), F32)], axis=0)
    mods = _mods(cond, w_ada, b_ada).reshape(DEPTH, 8, 6, D)

    tr = lambda a: jnp.swapaxes(a, -1, -2)
    cache_win_k, cache_win_v, cache_diff_k, cache_diff_v = map(tr, (cache_win_k, cache_win_v, cache_diff_k, cache_diff_v))
    cache_na_k, cache_na_v, cache_mla_kpe = map(tr, (cache_na_k, cache_na_v, cache_mla_kpe))

    xs = (x_prompt.reshape(T_CTX, D), x_sample.reshape(T_LAT, D))
    caches = ()
    for l in range(DEPTH):
        pa, pb, pc, pd = _in_proj(l, xs, mods, g_mix_pre, w_in_p, qn_p, kvn, wuq_p, wukv_p)
        o_ctx, *caches = _attn_ctx(l, win_sink[l], lam_tab[l], pa, pb, pc, pd, subln[l], caches)
        o_lat = [_lat_win(l, win_sink[l], pa, cache_win_k, cache_win_v, tabs64),
                 _lat_diff(l, lam_tab[l], pb, cache_diff_k, cache_diff_v, subln[l], tabs32),
                 _lat_mla(l, pc, cache_mla_ckv, cache_mla_kpe, wukv_p, tabs32),
                 _lat_na(l, pd, cache_na_k, cache_na_v, na_bias)]
        if l % 2 == 0:
            x1, h2 = _out_proj(l, o_ctx, o_lat, xs, mods, w_o_b, g_mix_post, g_ffn_pre, None)
            xs = (_ffn(l, l // 2, h2, x1, mods, g_ffn_post, ffn_wg, ffn_wu, ffn_wd),)
        else:
            x1, h2, route, counts = _out_proj(l, o_ctx, o_lat, xs, mods, w_o_b, g_mix_post, g_ffn_pre,
                                              router_p[l // 2])
            slots, item_e, item_blk, item_rows = _route_plan(route, counts)
            x_sorted = _dispatch(slots, h2)
            y_sorted = _moe(l // 2, item_e, item_blk, item_rows, x_sorted, moe_w_gate, moe_w_up, moe_w_down)
            xs = _post(l, slots, y_sorted, route, x1, mods, g_ffn_post, final=l == DEPTH - 1)
            xs = tuple(xs) if l == DEPTH - 1 else (xs,)

    y_prompt, y_sample = xs
    return (y_prompt.reshape(N_CTX_B, L_CTX, D), y_sample.reshape(N_LAT_B, N_LAT, D)) + tuple(caches)
```

```python
import functools
import math

import jax
import jax.numpy as jnp
from jax import lax
from jax.experimental import pallas as pl
from jax.experimental.pallas import tpu as pltpu

F32 = jnp.float32
BF16 = jnp.bfloat16

D = 1024
DEPTH = 4
N_CTX_B, L_CTX = 16, 256
N_LAT_B, N_LAT = 4, 1024
T_CTX = N_CTX_B * L_CTX
T_LAT = N_LAT_B * N_LAT
T = T_CTX + T_LAT
HD = 64
GRID_W = 64
WINDOW = 128
EPS = 1e-6
NEG_INF = -1e30
ROPE_THETA = 10000.0
MLA_Q_RANK = 192
D_FF = 2816
N_EXPERTS = 8
D_FF_EXPERT = 3584

PW = 2560
C_CKV, C_KPE, C_CQ, C_D = 1280, 1408, 1536, 1792
PC_W = 1152

TM = 512
EXPERT_CAP = T
ITEM_ROWS = 1024
SUB_ROWS = 512
TAIL_ROWS = 256
N_ITEMS = N_EXPERTS + (2 * T) // ITEM_ROWS
FF_CHUNK = 512
DISPATCH_TM = 1024
POST_TM = 1024
FFN_SLICES = (1024, 1024, 768)
VMEM_LIMIT = 56 * 1024 * 1024


def _cparams(sem):
    return pltpu.CompilerParams(dimension_semantics=sem, vmem_limit_bytes=VMEM_LIMIT)


def _rms(x, g):
    return x * lax.rsqrt(jnp.mean(x * x, axis=-1, keepdims=True) + EPS) * g


def _silu(a):
    return a / (1.0 + jnp.exp(-a))


def _dot(a, b):
    return jnp.dot(a, b, preferred_element_type=F32)


def _dot_t(a, b):
    return lax.dot_general(a, b, (((1,), (1,)), ((), ())), preferred_element_type=F32)


def _with_ones(v):
    return jnp.concatenate([v, jnp.ones(v.shape, v.dtype)], axis=1)


def _with_ones_t(vt):
    return jnp.concatenate([vt, jnp.ones(vt.shape, vt.dtype)], axis=0)


class _T:
    def __init__(self, a):
        self.a = a


def _scores(q, k):
    return _dot(q, k.a) if isinstance(k, _T) else _dot_t(q, k)


def _attend_scores(scores, vexts, extra_logit=None):
    mx = jnp.max(scores[0], axis=-1, keepdims=True)
    for s in scores[1:]:
        mx = jnp.maximum(mx, jnp.max(s, axis=-1, keepdims=True))
    if extra_logit is not None:
        mx = jnp.maximum(mx, extra_logit)
    r = None
    for s, v in zip(scores, vexts):
        p = jnp.exp(s - mx).astype(BF16)
        part = _dot_t(p, v.a) if isinstance(v, _T) else _dot(p, v)
        r = part if r is None else r + part
    den = r[:, 128:256]
    if extra_logit is not None:
        den = den + jnp.exp(extra_logit - mx)
    return r[:, 0:128] / den


def _attend(q, keys, vexts, extra_logit=None):
    return _attend_scores([_scores(q, k) for k in keys], vexts, extra_logit)


def _attend_diff(qp, keys, vexts, lam, low):
    n = qp.shape[0]
    q = jnp.concatenate([_keep_lanes(qp, (32 * i, 32)) for i in range(4)], axis=0)
    r = _attend(q, keys, vexts)
    return jnp.where(low, r[0:n] - lam * r[n:2 * n], r[2 * n:3 * n] - lam * r[3 * n:4 * n])


def _keep_lanes(x, *ranges):
    lane = lax.broadcasted_iota(jnp.int32, x.shape, 1)
    keep = None
    for lo, width in ranges:
        k = (lane >= lo) & (lane < lo + width)
        keep = k if keep is None else keep | k
    return jnp.where(keep, x, jnp.zeros_like(x))


def _low_half(n):
    return lax.broadcasted_iota(jnp.int32, (n, 128), 1) < HD


def _rms_halves(o, low, g):
    sq = o * o
    ms = jnp.where(low, jnp.sum(jnp.where(low, sq, 0.0), axis=-1, keepdims=True),
                   jnp.sum(jnp.where(low, 0.0, sq), axis=-1, keepdims=True)) / HD
    return o * lax.rsqrt(ms + EPS) * g


def _rope(x, c, sa, sb, shift):
    return x * c + pltpu.roll(x, 128 - shift, 1) * sa + pltpu.roll(x, shift, 1) * sb


def _mod_row(i):
    nct = T_CTX // TM
    per_b = N_LAT // TM
    return jnp.where(i >= nct, 1 + (i - nct) // per_b, 0)


def _mods_kernel(cond_ref, w_ref, b_ref, o_ref):
    a = _silu(cond_ref[...]).astype(BF16)
    o_ref[...] = _dot(a, w_ref[...].astype(BF16)) + b_ref[...]


def _mods(cond, w_ada, b_ada):
    tn = 1536
    n = 6 * D
    return pl.pallas_call(
        _mods_kernel,
        grid=(DEPTH, n // tn),
        in_specs=[pl.BlockSpec((8, D), lambda l, j: (0, 0)),
                  pl.BlockSpec((None, D, tn), lambda l, j: (l, 0, j)),
                  pl.BlockSpec((None, 1, tn), lambda l, j: (l, 0, j))],
        out_specs=pl.BlockSpec((None, 8, tn), lambda l, j: (l, 0, j)),
        out_shape=jax.ShapeDtypeStruct((DEPTH, 8, n), F32),
        compiler_params=_cparams(("arbitrary", "arbitrary")),
        name="mods",
    )(cond, w_ada, b_ada.reshape(DEPTH, 1, n))


def _x_specs(xs):
    if len(xs) == 1:
        return [pl.BlockSpec((TM, D), lambda i: (i, 0))]
    nct = T_CTX // TM
    return [pl.BlockSpec((TM, D), lambda i: (jnp.minimum(i, nct - 1), 0)),
            pl.BlockSpec((TM, D), lambda i: (jnp.maximum(i - nct, 0), 0))]


def _x_tile(x_refs):
    if len(x_refs) == 1:
        return x_refs[0][...]
    return jnp.where(pl.program_id(0) < T_CTX // TM, x_refs[0][...], x_refs[1][...])


def _in_proj_kernel(n_x, *refs):
    x_refs = refs[:n_x]
    mod_ref, g_ref, w_ref, qn_ref, kvn_ref, wuq_ref, wukv_ref, pa_ref, pb_ref, pc_ref, pd_ref = refs[n_x:]
    m = mod_ref[...]
    h = _rms(_x_tile(x_refs), g_ref[...]) * (1.0 + m[1:2]) + m[0:1]
    p = _dot_t(h.astype(BF16), w_ref[...])
    pa_ref[...] = p[:, 0:512].astype(BF16)
    pb_ref[...] = p[:, 512:1280].astype(BF16)
    pd_ref[...] = p[:, C_D:PW].astype(BF16)
    ckv_n = _rms(p[:, C_CKV:C_CKV + 128], kvn_ref[...])
    z = p[:, C_CQ:C_CQ + 256]
    cqn = z * lax.rsqrt(jnp.sum(z * z, axis=-1, keepdims=True) / MLA_Q_RANK + EPS) * qn_ref[...]
    qc = _dot(cqn.astype(BF16), wuq_ref[...])
    kv = _dot(ckv_n.astype(BF16), wukv_ref[...])
    pc_ref[:, 0:384] = qc.astype(BF16)
    pc_ref[:, 384:512] = p[:, C_KPE:C_KPE + 128].astype(BF16)
    pc_ref[:, 512:1024] = kv.astype(BF16)
    pc_ref[:, 1024:1152] = ckv_n.astype(BF16)


def _in_proj(l, xs, mods, g_pre, w_in_p, qn_p, kvn, wuq_p, wukv_p):
    row = lambda i: (i, 0)
    lyr2 = lambda i: (l, 0, 0)
    return pl.pallas_call(
        functools.partial(_in_proj_kernel, len(xs)),
        grid=(T // TM,),
        in_specs=_x_specs(xs) + [
                  pl.BlockSpec((None, None, 6, D), lambda i: (l, _mod_row(i), 0, 0)),
                  pl.BlockSpec((None, 1, D), lyr2),
                  pl.BlockSpec((None, PW, D), lyr2),
                  pl.BlockSpec((None, 1, 256), lyr2),
                  pl.BlockSpec((None, 1, 128), lyr2),
                  pl.BlockSpec((None, 256, 384), lyr2),
                  pl.BlockSpec((None, 128, 512), lyr2)],
        out_specs=[pl.BlockSpec((TM, 512), row), pl.BlockSpec((TM, 768), row),
                   pl.BlockSpec((TM, PC_W), row), pl.BlockSpec((TM, 768), row)],
        out_shape=[jax.ShapeDtypeStruct((T, 512), BF16), jax.ShapeDtypeStruct((T, 768), BF16),
                   jax.ShapeDtypeStruct((T, PC_W), BF16), jax.ShapeDtypeStruct((T, 768), BF16)],
        compiler_params=_cparams(("arbitrary",)),
        name="in_proj",
    )(*xs, mods, g_pre, w_in_p, qn_p, kvn, wuq_p, wukv_p)


def _attn_ctx_kernel(n_carried, sink_ref, lam_ref, pa, pb, pc, pd, subln_ref, *refs):
    o_ref, ka_o, va_o, kb_o, vb_o, ckv_o, kpe_o, kd_o, vd_o = refs[n_carried:]
    n = L_CTX
    low = _low_half(n)
    first = lax.broadcasted_iota(jnp.int32, (2 * n, 1), 0) < n
    ka, va = pa[:, 256:384], _with_ones(pa[:, 384:512])
    res = []
    for g in range(2):
        q = jnp.concatenate([_keep_lanes(pa[:, 0:128], (64 * g, 64)), _keep_lanes(pa[:, 128:256], (64 * g, 64))], axis=0)
        sink = jnp.where(first, sink_ref[2 * g], sink_ref[2 * g + 1])
        res.append(_attend(q, [ka], [va], sink))
    o_ref[:, 0:128] = jnp.where(low, res[0][0:n], res[1][0:n]).astype(BF16)
    o_ref[:, 128:256] = jnp.where(low, res[0][n:], res[1][n:]).astype(BF16)
    for g in range(2):
        ka_o[g] = pa[:, 256 + 64 * g:320 + 64 * g].astype(F32)
        va_o[g] = pa[:, 384 + 64 * g:448 + 64 * g].astype(F32)
    lam = lam_ref[0]
    out_scale = lam_ref[1]
    for j in range(2):
        o = _attend_diff(pb[:, 128 * j:128 * j + 128], [pb[:, 256 + 128 * j:384 + 128 * j]],
                         [_with_ones(pb[:, 512 + 128 * j:640 + 128 * j])], lam, low)
        o_ref[:, 256 + 128 * j:384 + 128 * j] = (_rms_halves(o, low, subln_ref[...]) * out_scale).astype(BF16)
    for h in range(4):
        kb_o[h] = pb[:, 256 + 64 * h:320 + 64 * h].astype(F32)
        vb_o[h] = pb[:, 512 + 64 * h:576 + 64 * h].astype(F32)
    kpe4 = pc[:, 384:512]
    for j in range(2):
        qn, kcat = pc[:, 128 * j:128 * j + 128], jnp.concatenate([pc[:, 512 + 128 * j:640 + 128 * j], kpe4], axis=1)
        qcat = jnp.concatenate([qn, pc[:, 256:384]], axis=1)
        q = jnp.concatenate([_keep_lanes(qcat, (64 * hh, 64), (128 + 32 * (2 * j + hh), 32)) for hh in range(2)], axis=0)
        r = _attend(q, [kcat], [_with_ones(pc[:, 768 + 128 * j:896 + 128 * j])])
        o_ref[:, 512 + 128 * j:640 + 128 * j] = jnp.where(low, r[0:n], r[n:]).astype(BF16)
    ckv_o[...] = pc[:, 1024:1152].astype(F32)
    kpe_o[...] = pc[:, 384:416].astype(F32)
    for j in range(2):
        qp = pd[:, 128 * j:128 * j + 128]
        q = jnp.concatenate([_keep_lanes(qp, (0, 64)), _keep_lanes(qp, (64, 64))], axis=0)
        r = _attend(q, [pd[:, 256 + 128 * j:384 + 128 * j]], [_with_ones(pd[:, 512 + 128 * j:640 + 128 * j])])
        o_ref[:, 768 + 128 * j:896 + 128 * j] = jnp.where(low, r[0:n], r[n:]).astype(BF16)
    for h in range(4):
        kd_o[h] = pd[:, 256 + 64 * h:320 + 64 * h].astype(F32)
        vd_o[h] = pd[:, 512 + 64 * h:576 + 64 * h].astype(F32)


def _attn_ctx(l, sink, lam, pa, pb, pc, pd, subln, carried):
    row = lambda b: (b, 0)
    smem = pl.BlockSpec(memory_space=pltpu.SMEM)
    hd4 = lambda n: pl.BlockSpec((None, None, n, L_CTX, HD), lambda b: (b, l, 0, 0, 0))
    hshape = lambda n: jax.ShapeDtypeStruct((N_CTX_B, DEPTH, n, L_CTX, HD), F32)
    n_in = 7
    return pl.pallas_call(
        functools.partial(_attn_ctx_kernel, len(carried)),
        grid=(N_CTX_B,),
        in_specs=[smem, smem,
                  pl.BlockSpec((L_CTX, 512), row), pl.BlockSpec((L_CTX, 768), row),
                  pl.BlockSpec((L_CTX, PC_W), row), pl.BlockSpec((L_CTX, 768), row),
                  pl.BlockSpec((1, 128), lambda b: (0, 0))] + [pl.BlockSpec(memory_space=pl.ANY)] * len(carried),
        out_specs=[pl.BlockSpec((L_CTX, D), row),
                   hd4(2), hd4(2), hd4(4), hd4(4),
                   pl.BlockSpec((None, None, L_CTX, 128), lambda b: (b, l, 0, 0)),
                   pl.BlockSpec((None, None, L_CTX, 32), lambda b: (b, l, 0, 0)),
                   hd4(4), hd4(4)],
        out_shape=[jax.ShapeDtypeStruct((T_CTX, D), BF16),
                   hshape(2), hshape(2), hshape(4), hshape(4),
                   jax.ShapeDtypeStruct((N_CTX_B, DEPTH, L_CTX, 128), F32),
                   jax.ShapeDtypeStruct((N_CTX_B, DEPTH, L_CTX, 32), F32),
                   hshape(4), hshape(4)],
        input_output_aliases={n_in + i: 1 + i for i in range(len(carried))},
        compiler_params=_cparams(("arbitrary",)),
        name="attn_ctx",
    )(sink, lam, pa, pb, pc, pd, subln, *carried)


_LAT_ROW = lambda b: (T_CTX // N_LAT + b, 0)


def _cache_spec(l, n):
    return pl.BlockSpec((None, None, n, HD, L_CTX), lambda b: (b, l, 0, 0, 0))


def _load_cache_pair(ck_ref, cv_ref, j, kcs, vcs):
    for hh in range(2):
        kcs[64 * hh:64 * hh + 64, :] = ck_ref[2 * j + hh].astype(BF16)
        vcs[64 * hh:64 * hh + 64, :] = cv_ref[2 * j + hh].astype(BF16)
    vcs[128:256, :] = jnp.ones((128, L_CTX), BF16)


def _tab_spec():
    return pl.BlockSpec((N_LAT, 128), lambda b: (0, 0))


def _lat_win_kernel(sink_ref, pa, ck_ref, cv_ref, c_ref, sa_ref, sb_ref, o_ref, qs, kpad, vpad, kcs, vcs):
    c, sa, sb = c_ref[...], sa_ref[...], sb_ref[...]
    for j in range(2):
        qs[:, 128 * j:128 * j + 128] = _rope(pa[:, 128 * j:128 * j + 128].astype(F32), c, sa, sb, 16).astype(BF16)
    zeros = jnp.zeros((WINDOW, 128), BF16)
    kpad[0:WINDOW, :] = zeros
    kpad[WINDOW:WINDOW + N_LAT, :] = _rope(pa[:, 256:384].astype(F32), c, sa, sb, 16).astype(BF16)
    kpad[WINDOW + N_LAT:, :] = zeros
    vpad[0:WINDOW, :] = zeros
    vpad[WINDOW:WINDOW + N_LAT, :] = pa[:, 384:512]
    vpad[WINDOW + N_LAT:, :] = zeros
    _load_cache_pair(ck_ref, cv_ref, 0, kcs, vcs)
    qi = lax.broadcasted_iota(jnp.int32, (2 * WINDOW, 3 * WINDOW), 0) % WINDOW
    kj = lax.broadcasted_iota(jnp.int32, (2 * WINDOW, 3 * WINDOW), 1)
    band = jnp.abs(kj - WINDOW - qi) <= WINDOW
    first_head = lax.broadcasted_iota(jnp.int32, (2 * WINDOW, 1), 0) < WINDOW
    low = _low_half(WINDOW)

    def body(nb, carry):
        r0 = pl.multiple_of(nb * WINDOW, WINDOW)
        kpos = r0 - WINDOW + kj
        valid = band & (kpos >= 0) & (kpos < N_LAT)
        kw = kpad[pl.ds(r0, 3 * WINDOW), :]
        vexts = [_with_ones(vpad[pl.ds(r0, 3 * WINDOW), :]), _T(vcs[...])]
        res = []
        for g in range(2):
            q = jnp.concatenate([_keep_lanes(qs[pl.ds(r0, WINDOW), 0:128], (64 * g, 64)),
                                 _keep_lanes(qs[pl.ds(r0, WINDOW), 128:256], (64 * g, 64))], axis=0)
            s_loc = jnp.where(valid, _dot_t(q, kw), NEG_INF)
            sink = jnp.where(first_head, sink_ref[2 * g], sink_ref[2 * g + 1])
            res.append(_attend_scores([s_loc, _dot(q, kcs[...])], vexts, sink))
        o_ref[pl.ds(r0, WINDOW), 0:128] = jnp.where(low, res[0][0:WINDOW], res[1][0:WINDOW]).astype(BF16)
        o_ref[pl.ds(r0, WINDOW), 128:256] = jnp.where(low, res[0][WINDOW:], res[1][WINDOW:]).astype(BF16)
        return carry

    lax.fori_loop(0, N_LAT // WINDOW, body, 0, unroll=4)


def _lat_win(l, sink, pa, ck, cv, tabs):
    return pl.pallas_call(
        _lat_win_kernel,
        grid=(N_LAT_B,),
        in_specs=[pl.BlockSpec(memory_space=pltpu.SMEM),
                  pl.BlockSpec((N_LAT, 512), _LAT_ROW),
                  _cache_spec(l, 2), _cache_spec(l, 2),
                  _tab_spec(), _tab_spec(), _tab_spec()],
        out_specs=pl.BlockSpec((N_LAT, 256), lambda b: (b, 0)),
        out_shape=jax.ShapeDtypeStruct((T_LAT, 256), BF16),
        scratch_shapes=[pltpu.VMEM((N_LAT, 256), BF16),
                        pltpu.VMEM((N_LAT + 2 * WINDOW, 128), BF16),
                        pltpu.VMEM((N_LAT + 2 * WINDOW, 128), BF16),
                        pltpu.VMEM((128, L_CTX), BF16),
                        pltpu.VMEM((256, L_CTX), BF16)],
        compiler_params=_cparams(("arbitrary",)),
        name="lat_win",
    )(sink, pa, ck, cv, *tabs)


QT = 256
DIFF_QT = 256
N_KEYS = N_LAT + L_CTX


def _lat_diff_kernel(lam_ref, pb, ck_ref, cv_ref, subln_ref, c_ref, sa_ref, sb_ref, o_ref, qs, klat, kcs, vcs):
    c, sa, sb = c_ref[...], sa_ref[...], sb_ref[...]
    for j in range(2):
        qs[:, 128 * j:128 * j + 128] = _rope(pb[:, 128 * j:128 * j + 128].astype(F32), c, sa, sb, 8).astype(BF16)
        klat[j] = _rope(pb[:, 256 + 128 * j:384 + 128 * j].astype(F32), c, sa, sb, 8).astype(BF16)
        _load_cache_pair(ck_ref, cv_ref, j, kcs.at[j], vcs.at[j])
    lam = lam_ref[0]
    out_scale = lam_ref[1]
    low = _low_half(DIFF_QT)
    n = DIFF_QT

    def body(t, carry):
        r0 = pl.multiple_of(t * n, n)
        for j in range(2):
            o = _attend_diff(qs[pl.ds(r0, n), 128 * j:128 * j + 128], [klat[j], _T(kcs[j])],
                             [_with_ones(pb[:, 512 + 128 * j:640 + 128 * j]), _T(vcs[j])], lam, low)
            o_ref[pl.ds(r0, n), 128 * j:128 * j + 128] = (_rms_halves(o, low, subln_ref[...]) * out_scale).astype(BF16)
        return carry

    lax.fori_loop(0, N_LAT // n, body, 0, unroll=2)


def _lat_diff(l, lam, pb, ck, cv, subln, tabs):
    return pl.pallas_call(
        _lat_diff_kernel,
        grid=(N_LAT_B,),
        in_specs=[pl.BlockSpec(memory_space=pltpu.SMEM),
                  pl.BlockSpec((N_LAT, 768), _LAT_ROW),
                  _cache_spec(l, 4), _cache_spec(l, 4),
                  pl.BlockSpec((1, 128), lambda b: (0, 0)),
                  _tab_spec(), _tab_spec(), _tab_spec()],
        out_specs=pl.BlockSpec((N_LAT, 256), lambda b: (b, 0)),
        out_shape=jax.ShapeDtypeStruct((T_LAT, 256), BF16),
        scratch_shapes=[pltpu.VMEM((N_LAT, 256), BF16),
                        pltpu.VMEM((2, N_LAT, 128), BF16),
                        pltpu.VMEM((2, 128, L_CTX), BF16),
                        pltpu.VMEM((2, 256, L_CTX), BF16)],
        compiler_params=_cparams(("arbitrary",)),
        name="lat_diff",
    )(lam, pb, ck, cv, subln, *tabs)


def _lat_mla_kernel(pc, ckv_ref, kpe_ref, wukv_ref, c_ref, sa_ref, sb_ref, o_ref, qcat, kcat, kvc, kpe4t):
    c, sa, sb = c_ref[...], sa_ref[...], sb_ref[...]
    qr = _rope(pc[:, 256:384].astype(F32), c, sa, sb, 8).astype(BF16)
    kpe_l = _rope(pc[:, 384:512].astype(F32), c, sa, sb, 8).astype(BF16)
    kvc[...] = _dot(ckv_ref[...].astype(BF16), wukv_ref[...]).astype(BF16)
    for i in range(4):
        kpe4t[32 * i:32 * i + 32, :] = kpe_ref[...].astype(BF16)
    for j in range(2):
        qcat[j, :, 0:128] = pc[:, 128 * j:128 * j + 128]
        qcat[j, :, 128:256] = qr
        kcat[j, :, 0:128] = pc[:, 512 + 128 * j:640 + 128 * j]
        kcat[j, :, 128:256] = kpe_l
    low = _low_half(QT)

    def body(t, carry):
        r0 = pl.multiple_of(t * QT, QT)
        for j in range(2):
            q2 = qcat[j, pl.ds(r0, QT), :]
            q = jnp.concatenate([_keep_lanes(q2, (64 * hh, 64), (128 + 32 * (2 * j + hh), 32)) for hh in range(2)],
                                axis=0)
            s_ctx = _dot_t(q[:, 0:128], kvc[:, 128 * j:128 * j + 128]) + _dot(q[:, 128:256], kpe4t[...])
            r = _attend_scores([_dot_t(q, kcat[j]), s_ctx],
                               [_with_ones(pc[:, 768 + 128 * j:896 + 128 * j]),
                                _with_ones(kvc[:, 256 + 128 * j:384 + 128 * j])])
            o_ref[pl.ds(r0, QT), 128 * j:128 * j + 128] = jnp.where(low, r[0:QT], r[QT:]).astype(BF16)
        return carry

    lax.fori_loop(0, N_LAT // QT, body, 0, unroll=2)


def _lat_mla(l, pc, ckv, kpe, wukv_p, tabs):
    return pl.pallas_call(
        _lat_mla_kernel,
        grid=(N_LAT_B,),
        in_specs=[pl.BlockSpec((N_LAT, PC_W), _LAT_ROW),
                  pl.BlockSpec((None, None, L_CTX, 128), lambda b: (b, l, 0, 0)),
                  pl.BlockSpec((None, None, 32, L_CTX), lambda b: (b, l, 0, 0)),
                  pl.BlockSpec((None, 128, 512), lambda b: (l, 0, 0)),
                  _tab_spec(), _tab_spec(), _tab_spec()],
        out_specs=pl.BlockSpec((N_LAT, 256), lambda b: (b, 0)),
        out_shape=jax.ShapeDtypeStruct((T_LAT, 256), BF16),
        scratch_shapes=[pltpu.VMEM((2, N_LAT, 256), BF16),
                        pltpu.VMEM((2, N_LAT, 256), BF16),
                        pltpu.VMEM((L_CTX, 512), BF16),
                        pltpu.VMEM((128, L_CTX), BF16)],
        compiler_params=_cparams(("arbitrary",)),
        name="lat_mla",
    )(pc, ckv, kpe, wukv_p, *tabs)


NA_ROWS = 8
NA_KEYS = NA_ROWS * GRID_W


def _lat_na_kernel(pd, ck_ref, cv_ref, bias_ref, o_ref, kcs, vcs):
    n_rows = N_LAT // GRID_W
    for j in range(2):
        _load_cache_pair(ck_ref, cv_ref, j, kcs.at[j], vcs.at[j])
    low = _low_half(GRID_W)

    def body(r, carry):
        rs = jnp.clip(r - NA_ROWS // 2, 0, n_rows - NA_ROWS)
        q0 = pl.multiple_of(r * GRID_W, GRID_W)
        k0 = pl.multiple_of(rs * GRID_W, GRID_W)
        first = NA_ROWS - 1 - (r - rs)
        for j in range(2):
            qp = pd[pl.ds(q0, GRID_W), 128 * j:128 * j + 128]
            q = jnp.concatenate([_keep_lanes(qp, (0, 64)), _keep_lanes(qp, (64, 64))], axis=0)
            bias = jnp.concatenate(
                [jnp.concatenate([bias_ref[2 * j + hh, first + 2 * m] for m in range(NA_ROWS // 2)], axis=1)
                 for hh in range(2)], axis=0)
            s_loc = _dot_t(q, pd[pl.ds(k0, NA_KEYS), 256 + 128 * j:384 + 128 * j]) + bias
            s_ctx = _dot(q, kcs[j])
            res = _attend_scores([s_loc, s_ctx],
                                 [_with_ones(pd[pl.ds(k0, NA_KEYS), 512 + 128 * j:640 + 128 * j]), _T(vcs[j])])
            o_ref[pl.ds(q0, GRID_W), 128 * j:128 * j + 128] = jnp.where(low, res[0:GRID_W], res[GRID_W:]).astype(BF16)
        return carry

    lax.fori_loop(0, n_rows, body, 0, unroll=4)


def _lat_na(l, pd, ck, cv, bias):
    return pl.pallas_call(
        _lat_na_kernel,
        grid=(N_LAT_B,),
        in_specs=[pl.BlockSpec((N_LAT, 768), _LAT_ROW),
                  _cache_spec(l, 4), _cache_spec(l, 4),
                  pl.BlockSpec((None, 4, 2 * NA_ROWS - 2, GRID_W, 2 * GRID_W), lambda b: (l, 0, 0, 0, 0))],
        out_specs=pl.BlockSpec((N_LAT, 256), lambda b: (b, 0)),
        out_shape=jax.ShapeDtypeStruct((T_LAT, 256), BF16),
        scratch_shapes=[pltpu.VMEM((2, 128, L_CTX), BF16), pltpu.VMEM((2, 256, L_CTX), BF16)],
        compiler_params=_cparams(("arbitrary",)),
        name="lat_na",
    )(pd, ck, cv, bias)


def _out_proj_kernel(with_router, n_x, *refs):
    x_refs = refs[:n_x]
    refs = refs[n_x:]
    if with_router:
        (oc_ref, oa_ref, ob_ref, om_ref, od_ref, mod_ref, wo_ref, gpost_ref, gffn_ref, r_ref,
         x1_ref, h2_ref, route_ref, cnt_ref) = refs
    else:
        (oc_ref, oa_ref, ob_ref, om_ref, od_ref, mod_ref, wo_ref, gpost_ref, gffn_ref,
         x1_ref, h2_ref) = refs
    i = pl.program_id(0)
    nct = T_CTX // TM

    if with_router:
        @pl.when(i == 0)
        def _():
            cnt_ref[...] = jnp.zeros(cnt_ref.shape, F32)

    def finish(y):
        m = mod_ref[...]
        x1 = _x_tile(x_refs) + m[2:3] * _rms(y, gpost_ref[...])
        x1_ref[...] = x1
        h2 = _rms(x1, gffn_ref[...]) * (1.0 + m[4:5]) + m[3:4]
        h2_ref[...] = h2.astype(h2_ref.dtype)
        if with_router:
            lg = _dot(h2.astype(BF16), r_ref[...])
            lane = lax.broadcasted_iota(jnp.int32, lg.shape, 1)
            lane_f = lane.astype(F32)
            lg = jnp.where(lane < N_EXPERTS, lg, -jnp.inf)
            m1 = jnp.max(lg, axis=-1, keepdims=True)
            i1 = jnp.min(jnp.where(lg == m1, lane_f, 128.0), axis=-1, keepdims=True)
            lg2 = jnp.where(lane_f == i1, -jnp.inf, lg)
            m2 = jnp.max(lg2, axis=-1, keepdims=True)
            i2 = jnp.min(jnp.where(lg2 == m2, lane_f, 128.0), axis=-1, keepdims=True)
            e = jnp.exp(m2 - m1)
            g1 = 1.0 / (1.0 + e)
            g2 = e / (1.0 + e)
            sel1 = lane_f == i1
            sel2 = lane_f == i2
            picked = jnp.where(sel1 | sel2, 1.0, 0.0)
            ri = lax.broadcasted_iota(jnp.int32, (TM, TM), 0)
            ci = lax.broadcasted_iota(jnp.int32, (TM, TM), 1)
            before = _dot(jnp.where(ci < ri, 1.0, 0.0).astype(BF16), picked.astype(BF16)) + cnt_ref[0:1, :]
            r1 = jnp.sum(jnp.where(sel1, before, 0.0), axis=-1, keepdims=True)
            r2 = jnp.sum(jnp.where(sel2, before, 0.0), axis=-1, keepdims=True)
            cnt_ref[...] = cnt_ref[...] + jnp.sum(picked, axis=0, keepdims=True)
            route = jnp.where(lane == 0, i1, jnp.where(lane == 1, i2, jnp.where(lane == 2, g1, 0.0)))
            route_ref[...] = jnp.where(lane == 3, g2, jnp.where(lane == 4, r1, jnp.where(lane == 5, r2, route)))

    @pl.when(i < nct)
    def _():
        finish(_dot(oc_ref[...], wo_ref[...]))

    @pl.when(i >= nct)
    def _():
        y = _dot(oa_ref[...], wo_ref[0:256, :])
        y = y + _dot(ob_ref[...], wo_ref[256:512, :])
        y = y + _dot(om_ref[...], wo_ref[512:768, :])
        y = y + _dot(od_ref[...], wo_ref[768:1024, :])
        finish(y)


def _out_proj(l, o_ctx, o_lat, xs, mods, w_o, g_post, g_ffn, router_p):
    nct = T_CTX // TM
    row = lambda i: (i, 0)
    lyr2 = lambda i: (l, 0, 0)
    lat = lambda i: (jnp.maximum(i - nct, 0), 0)
    with_router = router_p is not None
    in_specs = _x_specs(xs) + [
                pl.BlockSpec((TM, D), lambda i: (jnp.minimum(i, nct - 1), 0)),
                pl.BlockSpec((TM, 256), lat), pl.BlockSpec((TM, 256), lat),
                pl.BlockSpec((TM, 256), lat), pl.BlockSpec((TM, 256), lat),
                pl.BlockSpec((None, None, 6, D), lambda i: (l, _mod_row(i), 0, 0)),
                pl.BlockSpec((None, D, D), lyr2),
                pl.BlockSpec((None, 1, D), lyr2),
                pl.BlockSpec((None, 1, D), lyr2)]
    out_specs = [pl.BlockSpec((TM, D), row), pl.BlockSpec((TM, D), row)]
    out_shape = [jax.ShapeDtypeStruct((T, D), F32), jax.ShapeDtypeStruct((T, D), F32 if with_router else BF16)]
    args = [*xs, o_ctx, *o_lat, mods, w_o, g_post, g_ffn]
    if with_router:
        in_specs.append(pl.BlockSpec((D, 128), lambda i: (0, 0)))
        out_specs += [pl.BlockSpec((TM, 128), row), pl.BlockSpec((8, 128), lambda i: (0, 0))]
        out_shape += [jax.ShapeDtypeStruct((T, 128), F32), jax.ShapeDtypeStruct((8, 128), F32)]
        args.append(router_p)
    return pl.pallas_call(
        functools.partial(_out_proj_kernel, with_router, len(xs)),
        grid=(T // TM,),
        in_specs=in_specs, out_specs=out_specs, out_shape=out_shape,
        compiler_params=_cparams(("arbitrary",)),
        name="out_proj_router" if with_router else "out_proj",
    )(*args)


FFN_TM = 512


def _ffn_kernel(h_ref, x_ref, mod_ref, g_ref, wg_ref, wu_ref, wd_ref, o_ref):
    h = h_ref[...]
    acc = None
    lo = 0
    for width in FFN_SLICES:
        t = (_silu(_dot(h, wg_ref[:, lo:lo + width])) * _dot(h, wu_ref[:, lo:lo + width])).astype(BF16)
        part = _dot(t, wd_ref[lo:lo + width, :])
        acc = part if acc is None else acc + part
        lo += width
    m = mod_ref[...]
    o_ref[...] = x_ref[...] + m[5:6] * _rms(acc, g_ref[...])


def _ffn_mod_row(i):
    nct = T_CTX // FFN_TM
    return jnp.where(i >= nct, 1 + (i - nct) // (N_LAT // FFN_TM), 0)


def _post_mod_row(i):
    nct = T_CTX // POST_TM
    return jnp.where(i >= nct, 1 + (i - nct) // (N_LAT // POST_TM), 0)


def _ffn(l, i_dense, h2, x1, mods, g_post, wg, wu, wd):
    row = lambda i: (i, 0)
    resident = lambda shape: pl.BlockSpec((None,) + shape, lambda i: (i_dense, 0, 0), pipeline_mode=pl.Buffered(1))
    return pl.pallas_call(
        _ffn_kernel,
        grid=(T // FFN_TM,),
        in_specs=[pl.BlockSpec((FFN_TM, D), row),
                  pl.BlockSpec((FFN_TM, D), row),
                  pl.BlockSpec((None, None, 6, D), lambda i: (l, _ffn_mod_row(i), 0, 0)),
                  pl.BlockSpec((None, 1, D), lambda i: (l, 0, 0)),
                  resident((D, D_FF)), resident((D, D_FF)), resident((D_FF, D))],
        out_specs=pl.BlockSpec((FFN_TM, D), row),
        out_shape=jax.ShapeDtypeStruct((T, D), F32),
        compiler_params=_cparams(("arbitrary",)),
        name="ffn",
    )(h2, x1, mods, g_post, wg, wu, wd)


def _dispatch_kernel(slot_ref, h_ref, xs_hbm, sem):
    def row_copy(g, r, k):
        return pltpu.make_async_copy(h_ref.at[g, pl.ds(r, 1), :],
                                     xs_hbm.at[pl.ds(slot_ref[0, 16 * g + 2 * r + k], 1)], sem)

    def issue(g, carry):
        for r in range(8):
            row_copy(g, r, 0).start()
            row_copy(g, r, 1).start()
        return carry

    def drain(g, carry):
        for r in range(8):
            row_copy(g, r, 0).wait()
            row_copy(g, r, 1).wait()
        return carry

    lax.fori_loop(0, DISPATCH_TM // 8, issue, 0)
    lax.fori_loop(0, DISPATCH_TM // 8, drain, 0)


def _dispatch(slots, h2):
    return pl.pallas_call(
        _dispatch_kernel,
        grid=(T // DISPATCH_TM,),
        in_specs=[pl.BlockSpec((None, 1, 2 * DISPATCH_TM), lambda i: (i, 0, 0), memory_space=pltpu.SMEM),
                  pl.BlockSpec((DISPATCH_TM // 8, 8, D), lambda i: (i, 0, 0))],
        out_specs=pl.BlockSpec(memory_space=pl.ANY),
        out_shape=jax.ShapeDtypeStruct((N_EXPERTS * EXPERT_CAP, D), F32),
        scratch_shapes=[pltpu.SemaphoreType.DMA(())],
        compiler_params=_cparams(("arbitrary",)),
        name="moe_dispatch",
    )(slots.reshape(T // DISPATCH_TM, 1, 2 * DISPATCH_TM), h2.reshape(T // 8, 8, D))


def _moe_kernel(ie_ref, ib_ref, nr_ref, x_ref, wg_ref, wu_ref, wd_ref, o_ref, wgb, wub, wdb):
    i = pl.program_id(0)
    c = pl.program_id(1)
    n_rows = nr_ref[i]

    @pl.when(n_rows > 0)
    def _():
        @pl.when(c == 0)
        def _():
            o_ref[...] = jnp.zeros(o_ref.shape, F32)

        def step(r0, rows, wg, wu, wd):
            live = r0 + lax.broadcasted_iota(jnp.int32, (rows, 1), 0) < n_rows
            xs = jnp.where(live, x_ref[pl.ds(r0, rows), :], 0.0).astype(BF16)
            t = (_silu(_dot(xs, wg)) * _dot(xs, wu)).astype(BF16)
            o_ref[pl.ds(r0, rows), :] += _dot(t, wd)

        def first_step(rows, keep):
            w = wg_ref[...].astype(BF16), wu_ref[...].astype(BF16), wd_ref[...].astype(BF16)
            if keep:
                wgb[...], wub[...], wdb[...] = w
            step(0, rows, *w)

        @pl.when(n_rows == ITEM_ROWS)
        def _():
            first_step(ITEM_ROWS, keep=False)

        @pl.when(n_rows < ITEM_ROWS)
        def _():
            whole = lax.shift_right_logical(n_rows, SUB_ROWS.bit_length() - 1)
            left = jnp.bitwise_and(n_rows, SUB_ROWS - 1)
            n_full = whole + (left > TAIL_ROWS).astype(jnp.int32)
            tail = (left > 0) & (left <= TAIL_ROWS)

            @pl.when(n_full > 0)
            def _():
                first_step(SUB_ROWS, keep=True)

            @pl.when(n_full == 0)
            def _():
                first_step(TAIL_ROWS, keep=False)

            def body(s, carry):
                step(pl.multiple_of(s * SUB_ROWS, SUB_ROWS), SUB_ROWS, wgb[...], wub[...], wdb[...])
                return carry

            lax.fori_loop(1, n_full, body, 0)

            @pl.when(tail & (n_full > 0))
            def _():
                step(pl.multiple_of(whole * SUB_ROWS, SUB_ROWS), TAIL_ROWS, wgb[...], wub[...], wdb[...])


def _moe(i_moe, item_e, item_blk, item_rows, x_sorted, wg, wu, wd):
    nc = D_FF_EXPERT // FF_CHUNK
    chunk = lambda i, c, nr: jnp.where(nr[i] > 0, c, nc - 1)
    grid_spec = pltpu.PrefetchScalarGridSpec(
        num_scalar_prefetch=3,
        grid=(N_ITEMS, nc),
        in_specs=[pl.BlockSpec((ITEM_ROWS, D), lambda i, c, ie, ib, nr: (ib[i], 0)),
                  pl.BlockSpec((None, None, D, FF_CHUNK),
                               lambda i, c, ie, ib, nr: (i_moe, ie[i], 0, chunk(i, c, nr))),
                  pl.BlockSpec((None, None, D, FF_CHUNK),
                               lambda i, c, ie, ib, nr: (i_moe, ie[i], 0, chunk(i, c, nr))),
                  pl.BlockSpec((None, None, FF_CHUNK, D),
                               lambda i, c, ie, ib, nr: (i_moe, ie[i], chunk(i, c, nr), 0))],
        out_specs=pl.BlockSpec((ITEM_ROWS, D), lambda i, c, ie, ib, nr: (ib[i], 0)),
        scratch_shapes=[pltpu.VMEM((D, FF_CHUNK), BF16), pltpu.VMEM((D, FF_CHUNK), BF16),
                        pltpu.VMEM((FF_CHUNK, D), BF16)],
    )
    return pl.pallas_call(
        _moe_kernel,
        grid_spec=grid_spec,
        out_shape=jax.ShapeDtypeStruct((N_EXPERTS * EXPERT_CAP, D), F32),
        compiler_params=_cparams(("arbitrary", "arbitrary")),
        name="moe_experts",
    )(item_e, item_blk, item_rows, x_sorted, wg, wu, wd)


def _post_kernel(final, slot_ref, ys_hbm, route_ref, x_ref, mod_ref, g_ref, *refs):
    buf, sem = refs[-2:]

    def row_copy(g, r, k):
        return pltpu.make_async_copy(ys_hbm.at[pl.ds(slot_ref[0, 16 * g + 2 * r + k], 1)],
                                     buf.at[k, g, pl.ds(r, 1), :], sem)

    def issue(g, carry):
        for r in range(8):
            row_copy(g, r, 0).start()
            row_copy(g, r, 1).start()
        return carry

    def drain(g, carry):
        for r in range(8):
            row_copy(g, r, 0).wait()
            row_copy(g, r, 1).wait()
        return carry

    lax.fori_loop(0, POST_TM // 8, issue, 0)
    lax.fori_loop(0, POST_TM // 8, drain, 0)
    r = route_ref[...]
    y = r[:, 2:3] * buf[0].reshape(POST_TM, D) + r[:, 3:4] * buf[1].reshape(POST_TM, D)
    m = mod_ref[...]
    out = x_ref[...] + m[5:6] * _rms(y, g_ref[...])
    if not final:
        refs[0][...] = out
    else:
        i = pl.program_id(0)

        @pl.when(i < T_CTX // POST_TM)
        def _():
            refs[0][...] = out

        @pl.when(i >= T_CTX // POST_TM)
        def _():
            refs[1][...] = out


def _post(l, slots, y_sorted, route, x1, mods, g_post, final):
    row = lambda i: (i, 0)
    nct = T_CTX // POST_TM
    if final:
        out_specs = [pl.BlockSpec((POST_TM, D), lambda i: (jnp.minimum(i, nct - 1), 0)),
                     pl.BlockSpec((POST_TM, D), lambda i: (jnp.maximum(i - nct, 0), 0))]
        out_shape = [jax.ShapeDtypeStruct((T_CTX, D), F32), jax.ShapeDtypeStruct((T_LAT, D), F32)]
    else:
        out_specs = pl.BlockSpec((POST_TM, D), row)
        out_shape = jax.ShapeDtypeStruct((T, D), F32)
    return pl.pallas_call(
        functools.partial(_post_kernel, final),
        grid=(T // POST_TM,),
        in_specs=[pl.BlockSpec((None, 1, 2 * POST_TM), lambda i: (i, 0, 0), memory_space=pltpu.SMEM),
                  pl.BlockSpec(memory_space=pl.ANY),
                  pl.BlockSpec((POST_TM, 128), row), pl.BlockSpec((POST_TM, D), row),
                  pl.BlockSpec((None, None, 6, D), lambda i: (l, _post_mod_row(i), 0, 0)),
                  pl.BlockSpec((None, 1, D), lambda i: (l, 0, 0))],
        out_specs=out_specs,
        out_shape=out_shape,
        scratch_shapes=[pltpu.VMEM((2, POST_TM // 8, 8, D), F32), pltpu.SemaphoreType.DMA(())],
        compiler_params=_cparams(("arbitrary",)),
        name="moe_post_final" if final else "moe_post",
    )(slots.reshape(T // POST_TM, 1, 2 * POST_TM), y_sorted, route, x1, mods, g_post)


def _route_plan(route, counts):
    i32 = jnp.int32
    slots = (route[:, 0:2].astype(i32) * EXPERT_CAP + route[:, 4:6].astype(i32)).reshape(-1)
    n = counts[0, :N_EXPERTS].astype(i32)
    n_items = (n + ITEM_ROWS - 1) // ITEM_ROWS
    ends = jnp.cumsum(n_items)
    base = ends - n_items
    total = ends[-1]
    ii = jnp.arange(N_ITEMS, dtype=i32)
    ii_c = jnp.minimum(ii, total - 1)
    item_e = jnp.sum((ends[None, :] <= ii_c[:, None]).astype(i32), axis=1)
    j = ii_c - base[item_e]
    item_blk = item_e * (EXPERT_CAP // ITEM_ROWS) + j
    rows = jnp.where(ii < total, jnp.clip(n[item_e] - j * ITEM_ROWS, 0, ITEM_ROWS), 0)
    return slots, item_e.astype(i32), item_blk.astype(i32), rows.astype(i32)


def _rope_tables(d):
    half, quarter = d // 2, d // 4
    t = jnp.arange(N_LAT)
    pr, pc = t // GRID_W, t % GRID_W
    i = jnp.arange(128) % d
    j = (i % half) % quarter
    inv = ROPE_THETA ** (-(2 * j).astype(F32) / half)
    pos = jnp.where((i < half)[None, :], pr[:, None], pc[:, None]).astype(F32)
    ang = pos * inv[None, :]
    low = ((i % half) < quarter)[None, :]
    sin = jnp.sin(ang)
    return jnp.cos(ang), jnp.where(low, -sin, 0.0), jnp.where(low, 0.0, sin)


def _na_bias_table(rpb):
    q = jnp.arange(GRID_W)
    k = jnp.arange(GRID_W)
    rel_c = jnp.clip(k[None, :] - q[:, None], -15, 15) + 15
    cs = jnp.clip(q - 8, 0, GRID_W - 16)
    valid = (k[None, :] >= cs[:, None]) & (k[None, :] < cs[:, None] + 16)
    pick = (rel_c[:, :, None] == jnp.arange(31)[None, None, :]).astype(F32)
    band = jnp.einsum('qkc,lhrc->lhrqk', pick, rpb, precision=lax.Precision.HIGHEST)
    band = jnp.where(valid[None, None, None], band, NEG_INF)
    return jnp.concatenate([band[:, :, :-1], band[:, :, 1:]], axis=-1)


def _pad_in_proj(w_in):
    wt = jnp.swapaxes(w_in, 1, 2)
    z = lambda n: jnp.zeros((DEPTH, n, D), w_in.dtype)
    s64, s32 = HD ** -0.5, (HD // 2) ** -0.5
    qa = wt[:, 0:256] * s64
    kpe = wt[:, 1600:1632]
    return jnp.concatenate([qa[:, 0:64], qa[:, 128:192], qa[:, 64:128], qa[:, 192:256],
                            wt[:, 256:512],
                            wt[:, 512:768] * s32, wt[:, 768:1280],
                            wt[:, 1472:1600], kpe, kpe, kpe, kpe,
                            wt[:, 1280:1472], z(64),
                            wt[:, 1632:1888] * s64, wt[:, 1888:2400]], axis=1).astype(BF16)


def _split_heads_cols(w, first, second):
    lead = w.shape[:-1]
    w4 = w.reshape(*lead, 4, first + second)
    return jnp.concatenate([w4[..., :first].reshape(*lead, 4 * first),
                            w4[..., first:].reshape(*lead, 4 * second)], axis=-1)


def kernel(x_prompt, x_sample, cache_win_k, cache_win_v, cache_diff_k, cache_diff_v, cache_mla_ckv, cache_mla_kpe, cache_na_k, cache_na_v, c, c_ctx, w_ada, b_ada, norm_mix_pre, norm_mix_post, norm_ffn_pre, norm_ffn_post, w_in, w_o, win_sink, diff_lq1, diff_lk1, diff_lq2, diff_lk2, diff_subln, mla_q_norm, mla_kv_norm, mla_wuq, mla_wukv, na_rpb, ffn_w_gate, ffn_w_up, ffn_w_down, moe_router, moe_w_gate, moe_w_up, moe_w_down):
    w_in_p = _pad_in_proj(w_in)
    wuq_p = jnp.pad(_split_heads_cols(mla_wuq, 64, 32) * (HD + HD // 2) ** -0.5,
                    ((0, 0), (0, 256 - MLA_Q_RANK), (0, 0))).astype(BF16)
    wukv_p = _split_heads_cols(mla_wukv, 64, 64).astype(BF16)
    qn_p = jnp.pad(mla_q_norm, ((0, 0), (0, 256 - MLA_Q_RANK))).reshape(DEPTH, 1, 256)
    kvn = mla_kv_norm.reshape(DEPTH, 1, 128)
    w_o_b = jnp.concatenate([w_o[:, 0:64], w_o[:, 128:192], w_o[:, 64:128], w_o[:, 192:]], axis=1).astype(BF16)
    ffn_wg, ffn_wu, ffn_wd = ffn_w_gate.astype(BF16), ffn_w_up.astype(BF16), ffn_w_down.astype(BF16)
    router_p = jnp.pad(moe_router, ((0, 0), (0, 0), (0, 128 - N_EXPERTS))).astype(BF16)
    g_mix_pre = norm_mix_pre.reshape(DEPTH, 1, D)
    g_mix_post = norm_mix_post.reshape(DEPTH, 1, D)
    g_ffn_pre = norm_ffn_pre.reshape(DEPTH, 1, D)
    g_ffn_post = norm_ffn_post.reshape(DEPTH, 1, D)
    tabs64 = _rope_tables(64)
    tabs32 = _rope_tables(32)
    na_bias = _na_bias_table(na_rpb)
    lam_init = jnp.asarray([0.8 - 0.6 * math.exp(-0.3 * l) for l in range(DEPTH)], F32)
    lam = (jnp.exp(jnp.sum(diff_lq1 * diff_lk1, axis=-1)) - jnp.exp(jnp.sum(diff_lq2 * diff_lk2, axis=-1))
           + lam_init)
    lam_tab = jnp.stack([lam, 1.0 - lam_init], axis=-1)
    subln = jnp.tile(diff_subln, (1, 2)).reshape(DEPTH, 1, 2 * HD)

    cond = jnp.concatenate([c_ctx[None], c, jnp.zeros((3, D), F32)], axis=0)
    mods = _mods(cond, w_ada, b_ada).reshape(DEPTH, 8, 6, D)

    tr = lambda a: jnp.swapaxes(a, -1, -2)
    cache_win_k, cache_win_v, cache_diff_k, cache_diff_v = map(tr, (cache_win_k, cache_win_v, cache_diff_k, cache_diff_v))
    cache_na_k, cache_na_v, cache_mla_kpe = map(tr, (cache_na_k, cache_na_v, cache_mla_kpe))

    xs = (x_prompt.reshape(T_CTX, D), x_sample.reshape(T_LAT, D))
    caches = ()
    for l in range(DEPTH):
        pa, pb, pc, pd = _in_proj(l, xs, mods, g_mix_pre, w_in_p, qn_p, kvn, wuq_p, wukv_p)
        o_ctx, *caches = _attn_ctx(l, win_sink[l], lam_tab[l], pa, pb, pc, pd, subln[l], caches)
        o_lat = [_lat_win(l, win_sink[l], pa, cache_win_k, cache_win_v, tabs64),
                 _lat_diff(l, lam_tab[l], pb, cache_diff_k, cache_diff_v, subln[l], tabs32),
                 _lat_mla(l, pc, cache_mla_ckv, cache_mla_kpe, wukv_p, tabs32),
                 _lat_na(l, pd, cache_na_k, cache_na_v, na_bias)]
        if l % 2 == 0:
            x1, h2 = _out_proj(l, o_ctx, o_lat, xs, mods, w_o_b, g_mix_post, g_ffn_pre, None)
            xs = (_ffn(l, l // 2, h2, x1, mods, g_ffn_post, ffn_wg, ffn_wu, ffn_wd),)
        else:
            x1, h2, route, counts = _out_proj(l, o_ctx, o_lat, xs, mods, w_o_b, g_mix_post, g_ffn_pre,
                                              router_p[l // 2])
            slots, item_e, item_blk, item_rows = _route_plan(route, counts)
            x_sorted = _dispatch(slots, h2)
            y_sorted = _moe(l // 2, item_e, item_blk, item_rows, x_sorted, moe_w_gate, moe_w_up, moe_w_down)
            xs = _post(l, slots, y_sorted, route, x1, mods, g_ffn_post, final=l == DEPTH - 1)
            xs = tuple(xs) if l == DEPTH - 1 else (xs,)

    y_prompt, y_sample = xs
    return (y_prompt.reshape(N_CTX_B, L_CTX, D), y_sample.reshape(N_LAT_B, N_LAT, D)) + tuple(caches)
```

```python
import functools
import math

import jax
import jax.numpy as jnp
from jax import lax
from jax.experimental import pallas as pl
from jax.experimental.pallas import tpu as pltpu

F32 = jnp.float32
BF16 = jnp.bfloat16

D = 1024
DEPTH = 4
N_CTX_B, L_CTX = 16, 256
N_LAT_B, N_LAT = 4, 1024
T_CTX = N_CTX_B * L_CTX
T_LAT = N_LAT_B * N_LAT
T = T_CTX + T_LAT
HD = 64
GRID_W = 64
WINDOW = 128
EPS = 1e-6
NEG_INF = -1e30
ROPE_THETA = 10000.0
MLA_Q_RANK = 192
D_FF = 2816
N_EXPERTS = 8
D_FF_EXPERT = 3584

PW = 2560
C_CKV, C_KPE, C_CQ, C_D = 1280, 1408, 1536, 1792
PC_W = 1152

TM = 512
EXPERT_CAP = T
ITEM_ROWS = 1024
SUB_ROWS = 512
TAIL_ROWS = 256
N_ITEMS = N_EXPERTS + (2 * T) // ITEM_ROWS
FF_CHUNK = 512
DISPATCH_TM = 1024
POST_TM = 1024
FFN_SLICES = (1024, 1024, 768)
VMEM_LIMIT = 56 * 1024 * 1024


def _cparams(sem):
    return pltpu.CompilerParams(dimension_semantics=sem, vmem_limit_bytes=VMEM_LIMIT)


def _rms(x, g):
    return x * lax.rsqrt(jnp.mean(x * x, axis=-1, keepdims=True) + EPS) * g


def _silu(a):
    return a / (1.0 + jnp.exp(-a))


def _dot(a, b):
    return jnp.dot(a, b, preferred_element_type=F32)


def _dot_t(a, b):
    return lax.dot_general(a, b, (((1,), (1,)), ((), ())), preferred_element_type=F32)


def _with_ones(v):
    return jnp.concatenate([v, jnp.ones(v.shape, v.dtype)], axis=1)


def _with_ones_t(vt):
    return jnp.concatenate([vt, jnp.ones(vt.shape, vt.dtype)], axis=0)


class _T:
    def __init__(self, a):
        self.a = a


def _scores(q, k):
    return _dot(q, k.a) if isinstance(k, _T) else _dot_t(q, k)


def _attend_scores(scores, vexts, extra_logit=None):
    mx = jnp.max(scores[0], axis=-1, keepdims=True)
    for s in scores[1:]:
        mx = jnp.maximum(mx, jnp.max(s, axis=-1, keepdims=True))
    if extra_logit is not None:
        mx = jnp.maximum(mx, extra_logit)
    r = None
    for s, v in zip(scores, vexts):
        p = jnp.exp(s - mx).astype(BF16)
        part = _dot_t(p, v.a) if isinstance(v, _T) else _dot(p, v)
        r = part if r is None else r + part
    den = r[:, 128:256]
    if extra_logit is not None:
        den = den + jnp.exp(extra_logit - mx)
    return r[:, 0:128] / den


def _attend(q, keys, vexts, extra_logit=None):
    return _attend_scores([_scores(q, k) for k in keys], vexts, extra_logit)


def _attend_diff(qp, keys, vexts, lam, low):
    n = qp.shape[0]
    q = jnp.concatenate([_keep_lanes(qp, (32 * i, 32)) for i in range(4)], axis=0)
    r = _attend(q, keys, vexts)
    return jnp.where(low, r[0:n] - lam * r[n:2 * n], r[2 * n:3 * n] - lam * r[3 * n:4 * n])


def _keep_lanes(x, *ranges):
    lane = lax.broadcasted_iota(jnp.int32, x.shape, 1)
    keep = None
    for lo, width in ranges:
        k = (lane >= lo) & (lane < lo + width)
        keep = k if keep is None else keep | k
    return jnp.where(keep, x, jnp.zeros_like(x))


def _low_half(n):
    return lax.broadcasted_iota(jnp.int32, (n, 128), 1) < HD


def _rms_halves(o, low, g):
    sq = o * o
    ms = jnp.where(low, jnp.sum(jnp.where(low, sq, 0.0), axis=-1, keepdims=True),
                   jnp.sum(jnp.where(low, 0.0, sq), axis=-1, keepdims=True)) / HD
    return o * lax.rsqrt(ms + EPS) * g


def _rope(x, c, sa, sb, shift):
    return x * c + pltpu.roll(x, 128 - shift, 1) * sa + pltpu.roll(x, shift, 1) * sb


def _mod_row(i):
    nct = T_CTX // TM
    per_b = N_LAT // TM
    return jnp.where(i >= nct, 1 + (i - nct) // per_b, 0)


def _mods_kernel(cond_ref, w_ref, b_ref, o_ref):
    a = _silu(cond_ref[...]).astype(BF16)
    o_ref[...] = _dot(a, w_ref[...].astype(BF16)) + b_ref[...]


def _mods(cond, w_ada, b_ada):
    tn = 1536
    n = 6 * D
    return pl.pallas_call(
        _mods_kernel,
        grid=(DEPTH, n // tn),
        in_specs=[pl.BlockSpec((8, D), lambda l, j: (0, 0)),
                  pl.BlockSpec((None, D, tn), lambda l, j: (l, 0, j)),
                  pl.BlockSpec((None, 1, tn), lambda l, j: (l, 0, j))],
        out_specs=pl.BlockSpec((None, 8, tn), lambda l, j: (l, 0, j)),
        out_shape=jax.ShapeDtypeStruct((DEPTH, 8, n), F32),
        compiler_params=_cparams(("arbitrary", "arbitrary")),
        name="mods",
    )(cond, w_ada, b_ada.reshape(DEPTH, 1, n))


def _x_specs(xs):
    if len(xs) == 1:
        return [pl.BlockSpec((TM, D), lambda i: (i, 0))]
    nct = T_CTX // TM
    return [pl.BlockSpec((TM, D), lambda i: (jnp.minimum(i, nct - 1), 0)),
            pl.BlockSpec((TM, D), lambda i: (jnp.maximum(i - nct, 0), 0))]


def _x_tile(x_refs):
    if len(x_refs) == 1:
        return x_refs[0][...]
    return jnp.where(pl.program_id(0) < T_CTX // TM, x_refs[0][...], x_refs[1][...])


def _in_proj_kernel(n_x, *refs):
    x_refs = refs[:n_x]
    mod_ref, g_ref, w_ref, qn_ref, kvn_ref, wuq_ref, wukv_ref, pa_ref, pb_ref, pc_ref, pd_ref = refs[n_x:]
    m = mod_ref[...]
    h = _rms(_x_tile(x_refs), g_ref[...]) * (1.0 + m[1:2]) + m[0:1]
    p = _dot_t(h.astype(BF16), w_ref[...])
    pa_ref[...] = p[:, 0:512].astype(BF16)
    pb_ref[...] = p[:, 512:1280].astype(BF16)
    pd_ref[...] = p[:, C_D:PW].astype(BF16)
    ckv_n = _rms(p[:, C_CKV:C_CKV + 128], kvn_ref[...])
    z = p[:, C_CQ:C_CQ + 256]
    cqn = z * lax.rsqrt(jnp.sum(z * z, axis=-1, keepdims=True) / MLA_Q_RANK + EPS) * qn_ref[...]
    qc = _dot(cqn.astype(BF16), wuq_ref[...])
    kv = _dot(ckv_n.astype(BF16), wukv_ref[...])
    pc_ref[:, 0:384] = qc.astype(BF16)
    pc_ref[:, 384:512] = p[:, C_KPE:C_KPE + 128].astype(BF16)
    pc_ref[:, 512:1024] = kv.astype(BF16)
    pc_ref[:, 1024:1152] = ckv_n.astype(BF16)


def _in_proj(l, xs, mods, g_pre, w_in_p, qn_p, kvn, wuq_p, wukv_p):
    row = lambda i: (i, 0)
    lyr2 = lambda i: (l, 0, 0)
    return pl.pallas_call(
        functools.partial(_in_proj_kernel, len(xs)),
        grid=(T // TM,),
        in_specs=_x_specs(xs) + [
                  pl.BlockSpec((None, None, 6, D), lambda i: (l, _mod_row(i), 0, 0)),
                  pl.BlockSpec((None, 1, D), lyr2),
                  pl.BlockSpec((None, PW, D), lyr2),
                  pl.BlockSpec((None, 1, 256), lyr2),
                  pl.BlockSpec((None, 1, 128), lyr2),
                  pl.BlockSpec((None, 256, 384), lyr2),
                  pl.BlockSpec((None, 128, 512), lyr2)],
        out_specs=[pl.BlockSpec((TM, 512), row), pl.BlockSpec((TM, 768), row),
                   pl.BlockSpec((TM, PC_W), row), pl.BlockSpec((TM, 768), row)],
        out_shape=[jax.ShapeDtypeStruct((T, 512), BF16), jax.ShapeDtypeStruct((T, 768), BF16),
                   jax.ShapeDtypeStruct((T, PC_W), BF16), jax.ShapeDtypeStruct((T, 768), BF16)],
        compiler_params=_cparams(("arbitrary",)),
        name="in_proj",
    )(*xs, mods, g_pre, w_in_p, qn_p, kvn, wuq_p, wukv_p)


def _attn_ctx_kernel(n_carried, sink_ref, lam_ref, pa, pb, pc, pd, subln_ref, *refs):
    o_ref, ka_o, va_o, kb_o, vb_o, ckv_o, kpe_o, kd_o, vd_o = refs[n_carried:]
    n = L_CTX
    low = _low_half(n)
    first = lax.broadcasted_iota(jnp.int32, (2 * n, 1), 0) < n
    ka, va = pa[:, 256:384], _with_ones(pa[:, 384:512])
    res = []
    for g in range(2):
        q = jnp.concatenate([_keep_lanes(pa[:, 0:128], (64 * g, 64)), _keep_lanes(pa[:, 128:256], (64 * g, 64))], axis=0)
        sink = jnp.where(first, sink_ref[2 * g], sink_ref[2 * g + 1])
        res.append(_attend(q, [ka], [va], sink))
    o_ref[:, 0:128] = jnp.where(low, res[0][0:n], res[1][0:n]).astype(BF16)
    o_ref[:, 128:256] = jnp.where(low, res[0][n:], res[1][n:]).astype(BF16)
    for g in range(2):
        ka_o[g] = pa[:, 256 + 64 * g:320 + 64 * g].astype(F32)
        va_o[g] = pa[:, 384 + 64 * g:448 + 64 * g].astype(F32)
    lam = lam_ref[0]
    out_scale = lam_ref[1]
    for j in range(2):
        o = _attend_diff(pb[:, 128 * j:128 * j + 128], [pb[:, 256 + 128 * j:384 + 128 * j]],
                         [_with_ones(pb[:, 512 + 128 * j:640 + 128 * j])], lam, low)
        o_ref[:, 256 + 128 * j:384 + 128 * j] = (_rms_halves(o, low, subln_ref[...]) * out_scale).astype(BF16)
    for h in range(4):
        kb_o[h] = pb[:, 256 + 64 * h:320 + 64 * h].astype(F32)
        vb_o[h] = pb[:, 512 + 64 * h:576 + 64 * h].astype(F32)
    kpe4 = pc[:, 384:512]
    for j in range(2):
        qn, kcat = pc[:, 128 * j:128 * j + 128], jnp.concatenate([pc[:, 512 + 128 * j:640 + 128 * j], kpe4], axis=1)
        qcat = jnp.concatenate([qn, pc[:, 256:384]], axis=1)
        q = jnp.concatenate([_keep_lanes(qcat, (64 * hh, 64), (128 + 32 * (2 * j + hh), 32)) for hh in range(2)], axis=0)
        r = _attend(q, [kcat], [_with_ones(pc[:, 768 + 128 * j:896 + 128 * j])])
        o_ref[:, 512 + 128 * j:640 + 128 * j] = jnp.where(low, r[0:n], r[n:]).astype(BF16)
    ckv_o[...] = pc[:, 1024:1152].astype(F32)
    kpe_o[...] = pc[:, 384:416].astype(F32)
    for j in range(2):
        qp = pd[:, 128 * j:128 * j + 128]
        q = jnp.concatenate([_keep_lanes(qp, (0, 64)), _keep_lanes(qp, (64, 64))], axis=0)
        r = _attend(q, [pd[:, 256 + 128 * j:384 + 128 * j]], [_with_ones(pd[:, 512 + 128 * j:640 + 128 * j])])
        o_ref[:, 768 + 128 * j:896 + 128 * j] = jnp.where(low, r[0:n], r[n:]).astype(BF16)
    for h in range(4):
        kd_o[h] = pd[:, 256 + 64 * h:320 + 64 * h].astype(F32)
        vd_o[h] = pd[:, 512 + 64 * h:576 + 64 * h].astype(F32)


def _attn_ctx(l, sink, lam, pa, pb, pc, pd, subln, carried):
    row = lambda b: (b, 0)
    smem = pl.BlockSpec(memory_space=pltpu.SMEM)
    hd4 = lambda n: pl.BlockSpec((None, None, n, L_CTX, HD), lambda b: (b, l, 0, 0, 0))
    hshape = lambda n: jax.ShapeDtypeStruct((N_CTX_B, DEPTH, n, L_CTX, HD), F32)
    n_in = 7
    return pl.pallas_call(
        functools.partial(_attn_ctx_kernel, len(carried)),
        grid=(N_CTX_B,),
        in_specs=[smem, smem,
                  pl.BlockSpec((L_CTX, 512), row), pl.BlockSpec((L_CTX, 768), row),
                  pl.BlockSpec((L_CTX, PC_W), row), pl.BlockSpec((L_CTX, 768), row),
                  pl.BlockSpec((1, 128), lambda b: (0, 0))] + [pl.BlockSpec(memory_space=pl.ANY)] * len(carried),
        out_specs=[pl.BlockSpec((L_CTX, D), row),
                   hd4(2), hd4(2), hd4(4), hd4(4),
                   pl.BlockSpec((None, None, L_CTX, 128), lambda b: (b, l, 0, 0)),
                   pl.BlockSpec((None, None, L_CTX, 32), lambda b: (b, l, 0, 0)),
                   hd4(4), hd4(4)],
        out_shape=[jax.ShapeDtypeStruct((T_CTX, D), BF16),
                   hshape(2), hshape(2), hshape(4), hshape(4),
                   jax.ShapeDtypeStruct((N_CTX_B, DEPTH, L_CTX, 128), F32),
                   jax.ShapeDtypeStruct((N_CTX_B, DEPTH, L_CTX, 32), F32),
                   hshape(4), hshape(4)],
        input_output_aliases={n_in + i: 1 + i for i in range(len(carried))},
        compiler_params=_cparams(("arbitrary",)),
        name="attn_ctx",
    )(sink, lam, pa, pb, pc, pd, subln, *carried)


_LAT_ROW = lambda b: (T_CTX // N_LAT + b, 0)


def _cache_spec(l, n):
    return pl.BlockSpec((None, None, n, HD, L_CTX), lambda b: (b, l, 0, 0, 0))


def _load_cache_pair(ck_ref, cv_ref, j, kcs, vcs):
    for hh in range(2):
        kcs[64 * hh:64 * hh + 64, :] = ck_ref[2 * j + hh].astype(BF16)
        vcs[64 * hh:64 * hh + 64, :] = cv_ref[2 * j + hh].astype(BF16)
    vcs[128:256, :] = jnp.ones((128, L_CTX), BF16)


def _tab_spec():
    return pl.BlockSpec((N_LAT, 128), lambda b: (0, 0))


def _lat_win_kernel(sink_ref, pa, ck_ref, cv_ref, c_ref, sa_ref, sb_ref, o_ref, qs, kpad, vpad, kcs, vcs):
    c, sa, sb = c_ref[...], sa_ref[...], sb_ref[...]
    for j in range(2):
        qs[:, 128 * j:128 * j + 128] = _rope(pa[:, 128 * j:128 * j + 128].astype(F32), c, sa, sb, 16).astype(BF16)
    zeros = jnp.zeros((WINDOW, 128), BF16)
    kpad[0:WINDOW, :] = zeros
    kpad[WINDOW:WINDOW + N_LAT, :] = _rope(pa[:, 256:384].astype(F32), c, sa, sb, 16).astype(BF16)
    kpad[WINDOW + N_LAT:, :] = zeros
    vpad[0:WINDOW, :] = zeros
    vpad[WINDOW:WINDOW + N_LAT, :] = pa[:, 384:512]
    vpad[WINDOW + N_LAT:, :] = zeros
    _load_cache_pair(ck_ref, cv_ref, 0, kcs, vcs)
    qi = lax.broadcasted_iota(jnp.int32, (2 * WINDOW, 3 * WINDOW), 0) % WINDOW
    kj = lax.broadcasted_iota(jnp.int32, (2 * WINDOW, 3 * WINDOW), 1)
    band = jnp.abs(kj - WINDOW - qi) <= WINDOW
    first_head = lax.broadcasted_iota(jnp.int32, (2 * WINDOW, 1), 0) < WINDOW
    low = _low_half(WINDOW)

    def body(nb, carry):
        r0 = pl.multiple_of(nb * WINDOW, WINDOW)
        kpos = r0 - WINDOW + kj
        valid = band & (kpos >= 0) & (kpos < N_LAT)
        kw = kpad[pl.ds(r0, 3 * WINDOW), :]
        vexts = [_with_ones(vpad[pl.ds(r0, 3 * WINDOW), :]), _T(vcs[...])]
        res = []
        for g in range(2):
            q = jnp.concatenate([_keep_lanes(qs[pl.ds(r0, WINDOW), 0:128], (64 * g, 64)),
                                 _keep_lanes(qs[pl.ds(r0, WINDOW), 128:256], (64 * g, 64))], axis=0)
            s_loc = jnp.where(valid, _dot_t(q, kw), NEG_INF)
            sink = jnp.where(first_head, sink_ref[2 * g], sink_ref[2 * g + 1])
            res.append(_attend_scores([s_loc, _dot(q, kcs[...])], vexts, sink))
        o_ref[pl.ds(r0, WINDOW), 0:128] = jnp.where(low, res[0][0:WINDOW], res[1][0:WINDOW]).astype(BF16)
        o_ref[pl.ds(r0, WINDOW), 128:256] = jnp.where(low, res[0][WINDOW:], res[1][WINDOW:]).astype(BF16)
        return carry

    lax.fori_loop(0, N_LAT // WINDOW, body, 0, unroll=4)


def _lat_win(l, sink, pa, ck, cv, tabs):
    return pl.pallas_call(
        _lat_win_kernel,
        grid=(N_LAT_B,),
        in_specs=[pl.BlockSpec(memory_space=pltpu.SMEM),
                  pl.BlockSpec((N_LAT, 512), _LAT_ROW),
                  _cache_spec(l, 2), _cache_spec(l, 2),
                  _tab_spec(), _tab_spec(), _tab_spec()],
        out_specs=pl.BlockSpec((N_LAT, 256), lambda b: (b, 0)),
        out_shape=jax.ShapeDtypeStruct((T_LAT, 256), BF16),
        scratch_shapes=[pltpu.VMEM((N_LAT, 256), BF16),
                        pltpu.VMEM((N_LAT + 2 * WINDOW, 128), BF16),
                        pltpu.VMEM((N_LAT + 2 * WINDOW, 128), BF16),
                        pltpu.VMEM((128, L_CTX), BF16),
                        pltpu.VMEM((256, L_CTX), BF16)],
        compiler_params=_cparams(("arbitrary",)),
        name="lat_win",
    )(sink, pa, ck, cv, *tabs)


QT = 256
DIFF_QT = 256
N_KEYS = N_LAT + L_CTX


def _lat_diff_kernel(lam_ref, pb, ck_ref, cv_ref, subln_ref, c_ref, sa_ref, sb_ref, o_ref, qs, klat, kcs, vcs):
    c, sa, sb = c_ref[...], sa_ref[...], sb_ref[...]
    for j in range(2):
        qs[:, 128 * j:128 * j + 128] = _rope(pb[:, 128 * j:128 * j + 128].astype(F32), c, sa, sb, 8).astype(BF16)
        klat[j] = _rope(pb[:, 256 + 128 * j:384 + 128 * j].astype(F32), c, sa, sb, 8).astype(BF16)
        _load_cache_pair(ck_ref, cv_ref, j, kcs.at[j], vcs.at[j])
    lam = lam_ref[0]
    out_scale = lam_ref[1]
    low = _low_half(DIFF_QT)
    n = DIFF_QT

    def body(t, carry):
        r0 = pl.multiple_of(t * n, n)
        for j in range(2):
            o = _attend_diff(qs[pl.ds(r0, n), 128 * j:128 * j + 128], [klat[j], _T(kcs[j])],
                             [_with_ones(pb[:, 512 + 128 * j:640 + 128 * j]), _T(vcs[j])], lam, low)
            o_ref[pl.ds(r0, n), 128 * j:128 * j + 128] = (_rms_halves(o, low, subln_ref[...]) * out_scale).astype(BF16)
        return carry

    lax.fori_loop(0, N_LAT // n, body, 0, unroll=2)


def _lat_diff(l, lam, pb, ck, cv, subln, tabs):
    return pl.pallas_call(
        _lat_diff_kernel,
        grid=(N_LAT_B,),
        in_specs=[pl.BlockSpec(memory_space=pltpu.SMEM),
                  pl.BlockSpec((N_LAT, 768), _LAT_ROW),
                  _cache_spec(l, 4), _cache_spec(l, 4),
                  pl.BlockSpec((1, 128), lambda b: (0, 0)),
                  _tab_spec(), _tab_spec(), _tab_spec()],
        out_specs=pl.BlockSpec((N_LAT, 256), lambda b: (b, 0)),
        out_shape=jax.ShapeDtypeStruct((T_LAT, 256), BF16),
        scratch_shapes=[pltpu.VMEM((N_LAT, 256), BF16),
                        pltpu.VMEM((2, N_LAT, 128), BF16),
                        pltpu.VMEM((2, 128, L_CTX), BF16),
                        pltpu.VMEM((2, 256, L_CTX), BF16)],
        compiler_params=_cparams(("arbitrary",)),
        name="lat_diff",
    )(lam, pb, ck, cv, subln, *tabs)


def _lat_mla_kernel(pc, ckv_ref, kpe_ref, wukv_ref, c_ref, sa_ref, sb_ref, o_ref, qcat, kcat, kvc, kpe4t):
    c, sa, sb = c_ref[...], sa_ref[...], sb_ref[...]
    qr = _rope(pc[:, 256:384].astype(F32), c, sa, sb, 8).astype(BF16)
    kpe_l = _rope(pc[:, 384:512].astype(F32), c, sa, sb, 8).astype(BF16)
    kvc[...] = _dot(ckv_ref[...].astype(BF16), wukv_ref[...]).astype(BF16)
    for i in range(4):
        kpe4t[32 * i:32 * i + 32, :] = kpe_ref[...].astype(BF16)
    for j in range(2):
        qcat[j, :, 0:128] = pc[:, 128 * j:128 * j + 128]
        qcat[j, :, 128:256] = qr
        kcat[j, :, 0:128] = pc[:, 512 + 128 * j:640 + 128 * j]
        kcat[j, :, 128:256] = kpe_l
    low = _low_half(QT)

    def body(t, carry):
        r0 = pl.multiple_of(t * QT, QT)
        for j in range(2):
            q2 = qcat[j, pl.ds(r0, QT), :]
            q = jnp.concatenate([_keep_lanes(q2, (64 * hh, 64), (128 + 32 * (2 * j + hh), 32)) for hh in range(2)],
                                axis=0)
            s_ctx = _dot_t(q[:, 0:128], kvc[:, 128 * j:128 * j + 128]) + _dot(q[:, 128:256], kpe4t[...])
            r = _attend_scores([_dot_t(q, kcat[j]), s_ctx],
                               [_with_ones(pc[:, 768 + 128 * j:896 + 128 * j]),
                                _with_ones(kvc[:, 256 + 128 * j:384 + 128 * j])])
            o_ref[pl.ds(r0, QT), 128 * j:128 * j + 128] = jnp.where(low, r[0:QT], r[QT:]).astype(BF16)
        return carry

    lax.fori_loop(0, N_LAT // QT, body, 0, unroll=2)


def _lat_mla(l, pc, ckv, kpe, wukv_p, tabs):
    return pl.pallas_call(
        _lat_mla_kernel,
        grid=(N_LAT_B,),
        in_specs=[pl.BlockSpec((N_LAT, PC_W), _LAT_ROW),
                  pl.BlockSpec((None, None, L_CTX, 128), lambda b: (b, l, 0, 0)),
                  pl.BlockSpec((None, None, 32, L_CTX), lambda b: (b, l, 0, 0)),
                  pl.BlockSpec((None, 128, 512), lambda b: (l, 0, 0)),
                  _tab_spec(), _tab_spec(), _tab_spec()],
        out_specs=pl.BlockSpec((N_LAT, 256), lambda b: (b, 0)),
        out_shape=jax.ShapeDtypeStruct((T_LAT, 256), BF16),
        scratch_shapes=[pltpu.VMEM((2, N_LAT, 256), BF16),
                        pltpu.VMEM((2, N_LAT, 256), BF16),
                        pltpu.VMEM((L_CTX, 512), BF16),
                        pltpu.VMEM((128, L_CTX), BF16)],
        compiler_params=_cparams(("arbitrary",)),
        name="lat_mla",
    )(pc, ckv, kpe, wukv_p, *tabs)


NA_ROWS = 8
NA_KEYS = NA_ROWS * GRID_W


def _lat_na_kernel(pd, ck_ref, cv_ref, bias_ref, o_ref, kcs, vcs):
    n_rows = N_LAT // GRID_W
    for j in range(2):
        _load_cache_pair(ck_ref, cv_ref, j, kcs.at[j], vcs.at[j])
    low = _low_half(GRID_W)

    def body(r, carry):
        rs = jnp.clip(r - NA_ROWS // 2, 0, n_rows - NA_ROWS)
        q0 = pl.multiple_of(r * GRID_W, GRID_W)
        k0 = pl.multiple_of(rs * GRID_W, GRID_W)
        first = NA_ROWS - 1 - (r - rs)
        for j in range(2):
            qp = pd[pl.ds(q0, GRID_W), 128 * j:128 * j + 128]
            q = jnp.concatenate([_keep_lanes(qp, (0, 64)), _keep_lanes(qp, (64, 64))], axis=0)
            bias = jnp.concatenate(
                [jnp.concatenate([bias_ref[2 * j + hh, first + 2 * m] for m in range(NA_ROWS // 2)], axis=1)
                 for hh in range(2)], axis=0)
            s_loc = _dot_t(q, pd[pl.ds(k0, NA_KEYS), 256 + 128 * j:384 + 128 * j]) + bias
            s_ctx = _dot(q, kcs[j])
            res = _attend_scores([s_loc, s_ctx],
                                 [_with_ones(pd[pl.ds(k0, NA_KEYS), 512 + 128 * j:640 + 128 * j]), _T(vcs[j])])
            o_ref[pl.ds(q0, GRID_W), 128 * j:128 * j + 128] = jnp.where(low, res[0:GRID_W], res[GRID_W:]).astype(BF16)
        return carry

    lax.fori_loop(0, n_rows, body, 0, unroll=4)


def _lat_na(l, pd, ck, cv, bias):
    return pl.pallas_call(
        _lat_na_kernel,
        grid=(N_LAT_B,),
        in_specs=[pl.BlockSpec((N_LAT, 768), _LAT_ROW),
                  _cache_spec(l, 4), _cache_spec(l, 4),
                  pl.BlockSpec((None, 4, 2 * NA_ROWS - 2, GRID_W, 2 * GRID_W), lambda b: (l, 0, 0, 0, 0))],
        out_specs=pl.BlockSpec((N_LAT, 256), lambda b: (b, 0)),
        out_shape=jax.ShapeDtypeStruct((T_LAT, 256), BF16),
        scratch_shapes=[pltpu.VMEM((2, 128, L_CTX), BF16), pltpu.VMEM((2, 256, L_CTX), BF16)],
        compiler_params=_cparams(("arbitrary",)),
        name="lat_na",
    )(pd, ck, cv, bias)


def _out_proj_kernel(with_router, n_x, *refs):
    x_refs = refs[:n_x]
    refs = refs[n_x:]
    if with_router:
        (oc_ref, oa_ref, ob_ref, om_ref, od_ref, mod_ref, wo_ref, gpost_ref, gffn_ref, r_ref,
         x1_ref, h2_ref, route_ref, cnt_ref) = refs
    else:
        (oc_ref, oa_ref, ob_ref, om_ref, od_ref, mod_ref, wo_ref, gpost_ref, gffn_ref,
         x1_ref, h2_ref) = refs
    i = pl.program_id(0)
    nct = T_CTX // TM

    if with_router:
        @pl.when(i == 0)
        def _():
            cnt_ref[...] = jnp.zeros(cnt_ref.shape, F32)

    def finish(y):
        m = mod_ref[...]
        x1 = _x_tile(x_refs) + m[2:3] * _rms(y, gpost_ref[...])
        x1_ref[...] = x1
        h2 = _rms(x1, gffn_ref[...]) * (1.0 + m[4:5]) + m[3:4]
        h2_ref[...] = h2.astype(h2_ref.dtype)
        if with_router:
            lg = _dot(h2.astype(BF16), r_ref[...])
            lane = lax.broadcasted_iota(jnp.int32, lg.shape, 1)
            lane_f = lane.astype(F32)
            lg = jnp.where(lane < N_EXPERTS, lg, -jnp.inf)
            m1 = jnp.max(lg, axis=-1, keepdims=True)
            i1 = jnp.min(jnp.where(lg == m1, lane_f, 128.0), axis=-1, keepdims=True)
            lg2 = jnp.where(lane_f == i1, -jnp.inf, lg)
            m2 = jnp.max(lg2, axis=-1, keepdims=True)
            i2 = jnp.min(jnp.where(lg2 == m2, lane_f, 128.0), axis=-1, keepdims=True)
            e = jnp.exp(m2 - m1)
            g1 = 1.0 / (1.0 + e)
            g2 = e / (1.0 + e)
            sel1 = lane_f == i1
            sel2 = lane_f == i2
            picked = jnp.where(sel1 | sel2, 1.0, 0.0)
            ri = lax.broadcasted_iota(jnp.int32, (TM, TM), 0)
            ci = lax.broadcasted_iota(jnp.int32, (TM, TM), 1)
            before = _dot(jnp.where(ci < ri, 1.0, 0.0).astype(BF16), picked.astype(BF16)) + cnt_ref[0:1, :]
            r1 = jnp.sum(jnp.where(sel1, before, 0.0), axis=-1, keepdims=True)
            r2 = jnp.sum(jnp.where(sel2, before, 0.0), axis=-1, keepdims=True)
            cnt_ref[...] = cnt_ref[...] + jnp.sum(picked, axis=0, keepdims=True)
            route = jnp.where(lane == 0, i1, jnp.where(lane == 1, i2, jnp.where(lane == 2, g1, 0.0)))
            route_ref[...] = jnp.where(lane == 3, g2, jnp.where(lane == 4, r1, jnp.where(lane == 5, r2, route)))

    @pl.when(i < nct)
    def _():
        finish(_dot(oc_ref[...], wo_ref[...]))

    @pl.when(i >= nct)
    def _():
        y = _dot(oa_ref[...], wo_ref[0:256, :])
        y = y + _dot(ob_ref[...], wo_ref[256:512, :])
        y = y + _dot(om_ref[...], wo_ref[512:768, :])
        y = y + _dot(od_ref[...], wo_ref[768:1024, :])
        finish(y)


def _out_proj(l, o_ctx, o_lat, xs, mods, w_o, g_post, g_ffn, router_p):
    nct = T_CTX // TM
    row = lambda i: (i, 0)
    lyr2 = lambda i: (l, 0, 0)
    lat = lambda i: (jnp.maximum(i - nct, 0), 0)
    with_router = router_p is not None
    in_specs = _x_specs(xs) + [
                pl.BlockSpec((TM, D), lambda i: (jnp.minimum(i, nct - 1), 0)),
                pl.BlockSpec((TM, 256), lat), pl.BlockSpec((TM, 256), lat),
                pl.BlockSpec((TM, 256), lat), pl.BlockSpec((TM, 256), lat),
                pl.BlockSpec((None, None, 6, D), lambda i: (l, _mod_row(i), 0, 0)),
                pl.BlockSpec((None, D, D), lyr2),
                pl.BlockSpec((None, 1, D), lyr2),
                pl.BlockSpec((None, 1, D), lyr2)]
    out_specs = [pl.BlockSpec((TM, D), row), pl.BlockSpec((TM, D), row)]
    out_shape = [jax.ShapeDtypeStruct((T, D), F32), jax.ShapeDtypeStruct((T, D), F32 if with_router else BF16)]
    args = [*xs, o_ctx, *o_lat, mods, w_o, g_post, g_ffn]
    if with_router:
        in_specs.append(pl.BlockSpec((D, 128), lambda i: (0, 0)))
        out_specs += [pl.BlockSpec((TM, 128), row), pl.BlockSpec((8, 128), lambda i: (0, 0))]
        out_shape += [jax.ShapeDtypeStruct((T, 128), F32), jax.ShapeDtypeStruct((8, 128), F32)]
        args.append(router_p)
    return pl.pallas_call(
        functools.partial(_out_proj_kernel, with_router, len(xs)),
        grid=(T // TM,),
        in_specs=in_specs, out_specs=out_specs, out_shape=out_shape,
        compiler_params=_cparams(("arbitrary",)),
        name="out_proj_router" if with_router else "out_proj",
    )(*args)


FFN_TM = 512


def _ffn_kernel(h_ref, x_ref, mod_ref, g_ref, wg_ref, wu_ref, wd_ref, o_ref):
    h = h_ref[...]
    acc = None
    lo = 0
    for width in FFN_SLICES:
        t = (_silu(_dot(h, wg_ref[:, lo:lo + width])) * _dot(h, wu_ref[:, lo:lo + width])).astype(BF16)
        part = _dot(t, wd_ref[lo:lo + width, :])
        acc = part if acc is None else acc + part
        lo += width
    m = mod_ref[...]
    o_ref[...] = x_ref[...] + m[5:6] * _rms(acc, g_ref[...])


def _ffn_mod_row(i):
    nct = T_CTX // FFN_TM
    return jnp.where(i >= nct, 1 + (i - nct) // (N_LAT // FFN_TM), 0)


def _post_mod_row(i):
    nct = T_CTX // POST_TM
    return jnp.where(i >= nct, 1 + (i - nct) // (N_LAT // POST_TM), 0)


def _ffn(l, i_dense, h2, x1, mods, g_post, wg, wu, wd):
    row = lambda i: (i, 0)
    resident = lambda shape: pl.BlockSpec((None,) + shape, lambda i: (i_dense, 0, 0), pipeline_mode=pl.Buffered(1))
    return pl.pallas_call(
        _ffn_kernel,
        grid=(T // FFN_TM,),
        in_specs=[pl.BlockSpec((FFN_TM, D), row),
                  pl.BlockSpec((FFN_TM, D), row),
                  pl.BlockSpec((None, None, 6, D), lambda i: (l, _ffn_mod_row(i), 0, 0)),
                  pl.BlockSpec((None, 1, D), lambda i: (l, 0, 0)),
                  resident((D, D_FF)), resident((D, D_FF)), resident((D_FF, D))],
        out_specs=pl.BlockSpec((FFN_TM, D), row),
        out_shape=jax.ShapeDtypeStruct((T, D), F32),
        compiler_params=_cparams(("arbitrary",)),
        name="ffn",
    )(h2, x1, mods, g_post, wg, wu, wd)


def _dispatch_kernel(slot_ref, h_ref, xs_hbm, sem):
    def row_copy(g, r, k):
        return pltpu.make_async_copy(h_ref.at[g, pl.ds(r, 1), :],
                                     xs_hbm.at[pl.ds(slot_ref[0, 16 * g + 2 * r + k], 1)], sem)

    def issue(g, carry):
        for r in range(8):
            row_copy(g, r, 0).start(priority=0)
            row_copy(g, r, 1).start(priority=1)
        return carry

    def drain(g, carry):
        for r in range(8):
            row_copy(g, r, 0).wait()
            row_copy(g, r, 1).wait()
        return carry

    lax.fori_loop(0, DISPATCH_TM // 8, issue, 0)
    lax.fori_loop(0, DISPATCH_TM // 8, drain, 0)


def _dispatch(slots, h2):
    return pl.pallas_call(
        _dispatch_kernel,
        grid=(T // DISPATCH_TM,),
        in_specs=[pl.BlockSpec((None, 1, 2 * DISPATCH_TM), lambda i: (i, 0, 0), memory_space=pltpu.SMEM),
                  pl.BlockSpec((DISPATCH_TM // 8, 8, D), lambda i: (i, 0, 0))],
        out_specs=pl.BlockSpec(memory_space=pl.ANY),
        out_shape=jax.ShapeDtypeStruct((N_EXPERTS * EXPERT_CAP, D), F32),
        scratch_shapes=[pltpu.SemaphoreType.DMA(())],
        compiler_params=_cparams(("arbitrary",)),
        name="moe_dispatch",
    )(slots.reshape(T // DISPATCH_TM, 1, 2 * DISPATCH_TM), h2.reshape(T // 8, 8, D))


def _moe_kernel(ie_ref, ib_ref, nr_ref, x_ref, wg_ref, wu_ref, wd_ref, o_ref, wgb, wub, wdb):
    i = pl.program_id(0)
    c = pl.program_id(1)
    n_rows = nr_ref[i]

    @pl.when(n_rows > 0)
    def _():
        @pl.when(c == 0)
        def _():
            o_ref[...] = jnp.zeros(o_ref.shape, F32)

        def step(r0, rows, wg, wu, wd):
            live = r0 + lax.broadcasted_iota(jnp.int32, (rows, 1), 0) < n_rows
            xs = jnp.where(live, x_ref[pl.ds(r0, rows), :], 0.0).astype(BF16)
            t = (_silu(_dot(xs, wg)) * _dot(xs, wu)).astype(BF16)
            o_ref[pl.ds(r0, rows), :] += _dot(t, wd)

        def first_step(rows, keep):
            w = wg_ref[...].astype(BF16), wu_ref[...].astype(BF16), wd_ref[...].astype(BF16)
            if keep:
                wgb[...], wub[...], wdb[...] = w
            step(0, rows, *w)

        @pl.when(n_rows == ITEM_ROWS)
        def _():
            first_step(ITEM_ROWS, keep=False)

        @pl.when(n_rows < ITEM_ROWS)
        def _():
            whole = lax.shift_right_logical(n_rows, SUB_ROWS.bit_length() - 1)
            left = jnp.bitwise_and(n_rows, SUB_ROWS - 1)
            n_full = whole + (left > TAIL_ROWS).astype(jnp.int32)
            tail = (left > 0) & (left <= TAIL_ROWS)

            @pl.when(n_full > 0)
            def _():
                first_step(SUB_ROWS, keep=True)

            @pl.when(n_full == 0)
            def _():
                first_step(TAIL_ROWS, keep=False)

            def body(s, carry):
                step(pl.multiple_of(s * SUB_ROWS, SUB_ROWS), SUB_ROWS, wgb[...], wub[...], wdb[...])
                return carry

            lax.fori_loop(1, n_full, body, 0)

            @pl.when(tail & (n_full > 0))
            def _():
                step(pl.multiple_of(whole * SUB_ROWS, SUB_ROWS), TAIL_ROWS, wgb[...], wub[...], wdb[...])


def _moe(i_moe, item_e, item_blk, item_rows, x_sorted, wg, wu, wd):
    nc = D_FF_EXPERT // FF_CHUNK
    chunk = lambda i, c, nr: jnp.where(nr[i] > 0, c, nc - 1)
    grid_spec = pltpu.PrefetchScalarGridSpec(
        num_scalar_prefetch=3,
        grid=(N_ITEMS, nc),
        in_specs=[pl.BlockSpec((ITEM_ROWS, D), lambda i, c, ie, ib, nr: (ib[i], 0)),
                  pl.BlockSpec((None, None, D, FF_CHUNK),
                               lambda i, c, ie, ib, nr: (i_moe, ie[i], 0, chunk(i, c, nr))),
                  pl.BlockSpec((None, None, D, FF_CHUNK),
                               lambda i, c, ie, ib, nr: (i_moe, ie[i], 0, chunk(i, c, nr))),
                  pl.BlockSpec((None, None, FF_CHUNK, D),
                               lambda i, c, ie, ib, nr: (i_moe, ie[i], chunk(i, c, nr), 0))],
        out_specs=pl.BlockSpec((ITEM_ROWS, D), lambda i, c, ie, ib, nr: (ib[i], 0)),
        scratch_shapes=[pltpu.VMEM((D, FF_CHUNK), BF16), pltpu.VMEM((D, FF_CHUNK), BF16),
                        pltpu.VMEM((FF_CHUNK, D), BF16)],
    )
    return pl.pallas_call(
        _moe_kernel,
        grid_spec=grid_spec,
        out_shape=jax.ShapeDtypeStruct((N_EXPERTS * EXPERT_CAP, D), F32),
        compiler_params=_cparams(("arbitrary", "arbitrary")),
        name="moe_experts",
    )(item_e, item_blk, item_rows, x_sorted, wg, wu, wd)


def _post_kernel(final, slot_ref, ys_hbm, route_ref, x_ref, mod_ref, g_ref, *refs):
    buf, sem = refs[-2:]

    def row_copy(g, r, k):
        return pltpu.make_async_copy(ys_hbm.at[pl.ds(slot_ref[0, 16 * g + 2 * r + k], 1)],
                                     buf.at[k, g, pl.ds(r, 1), :], sem)

    def issue(g, carry):
        for r in range(8):
            row_copy(g, r, 0).start(priority=0)
            row_copy(g, r, 1).start(priority=1)
        return carry

    def drain(g, carry):
        for r in range(8):
            row_copy(g, r, 0).wait()
            row_copy(g, r, 1).wait()
        return carry

    lax.fori_loop(0, POST_TM // 8, issue, 0)
    lax.fori_loop(0, POST_TM // 8, drain, 0)
    r = route_ref[...]
    y = r[:, 2:3] * buf[0].reshape(POST_TM, D) + r[:, 3:4] * buf[1].reshape(POST_TM, D)
    m = mod_ref[...]
    out = x_ref[...] + m[5:6] * _rms(y, g_ref[...])
    if not final:
        refs[0][...] = out
    else:
        i = pl.program_id(0)

        @pl.when(i < T_CTX // POST_TM)
        def _():
            refs[0][...] = out

        @pl.when(i >= T_CTX // POST_TM)
        def _():
            refs[1][...] = out


def _post(l, slots, y_sorted, route, x1, mods, g_post, final):
    row = lambda i: (i, 0)
    nct = T_CTX // POST_TM
    if final:
        out_specs = [pl.BlockSpec((POST_TM, D), lambda i: (jnp.minimum(i, nct - 1), 0)),
                     pl.BlockSpec((POST_TM, D), lambda i: (jnp.maximum(i - nct, 0), 0))]
        out_shape = [jax.ShapeDtypeStruct((T_CTX, D), F32), jax.ShapeDtypeStruct((T_LAT, D), F32)]
    else:
        out_specs = pl.BlockSpec((POST_TM, D), row)
        out_shape = jax.ShapeDtypeStruct((T, D), F32)
    return pl.pallas_call(
        functools.partial(_post_kernel, final),
        grid=(T // POST_TM,),
        in_specs=[pl.BlockSpec((None, 1, 2 * POST_TM), lambda i: (i, 0, 0), memory_space=pltpu.SMEM),
                  pl.BlockSpec(memory_space=pl.ANY),
                  pl.BlockSpec((POST_TM, 128), row), pl.BlockSpec((POST_TM, D), row),
                  pl.BlockSpec((None, None, 6, D), lambda i: (l, _post_mod_row(i), 0, 0)),
                  pl.BlockSpec((None, 1, D), lambda i: (l, 0, 0))],
        out_specs=out_specs,
        out_shape=out_shape,
        scratch_shapes=[pltpu.VMEM((2, POST_TM // 8, 8, D), F32), pltpu.SemaphoreType.DMA(())],
        compiler_params=_cparams(("arbitrary",)),
        name="moe_post_final" if final else "moe_post",
    )(slots.reshape(T // POST_TM, 1, 2 * POST_TM), y_sorted, route, x1, mods, g_post)


def _route_plan(route, counts):
    i32 = jnp.int32
    slots = (route[:, 0:2].astype(i32) * EXPERT_CAP + route[:, 4:6].astype(i32)).reshape(-1)
    n = counts[0, :N_EXPERTS].astype(i32)
    n_items = (n + ITEM_ROWS - 1) // ITEM_ROWS
    ends = jnp.cumsum(n_items)
    base = ends - n_items
    total = ends[-1]
    ii = jnp.arange(N_ITEMS, dtype=i32)
    ii_c = jnp.minimum(ii, total - 1)
    item_e = jnp.sum((ends[None, :] <= ii_c[:, None]).astype(i32), axis=1)
    j = ii_c - base[item_e]
    item_blk = item_e * (EXPERT_CAP // ITEM_ROWS) + j
    rows = jnp.where(ii < total, jnp.clip(n[item_e] - j * ITEM_ROWS, 0, ITEM_ROWS), 0)
    return slots, item_e.astype(i32), item_blk.astype(i32), rows.astype(i32)


def _rope_tables(d):
    half, quarter = d // 2, d // 4
    t = jnp.arange(N_LAT)
    pr, pc = t // GRID_W, t % GRID_W
    i = jnp.arange(128) % d
    j = (i % half) % quarter
    inv = ROPE_THETA ** (-(2 * j).astype(F32) / half)
    pos = jnp.where((i < half)[None, :], pr[:, None], pc[:, None]).astype(F32)
    ang = pos * inv[None, :]
    low = ((i % half) < quarter)[None, :]
    sin = jnp.sin(ang)
    return jnp.cos(ang), jnp.where(low, -sin, 0.0), jnp.where(low, 0.0, sin)


def _na_bias_table(rpb):
    q = jnp.arange(GRID_W)
    k = jnp.arange(GRID_W)
    rel_c = jnp.clip(k[None, :] - q[:, None], -15, 15) + 15
    cs = jnp.clip(q - 8, 0, GRID_W - 16)
    valid = (k[None, :] >= cs[:, None]) & (k[None, :] < cs[:, None] + 16)
    pick = (rel_c[:, :, None] == jnp.arange(31)[None, None, :]).astype(F32)
    band = jnp.einsum('qkc,lhrc->lhrqk', pick, rpb, precision=lax.Precision.HIGHEST)
    band = jnp.where(valid[None, None, None], band, NEG_INF)
    return jnp.concatenate([band[:, :, :-1], band[:, :, 1:]], axis=-1)


def _pad_in_proj(w_in):
    wt = jnp.swapaxes(w_in, 1, 2)
    z = lambda n: jnp.zeros((DEPTH, n, D), w_in.dtype)
    s64, s32 = HD ** -0.5, (HD // 2) ** -0.5
    qa = wt[:, 0:256] * s64
    kpe = wt[:, 1600:1632]
    return jnp.concatenate([qa[:, 0:64], qa[:, 128:192], qa[:, 64:128], qa[:, 192:256],
                            wt[:, 256:512],
                            wt[:, 512:768] * s32, wt[:, 768:1280],
                            wt[:, 1472:1600], kpe, kpe, kpe, kpe,
                            wt[:, 1280:1472], z(64),
                            wt[:, 1632:1888] * s64, wt[:, 1888:2400]], axis=1).astype(BF16)


def _split_heads_cols(w, first, second):
    lead = w.shape[:-1]
    w4 = w.reshape(*lead, 4, first + second)
    return jnp.concatenate([w4[..., :first].reshape(*lead, 4 * first),
                            w4[..., first:].reshape(*lead, 4 * second)], axis=-1)


def kernel(x_prompt, x_sample, cache_win_k, cache_win_v, cache_diff_k, cache_diff_v, cache_mla_ckv, cache_mla_kpe, cache_na_k, cache_na_v, c, c_ctx, w_ada, b_ada, norm_mix_pre, norm_mix_post, norm_ffn_pre, norm_ffn_post, w_in, w_o, win_sink, diff_lq1, diff_lk1, diff_lq2, diff_lk2, diff_subln, mla_q_norm, mla_kv_norm, mla_wuq, mla_wukv, na_rpb, ffn_w_gate, ffn_w_up, ffn_w_down, moe_router, moe_w_gate, moe_w_up, moe_w_down):
    w_in_p = _pad_in_proj(w_in)
    wuq_p = jnp.pad(_split_heads_cols(mla_wuq, 64, 32) * (HD + HD // 2) ** -0.5,
                    ((0, 0), (0, 256 - MLA_Q_RANK), (0, 0))).astype(BF16)
    wukv_p = _split_heads_cols(mla_wukv, 64, 64).astype(BF16)
    qn_p = jnp.pad(mla_q_norm, ((0, 0), (0, 256 - MLA_Q_RANK))).reshape(DEPTH, 1, 256)
    kvn = mla_kv_norm.reshape(DEPTH, 1, 128)
    w_o_b = jnp.concatenate([w_o[:, 0:64], w_o[:, 128:192], w_o[:, 64:128], w_o[:, 192:]], axis=1).astype(BF16)
    ffn_wg, ffn_wu, ffn_wd = ffn_w_gate.astype(BF16), ffn_w_up.astype(BF16), ffn_w_down.astype(BF16)
    router_p = jnp.pad(moe_router, ((0, 0), (0, 0), (0, 128 - N_EXPERTS))).astype(BF16)
    g_mix_pre = norm_mix_pre.reshape(DEPTH, 1, D)
    g_mix_post = norm_mix_post.reshape(DEPTH, 1, D)
    g_ffn_pre = norm_ffn_pre.reshape(DEPTH, 1, D)
    g_ffn_post = norm_ffn_post.reshape(DEPTH, 1, D)
    tabs64 = _rope_tables(64)
    tabs32 = _rope_tables(32)
    na_bias = _na_bias_table(na_rpb)
    lam_init = jnp.asarray([0.8 - 0.6 * math.exp(-0.3 * l) for l in range(DEPTH)], F32)
    lam = (jnp.exp(jnp.sum(diff_lq1 * diff_lk1, axis=-1)) - jnp.exp(jnp.sum(diff_lq2 * diff_lk2, axis=-1))
           + lam_init)
    lam_tab = jnp.stack([lam, 1.0 - lam_init], axis=-1)
    subln = jnp.tile(diff_subln, (1, 2)).reshape(DEPTH, 1, 2 * HD)

    cond = jnp.concatenate([c_ctx[None], c, jnp.zeros((3, D), F32)], axis=0)
    mods = _mods(cond, w_ada, b_ada).reshape(DEPTH, 8, 6, D)

    tr = lambda a: jnp.swapaxes(a, -1, -2)
    cache_win_k, cache_win_v, cache_diff_k, cache_diff_v = map(tr, (cache_win_k, cache_win_v, cache_diff_k, cache_diff_v))
    cache_na_k, cache_na_v, cache_mla_kpe = map(tr, (cache_na_k, cache_na_v, cache_mla_kpe))

    xs = (x_prompt.reshape(T_CTX, D), x_sample.reshape(T_LAT, D))
    caches = ()
    for l in range(DEPTH):
        pa, pb, pc, pd = _in_proj(l, xs, mods, g_mix_pre, w_in_p, qn_p, kvn, wuq_p, wukv_p)
        o_ctx, *caches = _attn_ctx(l, win_sink[l], lam_tab[l], pa, pb, pc, pd, subln[l], caches)
        o_lat = [_lat_win(l, win_sink[l], pa, cache_win_k, cache_win_v, tabs64),
                 _lat_diff(l, lam_tab[l], pb, cache_diff_k, cache_diff_v, subln[l], tabs32),
                 _lat_mla(l, pc, cache_mla_ckv, cache_mla_kpe, wukv_p, tabs32),
                 _lat_na(l, pd, cache_na_k, cache_na_v, na_bias)]
        if l % 2 == 0:
            x1, h2 = _out_proj(l, o_ctx, o_lat, xs, mods, w_o_b, g_mix_post, g_ffn_pre, None)
            xs = (_ffn(l, l // 2, h2, x1, mods, g_ffn_post, ffn_wg, ffn_wu, ffn_wd),)
        else:
            x1, h2, route, counts = _out_proj(l, o_ctx, o_lat, xs, mods, w_o_b, g_mix_post, g_ffn_pre,
                                              router_p[l // 2])
            slots, item_e, item_blk, item_rows = _route_plan(route, counts)
            x_sorted = _dispatch(slots, h2)
            y_sorted = _moe(l // 2, item_e, item_blk, item_rows, x_sorted, moe_w_gate, moe_w_up, moe_w_down)
            xs = _post(l, slots, y_sorted, route, x1, mods, g_ffn_post, final=l == DEPTH - 1)
            xs = tuple(xs) if l == DEPTH - 1 else (xs,)

    y_prompt, y_sample = xs
    return (y_prompt.reshape(N_CTX_B, L_CTX, D), y_sample.reshape(N_LAT_B, N_LAT, D)) + tuple(caches)
```
